```python
import math
import jax
import jax.numpy as jnp
from jax import lax
import numpy as np

D_MODEL = 1024
BATCH = 2
SEQ = 8192
DEPTH = 2
DEC_BATCH = 128
DEC_SEQ = 4
PAST_LEN = 8192
PAGE_SIZE = 128

CONV_WIDTH = 512
CONV_K = 3
N_HEADS = 8
N_KV_HEADS = 2
HEAD_DIM = 64
GROUP = N_HEADS // N_KV_HEADS
ATTN_WIDTH = N_HEADS * HEAD_DIM
KV_WIDTH = N_KV_HEADS * HEAD_DIM
WINDOW = 128
ATTN_BLOCK = 128
ROT_DIM = HEAD_DIM // 4
ROPE_THETA = 500000.0
SSM_HEADS = 16
SSM_HEAD_DIM = 64
SSM_WIDTH = SSM_HEADS * SSM_HEAD_DIM
SSM_GROUPS = 2
D_STATE = 128
SSM_CONV_K = 4
SSD_CHUNK = 128
XBC_WIDTH = SSM_WIDTH + 2 * SSM_GROUPS * D_STATE
N_BRANCH = 3
SPLIT_SIZES = (CONV_WIDTH, CONV_WIDTH, CONV_WIDTH, ATTN_WIDTH, KV_WIDTH, KV_WIDTH,
               SSM_WIDTH, XBC_WIDTH, SSM_HEADS, N_BRANCH * D_MODEL)
IN_WIDTH = sum(SPLIT_SIZES)
D_FF = 3584
N_EXPERTS = 8
TOP_K = 2
N_DENSE = (DEPTH + 1) // 2
N_MOE = DEPTH // 2
EPS = 1e-6

kernel_name = 'hybrid_conv_swa_ssd_decoder_step'


def _split_offsets():
    offs, o = [], 0
    for s in SPLIT_SIZES[:-1]:
        o += s
        offs.append(o)
    return offs


def rmsnorm(x, g):
    xf = x.astype(jnp.float32)
    xf = xf * lax.rsqrt(jnp.mean(xf * xf, axis=-1, keepdims=True) + EPS)
    return (xf * g.astype(jnp.float32)).astype(x.dtype)


def causal_dwconv(u, prev, w):
    full = jnp.concatenate([prev.astype(u.dtype), u], axis=1)
    y = lax.conv_general_dilated(full, w[:, None, :].astype(u.dtype), (1,), 'VALID',
                                 dimension_numbers=('NWC', 'WIO', 'NWC'),
                                 feature_group_count=u.shape[-1])
    return y, full[:, full.shape[1] - (w.shape[0] - 1):]


def rope_partial(x, pos):
    half = ROT_DIM // 2
    inv = jnp.exp(-(2.0 * jnp.arange(half, dtype=jnp.float32) / ROT_DIM) * math.log(ROPE_THETA))
    ang = pos.astype(jnp.float32)[:, None] * inv[None, :]
    cos = jnp.cos(ang)[None, :, None, :]
    sin = jnp.sin(ang)[None, :, None, :]
    xf = x.astype(jnp.float32)
    x1, x2, rest = xf[..., :half], xf[..., half:ROT_DIM], xf[..., ROT_DIM:]
    return jnp.concatenate([x1 * cos - x2 * sin, x2 * cos + x1 * sin, rest], axis=-1).astype(x.dtype)


def sink_attend(q, k, v, qpos, kpos, sinks):
    s = jnp.einsum('bnqkgd,bnskd->bnkgqs', q, k, preferred_element_type=jnp.float32) * (HEAD_DIM ** -0.5)
    diff = qpos[:, :, None] - kpos[:, None, :]
    mask = (diff >= 0) & (diff <= WINDOW) & (kpos[:, None, :] >= 0)
    s = jnp.where(mask[None, :, None, None], s, -1e30)
    sink = sinks.astype(jnp.float32).reshape(N_KV_HEADS, GROUP)[None, None, :, :, None, None]
    m = jnp.maximum(jnp.max(s, axis=-1, keepdims=True), sink)
    p = jnp.exp(s - m)
    p = p / (jnp.sum(p, axis=-1, keepdims=True) + jnp.exp(sink - m))
    return jnp.einsum('bnkgqs,bnskd->bnqkgd', p.astype(v.dtype), v)


def attn_prompt(q, k, v, pos, sinks):
    b, l = q.shape[:2]
    nb = l // ATTN_BLOCK
    qb = q.reshape(b, nb, ATTN_BLOCK, N_KV_HEADS, GROUP, HEAD_DIM)
    kb = k.reshape(b, nb, ATTN_BLOCK, N_KV_HEADS, HEAD_DIM)
    vb = v.reshape(b, nb, ATTN_BLOCK, N_KV_HEADS, HEAD_DIM)
    kk = jnp.concatenate([jnp.concatenate([jnp.zeros_like(kb[:, :1]), kb[:, :-1]], 1), kb], 2)
    vv = jnp.concatenate([jnp.concatenate([jnp.zeros_like(vb[:, :1]), vb[:, :-1]], 1), vb], 2)
    qpos = pos.reshape(nb, ATTN_BLOCK)
    kpos = jnp.concatenate([qpos - ATTN_BLOCK, qpos], axis=1)
    o = sink_attend(qb, kk, vv, qpos, kpos, sinks)
    return o.reshape(b, l, ATTN_WIDTH)


def attn_sample(q, k, v, pos, ck, cv, sinks):
    b, t = q.shape[:2]
    kall = jnp.concatenate([ck.astype(k.dtype), k], axis=1)
    vall = jnp.concatenate([cv.astype(v.dtype), v], axis=1)
    qpos = pos[None]
    kpos = (PAST_LEN - WINDOW + jnp.arange(WINDOW + t, dtype=jnp.int32))[None]
    o = sink_attend(q.reshape(b, 1, t, N_KV_HEADS, GROUP, HEAD_DIM), kall[:, None], vall[:, None], qpos, kpos, sinks)
    return o.reshape(b, t, ATTN_WIDTH), kall[:, t:], vall[:, t:]


def ssd_scan(x, dt, A, Bm, Cm, h0, chunk):
    b, l, H, P = x.shape
    G, N = Bm.shape[2], Bm.shape[3]
    R = H // G
    nc = l // chunk
    xc = x.reshape(b, nc, chunk, G, R, P)
    dtc = dt.reshape(b, nc, chunk, G, R)
    Bc = Bm.reshape(b, nc, chunk, G, N)
    Cc = Cm.reshape(b, nc, chunk, G, N)
    acum = jnp.cumsum(jnp.moveaxis(dtc * A.reshape(G, R), 2, -1), axis=-1)
    causal = jnp.tril(jnp.ones((chunk, chunk), dtype=bool))
    L = jnp.exp(jnp.where(causal, acum[..., :, None] - acum[..., None, :], -jnp.inf))
    xdt = xc * dtc[..., None]
    cb = jnp.einsum('bcqgn,bcsgn->bcgqs', Cc, Bc)
    y_diag = jnp.einsum('bcgrqs,bcsgrp->bcqgrp', cb[:, :, :, None] * L, xdt)
    decay_end = jnp.exp(acum[..., -1:] - acum)
    states = jnp.einsum('bcqgn,bcgrq,bcqgrp->bcgrpn', Bc, decay_end, xdt)
    chunk_decay = jnp.exp(acum[..., -1])

    def step(h, inp):
        s, d = inp
        return h * d[..., None, None] + s, h

    h_last, h_start = lax.scan(step, h0.reshape(b, G, R, P, N),
                               (jnp.moveaxis(states, 1, 0), jnp.moveaxis(chunk_decay, 1, 0)))
    h_start = jnp.moveaxis(h_start, 0, 1)
    y_off = jnp.einsum('bcqgn,bcgrpn,bcgrq->bcqgrp', Cc, h_start, jnp.exp(acum))
    return (y_diag + y_off).reshape(b, l, H, P), h_last.reshape(b, H, P, N)


def ssd_branch(z, xbc, dt_raw, conv_prev, h0, conv_w, conv_b, dt_bias, a_log, d_skip, norm_w):
    b, l = z.shape[:2]
    xbc_c, conv_new = causal_dwconv(xbc, conv_prev, conv_w)
    xbc_c = jax.nn.silu(xbc_c + conv_b.astype(xbc_c.dtype)).astype(jnp.float32)
    xs = xbc_c[..., :SSM_WIDTH].reshape(b, l, SSM_HEADS, SSM_HEAD_DIM)
    Bm = xbc_c[..., SSM_WIDTH:SSM_WIDTH + SSM_GROUPS * D_STATE].reshape(b, l, SSM_GROUPS, D_STATE)
    Cm = xbc_c[..., SSM_WIDTH + SSM_GROUPS * D_STATE:].reshape(b, l, SSM_GROUPS, D_STATE)
    dt = jax.nn.softplus(dt_raw.astype(jnp.float32) + dt_bias.astype(jnp.float32))
    A = -jnp.exp(a_log.astype(jnp.float32))
    chunk = SSD_CHUNK if l % SSD_CHUNK == 0 else l
    y, h_new = ssd_scan(xs, dt, A, Bm, Cm, h0.astype(jnp.float32), chunk)
    y = y + xs * d_skip.astype(jnp.float32)[:, None]
    y = y.reshape(b, l, SSM_WIDTH) * jax.nn.silu(z.astype(jnp.float32))
    yg = y.reshape(b, l, SSM_GROUPS, SSM_WIDTH // SSM_GROUPS)
    yg = yg * lax.rsqrt(jnp.mean(yg * yg, axis=-1, keepdims=True) + EPS)
    y = (yg.reshape(b, l, SSM_WIDTH) * norm_w.astype(jnp.float32)).astype(z.dtype)
    return y, conv_new, h_new.astype(z.dtype)


def swiglu(h, w_gate, w_up, w_down):
    return jnp.dot(jax.nn.silu(jnp.dot(h, w_gate)) * jnp.dot(h, w_up), w_down)


def moe_swiglu(h, w_router, w_gate, w_up, w_down):
    logits = jnp.dot(h, w_router).astype(jnp.float32)
    top_v, top_i = lax.top_k(logits, TOP_K)
    top_w = jax.nn.softmax(top_v, axis=-1)
    comb = jnp.sum(jax.nn.one_hot(top_i, N_EXPERTS, dtype=jnp.float32) * top_w[..., None], axis=-2).astype(h.dtype)
    out = jnp.zeros_like(h)
    for e in range(N_EXPERTS):
        out = out + comb[..., e:e + 1] * swiglu(h, w_gate[e], w_up[e], w_down[e])
    return out


def run_trunk(x, c, pos0, past, P):
    b, l, _ = x.shape
    pos = pos0 + jnp.arange(l, dtype=jnp.int32)
    offs = _split_offsets()
    nk, nv, ncv, nsc, nss = [], [], [], [], []
    for i in range(DEPTH):
        mod = jnp.dot(jax.nn.silu(c), P['w_mod'][i]) + P['b_mod'][i]
        sh1, sc1, g1, sh2, sc2, g2 = jnp.split(mod[:, None, :], 6, axis=-1)
        h = rmsnorm(x, P['norm_mix'][i]) * (1 + sc1) + sh1
        proj = jnp.dot(h, P['w_in'][i])
        cv_b, cv_c, cv_x, q, k, v, z, xbc, dt_raw, gates = jnp.split(proj, offs, axis=-1)
        if past is None:
            conv_prev = jnp.zeros((b, CONV_K - 1, CONV_WIDTH), x.dtype)
            ssm_conv_prev = jnp.zeros((b, SSM_CONV_K - 1, XBC_WIDTH), x.dtype)
            h0 = jnp.zeros((b, SSM_HEADS, SSM_HEAD_DIM, D_STATE), jnp.float32)
        else:
            conv_prev, ssm_conv_prev, h0 = past[2][i], past[3][i], past[4][i]
        conv_out, conv_new = causal_dwconv(cv_c * cv_x, conv_prev, P['w_sconv'][i])
        y_conv = cv_b * conv_out
        q = rope_partial(q.reshape(b, l, N_HEADS, HEAD_DIM), pos)
        k = rope_partial(k.reshape(b, l, N_KV_HEADS, HEAD_DIM), pos)
        v = v.reshape(b, l, N_KV_HEADS, HEAD_DIM)
        if past is None:
            y_attn = attn_prompt(q, k, v, pos, P['sinks'][i])
            k_new, v_new = k[:, l - WINDOW:], v[:, l - WINDOW:]
        else:
            y_attn, k_new, v_new = attn_sample(q, k, v, pos, past[0][i], past[1][i], P['sinks'][i])
        y_ssm, ssm_conv_new, h_new = ssd_branch(z, xbc, dt_raw, ssm_conv_prev, h0, P['ssm_conv_w'][i],
                                                P['ssm_conv_b'][i], P['dt_bias'][i], P['a_log'][i],
                                                P['d_skip'][i], P['ssm_norm'][i])
        g = jax.nn.sigmoid(gates.astype(jnp.float32)).astype(x.dtype).reshape(b, l, N_BRANCH, D_MODEL)
        merged = (g[:, :, 0] * jnp.dot(y_conv, P['w_br_conv'][i])
                  + g[:, :, 1] * jnp.dot(y_attn, P['w_br_attn'][i])
                  + g[:, :, 2] * jnp.dot(y_ssm, P['w_br_ssm'][i]))
        x = x + g1 * jnp.dot(merged, P['w_o'][i])
        h = rmsnorm(x, P['norm_ffn'][i]) * (1 + sc2) + sh2
        j = i // 2
        if i % 2 == 0:
            f = swiglu(h, P['ffn_w_gate'][j], P['ffn_w_up'][j], P['ffn_w_down'][j])
        else:
            f = moe_swiglu(h, P['router'][j], P['moe_w_gate'][j], P['moe_w_up'][j], P['moe_w_down'][j])
        x = x + g2 * f
        nk.append(k_new)
        nv.append(v_new)
        ncv.append(conv_new)
        nsc.append(ssm_conv_new)
        nss.append(h_new)
    y = rmsnorm(x, P['norm_final'])
    return y, (jnp.stack(nk), jnp.stack(nv), jnp.stack(ncv), jnp.stack(nsc), jnp.stack(nss))


def setup_inputs(seed: int = 0) -> dict:
    key = jax.random.key(seed)
    ks = iter(jax.random.split(key, 48))
    D = D_MODEL

    def nrm(shape, scale):
        return jax.random.normal(next(ks), shape, jnp.float32) * scale

    inp = {}
    inp['x_prompt'] = nrm((BATCH, SEQ, D), 1.0)
    inp['x_sample'] = nrm((DEC_BATCH, DEC_SEQ, D), 1.0)
    inp['c_prompt'] = nrm((BATCH, D), 1.0)
    inp['c_sample'] = nrm((DEC_BATCH, D), 1.0)
    inp['cache_k'] = nrm((DEPTH, DEC_BATCH, WINDOW, N_KV_HEADS, HEAD_DIM), 1.0)
    inp['cache_v'] = nrm((DEPTH, DEC_BATCH, WINDOW, N_KV_HEADS, HEAD_DIM), 1.0)
    inp['state_conv'] = nrm((DEPTH, DEC_BATCH, CONV_K - 1, CONV_WIDTH), 1.0)
    inp['state_ssm_conv'] = nrm((DEPTH, DEC_BATCH, SSM_CONV_K - 1, XBC_WIDTH), 1.0)
    inp['state_ssm'] = nrm((DEPTH, DEC_BATCH, SSM_HEADS, SSM_HEAD_DIM, D_STATE), 0.5)
    inp['w_mod'] = nrm((DEPTH, D, 6 * D), 0.3 * D ** -0.5)
    inp['b_mod'] = nrm((DEPTH, 6 * D), 0.02)
    inp['norm_mix'] = 1.0 + nrm((DEPTH, D), 0.05)
    inp['norm_ffn'] = 1.0 + nrm((DEPTH, D), 0.05)
    inp['norm_final'] = 1.0 + nrm((D,), 0.05)
    inp['w_in'] = nrm((DEPTH, D, IN_WIDTH), D ** -0.5)
    inp['w_sconv'] = nrm((DEPTH, CONV_K, CONV_WIDTH), CONV_K ** -0.5)
    inp['sinks'] = nrm((DEPTH, N_HEADS), 0.5)
    inp['ssm_conv_w'] = nrm((DEPTH, SSM_CONV_K, XBC_WIDTH), SSM_CONV_K ** -0.5)
    inp['ssm_conv_b'] = nrm((DEPTH, XBC_WIDTH), 0.02)
    dt0 = jnp.exp(jax.random.uniform(next(ks), (DEPTH, SSM_HEADS), jnp.float32,
                                     minval=math.log(1e-3), maxval=math.log(1e-1)))
    inp['dt_bias'] = dt0 + jnp.log(-jnp.expm1(-dt0))
    inp['a_log'] = jnp.log(jax.random.uniform(next(ks), (DEPTH, SSM_HEADS), jnp.float32, minval=1.0, maxval=16.0))
    inp['d_skip'] = 1.0 + nrm((DEPTH, SSM_HEADS), 0.1)
    inp['ssm_norm'] = 1.0 + nrm((DEPTH, SSM_WIDTH), 0.05)
    inp['w_br_conv'] = nrm((DEPTH, CONV_WIDTH, D), CONV_WIDTH ** -0.5)
    inp['w_br_attn'] = nrm((DEPTH, ATTN_WIDTH, D), ATTN_WIDTH ** -0.5)
    inp['w_br_ssm'] = nrm((DEPTH, SSM_WIDTH, D), SSM_WIDTH ** -0.5)
    inp['w_o'] = nrm((DEPTH, D, D), D ** -0.5)
    inp['ffn_w_gate'] = nrm((N_DENSE, D, D_FF), D ** -0.5)
    inp['ffn_w_up'] = nrm((N_DENSE, D, D_FF), D ** -0.5)
    inp['ffn_w_down'] = nrm((N_DENSE, D_FF, D), D_FF ** -0.5)
    inp['router'] = nrm((N_MOE, D, N_EXPERTS), D ** -0.5)
    inp['moe_w_gate'] = nrm((N_MOE, N_EXPERTS, D, D_FF), D ** -0.5)
    inp['moe_w_up'] = nrm((N_MOE, N_EXPERTS, D, D_FF), D ** -0.5)
    inp['moe_w_down'] = nrm((N_MOE, N_EXPERTS, D_FF, D), D_FF ** -0.5)
    return inp


def reference(x_prompt, x_sample, c_prompt, c_sample, cache_k, cache_v, state_conv, state_ssm_conv, state_ssm,
              w_mod, b_mod, norm_mix, norm_ffn, norm_final, w_in, w_sconv, sinks, ssm_conv_w, ssm_conv_b,
              dt_bias, a_log, d_skip, ssm_norm, w_br_conv, w_br_attn, w_br_ssm, w_o,
              ffn_w_gate, ffn_w_up, ffn_w_down, router, moe_w_gate, moe_w_up, moe_w_down):
    P = dict(w_mod=w_mod, b_mod=b_mod, norm_mix=norm_mix, norm_ffn=norm_ffn, norm_final=norm_final,
             w_in=w_in, w_sconv=w_sconv, sinks=sinks, ssm_conv_w=ssm_conv_w, ssm_conv_b=ssm_conv_b,
             dt_bias=dt_bias, a_log=a_log, d_skip=d_skip, ssm_norm=ssm_norm, w_br_conv=w_br_conv,
             w_br_attn=w_br_attn, w_br_ssm=w_br_ssm, w_o=w_o, ffn_w_gate=ffn_w_gate, ffn_w_up=ffn_w_up,
             ffn_w_down=ffn_w_down, router=router, moe_w_gate=moe_w_gate, moe_w_up=moe_w_up,
             moe_w_down=moe_w_down)
    y_prompt, (kp, vp, cp, scp, sp) = run_trunk(x_prompt, c_prompt, 0, None, P)
    y_sample, (ks, vs, cs, scs, ss) = run_trunk(
        x_sample, c_sample, PAST_LEN, (cache_k, cache_v, state_conv, state_ssm_conv, state_ssm), P)
    return (y_prompt, y_sample, kp, vp, cp, scp, sp, ks, vs, cs, scs, ss)
```

```python
import functools
import math

import jax
import jax.numpy as jnp
from jax import lax
from jax.experimental import pallas as pl
from jax.experimental.pallas import tpu as pltpu

F32 = jnp.float32
BF16 = jnp.bfloat16

PAST_LEN = 8192
ROPE_THETA = 500000.0
EPS = 1e-6
TOP_K = 2
SSM_GROUPS = 2
ATTN_BLOCK = 128
SSD_CHUNK = 128

LANES = 128
SUBLANES = 8
VMEM_LIMIT_BYTES = 56 * 1024 * 1024


def _cparams(*semantics):
    return pltpu.CompilerParams(dimension_semantics=semantics, vmem_limit_bytes=VMEM_LIMIT_BYTES)


def _tile(n, pref):
    if n <= pref:
        return n
    t = pref
    while n % t:
        t //= 2
    return t


def _silu(x):
    return x / (1.0 + jnp.exp(-x))


def _sigmoid(x):
    return 1.0 / (1.0 + jnp.exp(-x))


def _softplus(x):
    return jnp.maximum(x, 0.0) + jnp.log1p(jnp.exp(-jnp.abs(x)))


def _dot(a, b):
    return jnp.dot(a, b, preferred_element_type=F32)


def _dot_nt(a, b):
    return lax.dot_general(a, b, (((1,), (1,)), ((), ())), preferred_element_type=F32)


def _dot_tn(a, b):
    return lax.dot_general(a, b, (((0,), (0,)), ((), ())), preferred_element_type=F32)


def _split3(x):
    hi = x.astype(BF16)
    r1 = x - hi.astype(F32)
    mid = r1.astype(BF16)
    lo = (r1 - mid.astype(F32)).astype(BF16)
    return hi, mid, lo


def _dot_exact_rhs01(x, m01):
    hi, mid, lo = _split3(x)
    return _dot(hi, m01) + _dot(mid, m01) + _dot(lo, m01)


def _dot_exact_lhs01(m01, x):
    hi, mid, lo = _split3(x)
    return _dot(m01, hi) + _dot(m01, mid) + _dot(m01, lo)


def _rms_mod(x, norm_w, scale, shift):
    xn = x * lax.rsqrt(jnp.mean(x * x, axis=-1, keepdims=True) + EPS)
    return (xn * norm_w) * (1.0 + scale) + shift


def _mod_kernel(c_ref, w_ref, b_ref, o_ref):
    a = _silu(c_ref[...]).astype(BF16)
    o_ref[0] = _dot(a, w_ref[0].astype(BF16)) + b_ref[0]


def _modulation(c_all, w_mod, b_mod):
    depth, d, n = w_mod.shape
    rows = c_all.shape[0]
    tn = _tile(n, 1024)
    return pl.pallas_call(
        _mod_kernel,
        grid=(depth, n // tn),
        in_specs=[
            pl.BlockSpec((rows, d), lambda i, j: (0, 0)),
            pl.BlockSpec((1, d, tn), lambda i, j: (i, 0, j)),
            pl.BlockSpec((1, 1, tn), lambda i, j: (i, 0, j)),
        ],
        out_specs=pl.BlockSpec((1, rows, tn), lambda i, j: (i, 0, j)),
        out_shape=jax.ShapeDtypeStruct((depth, rows, n), F32),
        compiler_params=_cparams("arbitrary", "arbitrary"),
        name="modulation",
    )(c_all, w_mod, b_mod.reshape(depth, 1, n))


def _mod_spec(mod, tm, chunk, d):
    if mod.shape[1] == 1:
        return pl.BlockSpec((1, 1, d), lambda b, i, *_: (b, 0, chunk))
    return pl.BlockSpec((1, tm, d), lambda b, i, *_: (b, i, chunk))


def _inproj_kernel(x_ref, sh_ref, sc_ref, nw_ref, w_ref, o_ref, h_scr):
    @pl.when(pl.program_id(2) == 0)
    def _():
        h_scr[...] = _rms_mod(x_ref[0], nw_ref[...], sc_ref[0], sh_ref[0]).astype(BF16)

    o_ref[0] = _dot(h_scr[...], w_ref[...])


def _inproj(x, mod, norm_w, w):
    nb, l, d = x.shape
    n = w.shape[1]
    tm = _tile(l, 1024)
    tn = _tile(n, 1024)
    return pl.pallas_call(
        _inproj_kernel,
        grid=(nb, l // tm, n // tn),
        in_specs=[
            pl.BlockSpec((1, tm, d), lambda b, i, j: (b, i, 0)),
            _mod_spec(mod, tm, 0, d),
            _mod_spec(mod, tm, 1, d),
            pl.BlockSpec((1, d), lambda b, i, j: (0, 0)),
            pl.BlockSpec((d, tn), lambda b, i, j: (0, j)),
        ],
        out_specs=pl.BlockSpec((1, tm, tn), lambda b, i, j: (b, i, j)),
        out_shape=jax.ShapeDtypeStruct((nb, l, n), F32),
        scratch_shapes=[pltpu.VMEM((tm, d), BF16)],
        compiler_params=_cparams("arbitrary", "arbitrary", "arbitrary"),
        name="inproj",
    )(x, mod, mod, norm_w.reshape(1, d), w)


def _conv_prompt_kernel(p_ref, w_ref, y_ref, st_ref, ext, *, tm, cw):
    @pl.when(pl.program_id(1) == 0)
    def _():
        ext[0:SUBLANES, :] = jnp.zeros((SUBLANES, cw), F32)

    p = p_ref[0]
    gate_b, c, xx = p[:, :cw], p[:, cw:2 * cw], p[:, 2 * cw:]
    u = c * xx
    ext[SUBLANES:, :] = u
    w = w_ref[...]
    y = w[0:1] * ext[pl.ds(SUBLANES - 2, tm), :] + w[1:2] * ext[pl.ds(SUBLANES - 1, tm), :] + w[2:3] * u
    y_ref[0] = gate_b * y
    tail = u[tm - SUBLANES:, :]
    ext[0:SUBLANES, :] = tail
    st_ref[0] = tail


def _conv_prompt(proj, w, col_block, cw):
    nb, l, _ = proj.shape
    tm = _tile(l, 512)
    kern = functools.partial(_conv_prompt_kernel, tm=tm, cw=cw)
    return pl.pallas_call(
        kern,
        grid=(nb, l // tm),
        in_specs=[
            pl.BlockSpec((1, tm, 3 * cw), lambda b, i: (b, i, col_block)),
            pl.BlockSpec(w.shape, lambda b, i: (0, 0)),
        ],
        out_specs=[
            pl.BlockSpec((1, tm, cw), lambda b, i: (b, i, 0)),
            pl.BlockSpec((1, SUBLANES, cw), lambda b, i: (b, 0, 0)),
        ],
        out_shape=[
            jax.ShapeDtypeStruct((nb, l, cw), F32),
            jax.ShapeDtypeStruct((nb, SUBLANES, cw), F32),
        ],
        scratch_shapes=[pltpu.VMEM((tm + SUBLANES, cw), F32)],
        compiler_params=_cparams("arbitrary", "arbitrary"),
        name="conv_prompt",
    )(proj, w)


def _conv_sample_kernel(p_ref, st_ref, w_ref, y_ref, u_ref, *, nt, nbat, cw, k):
    w = w_ref[...]
    full = [st_ref[j] for j in range(k - 1)]
    gates = []
    for t in range(nt):
        p = p_ref[pl.ds(t * nbat, nbat), :]
        gates.append(p[:, :cw])
        u = p[:, cw:2 * cw] * p[:, 2 * cw:]
        u_ref[t] = u
        full.append(u)
    for t in range(nt):
        acc = w[0:1] * full[t]
        for j in range(1, k):
            acc = acc + w[j:j + 1] * full[t + j]
        y_ref[pl.ds(t * nbat, nbat), :] = gates[t] * acc


def _conv_sample(proj_conv, state_tm, w, nt, nbat, cw):
    k = w.shape[0]
    kern = functools.partial(_conv_sample_kernel, nt=nt, nbat=nbat, cw=cw, k=k)
    return pl.pallas_call(
        kern,
        out_shape=[
            jax.ShapeDtypeStruct((nt * nbat, cw), F32),
            jax.ShapeDtypeStruct((nt, nbat, cw), F32),
        ],
        compiler_params=pltpu.CompilerParams(vmem_limit_bytes=VMEM_LIMIT_BYTES),
        name="conv_sample",
    )(proj_conv, state_tm, w)


def _rope(x, cos, sin_lo, sin_hi, half_rot):
    return (x * cos + pltpu.roll(x, LANES - half_rot, 1) * sin_lo
            + pltpu.roll(x, half_rot, 1) * sin_hi)


def _attn_core(q, kcat, vcat, sinks_ref, valid, cos, sin_lo, sin_hi, *, n_heads, group, head_dim):
    tq = q.shape[0]
    half_rot = head_dim // 8
    heads_per_slab = LANES // head_dim
    scale = head_dim ** -0.5
    lane = lax.broadcasted_iota(jnp.int32, (tq, LANES), 1)
    k_bf = [kcat.astype(BF16), pltpu.roll(kcat, head_dim, 1).astype(BF16)]
    v_bf = [vcat.astype(BF16), pltpu.roll(vcat, head_dim, 1).astype(BF16)]
    slabs = []
    for s in range(n_heads // heads_per_slab):
        qs = _rope(q[:, s * LANES:(s + 1) * LANES], cos, sin_lo, sin_hi, half_rot)
        out = jnp.zeros((tq, LANES), F32)
        for half in range(heads_per_slab):
            h = s * heads_per_slab + half
            g = h // group
            in_head = (lane >= half * head_dim) & (lane < (half + 1) * head_dim)
            qm = jnp.where(in_head, qs, 0.0).astype(BF16)
            swap = 0 if (g % heads_per_slab) == half else 1
            sc = _dot_nt(qm, k_bf[swap]) * scale
            sc = jnp.where(valid, sc, -1e30)
            sink = sinks_ref[h]
            m = jnp.maximum(jnp.max(sc, axis=-1, keepdims=True), sink)
            p = jnp.exp(sc - m)
            p = p / (jnp.sum(p, axis=-1, keepdims=True) + jnp.exp(sink - m))
            o = _dot(p.astype(BF16), v_bf[swap])
            out = jnp.where(in_head, o, out)
        slabs.append(out)
    return slabs


def _attn_prompt_kernel(sinks_ref, q_ref, kv_ref, cos_ref, slo_ref, shi_ref, y_ref, last_ref, kprev, vprev,
                        *, n_heads, group, head_dim):
    j = pl.program_id(1)
    tq = q_ref.shape[1]

    @pl.when(j == 0)
    def _():
        kprev[...] = jnp.zeros_like(kprev)
        vprev[...] = jnp.zeros_like(vprev)

    cos, slo, shi = cos_ref[...], slo_ref[...], shi_ref[...]
    kv = kv_ref[0]
    k_rot = _rope(kv[:, :LANES], cos, slo, shi, head_dim // 8)
    v = kv[:, LANES:]
    kcat = jnp.concatenate([kprev[...], k_rot], axis=0)
    vcat = jnp.concatenate([vprev[...], v], axis=0)
    nk = kcat.shape[0]
    r = lax.broadcasted_iota(jnp.int32, (tq, nk), 0)
    c = lax.broadcasted_iota(jnp.int32, (tq, nk), 1)
    first_key = jnp.where(j > 0, 0, tq)
    valid = (c >= r) & (c <= r + tq) & (c >= first_key)
    slabs = _attn_core(q_ref[0], kcat, vcat, sinks_ref, valid, cos, slo, shi,
                       n_heads=n_heads, group=group, head_dim=head_dim)
    for s, o in enumerate(slabs):
        y_ref[0, :, s * LANES:(s + 1) * LANES] = o
    kprev[...] = k_rot
    vprev[...] = v
    last_ref[0, :, :LANES] = k_rot
    last_ref[0, :, LANES:] = v


def _attn_prompt(proj, sinks, tables, q_block, kv_block, n_heads, n_kv, head_dim):
    nb, l, _ = proj.shape
    tq = ATTN_BLOCK
    qw = n_heads * head_dim
    kvw = 2 * n_kv * head_dim
    assert n_kv * head_dim == LANES
    kern = functools.partial(_attn_prompt_kernel, n_heads=n_heads, group=n_heads // n_kv, head_dim=head_dim)
    tab_spec = pl.BlockSpec((tq, LANES), lambda b, j: (j, 0))
    return pl.pallas_call(
        kern,
        grid=(nb, l // tq),
        in_specs=[
            pl.BlockSpec(memory_space=pltpu.SMEM),
            pl.BlockSpec((1, tq, qw), lambda b, j: (b, j, q_block)),
            pl.BlockSpec((1, tq, kvw), lambda b, j: (b, j, kv_block)),
            tab_spec, tab_spec, tab_spec,
        ],
        out_specs=[
            pl.BlockSpec((1, tq, qw), lambda b, j: (b, j, 0)),
            pl.BlockSpec((1, tq, kvw), lambda b, j: (b, 0, 0)),
        ],
        out_shape=[
            jax.ShapeDtypeStruct((nb, l, qw), F32),
            jax.ShapeDtypeStruct((nb, tq, kvw), F32),
        ],
        scratch_shapes=[pltpu.VMEM((tq, LANES), F32), pltpu.VMEM((tq, LANES), F32)],
        compiler_params=_cparams("arbitrary", "arbitrary"),
        name="attn_prompt",
    )(sinks, proj, proj, *tables)


def _attn_sample_kernel(sinks_ref, q_ref, kv_ref, ck_ref, cv_ref, cos_ref, slo_ref, shi_ref, y_ref, knew_ref,
                        *, n_heads, group, head_dim):
    tq = q_ref.shape[1]
    window = ck_ref.shape[1]
    cos, slo, shi = cos_ref[...], slo_ref[...], shi_ref[...]
    kv = kv_ref[0]
    k_rot = _rope(kv[:, :LANES], cos, slo, shi, head_dim // 8)
    kcat = jnp.concatenate([ck_ref[0], k_rot], axis=0)
    vcat = jnp.concatenate([cv_ref[0], kv[:, LANES:]], axis=0)
    nk = kcat.shape[0]
    r = lax.broadcasted_iota(jnp.int32, (tq, nk), 0)
    c = lax.broadcasted_iota(jnp.int32, (tq, nk), 1)
    valid = (c >= r) & (c <= r + window)
    slabs = _attn_core(q_ref[0], kcat, vcat, sinks_ref, valid, cos, slo, shi,
                       n_heads=n_heads, group=group, head_dim=head_dim)
    for s, o in enumerate(slabs):
        y_ref[0, :, s * LANES:(s + 1) * LANES] = o
    knew_ref[0] = k_rot


def _attn_sample(q_bm, kv_bm, ck, cv, sinks, tables, n_heads, n_kv, head_dim):
    nbat, tq, qw = q_bm.shape
    window = ck.shape[1]
    kern = functools.partial(_attn_sample_kernel, n_heads=n_heads, group=n_heads // n_kv, head_dim=head_dim)
    tab_spec = pl.BlockSpec((tq, LANES), lambda b: (0, 0))
    return pl.pallas_call(
        kern,
        grid=(nbat,),
        in_specs=[
            pl.BlockSpec(memory_space=pltpu.SMEM),
            pl.BlockSpec((1, tq, qw), lambda b: (b, 0, 0)),
            pl.BlockSpec((1, tq, 2 * LANES), lambda b: (b, 0, 0)),
            pl.BlockSpec((1, window, LANES), lambda b: (b, 0, 0)),
            pl.BlockSpec((1, window, LANES), lambda b: (b, 0, 0)),
            tab_spec, tab_spec, tab_spec,
        ],
        out_specs=[
            pl.BlockSpec((1, tq, qw), lambda b: (b, 0, 0)),
            pl.BlockSpec((1, tq, LANES), lambda b: (b, 0, 0)),
        ],
        out_shape=[
            jax.ShapeDtypeStruct((nbat, tq, qw), F32),
            jax.ShapeDtypeStruct((nbat, tq, LANES), F32),
        ],
        compiler_params=_cparams("arbitrary"),
        name="attn_sample",
    )(sinks, q_bm, kv_bm, ck, cv, *tables)


def _rope_tables(pos, head_dim):
    rot = head_dim // 4
    half = rot // 2
    inv = jnp.exp(-(2.0 * jnp.arange(half, dtype=F32) / rot) * math.log(ROPE_THETA))
    ang = pos.astype(F32)[:, None] * inv[None, :]
    cos, sin = jnp.cos(ang), jnp.sin(ang)
    n = pos.shape[0]
    pad = jnp.zeros((n, head_dim - rot), F32)
    zeros = jnp.zeros((n, half), F32)
    cos_h = jnp.concatenate([cos, cos, pad + 1.0], axis=1)
    lo_h = jnp.concatenate([-sin, zeros, pad], axis=1)
    hi_h = jnp.concatenate([zeros, sin, pad], axis=1)
    reps = LANES // head_dim
    return tuple(jnp.tile(t, (1, reps)) for t in (cos_h, lo_h, hi_h))


def _gated_group_norm(y, z, norm_w, groups):
    y = y * _silu(z)
    gw = y.shape[1] // groups
    parts = []
    for g in range(groups):
        yg = y[:, g * gw:(g + 1) * gw]
        parts.append(yg * lax.rsqrt(jnp.mean(yg * yg, axis=-1, keepdims=True) + EPS))
    return jnp.concatenate(parts, axis=1) * norm_w


def _ssd_prompt_kernel(xbc_ref, z_ref, dt_ref, cw_ref, cb_ref, dtb_ref, alog_ref, dsk_ref, nw_ref, e_ref,
                       y_ref, hfin_ref, ext, ht, *, q, width, dstate, hdim, groups):
    j = pl.program_id(1)
    kconv = cw_ref.shape[0]

    @pl.when(j == 0)
    def _():
        ext[0:SUBLANES, :] = jnp.zeros((SUBLANES, ext.shape[1]), F32)
        ht[...] = jnp.zeros_like(ht)

    xbc = xbc_ref[0]
    ext[SUBLANES:, :] = xbc
    cw = cw_ref[...]
    conv = cw[kconv - 1:kconv] * xbc + cb_ref[...]
    for t in range(kconv - 1):
        conv = conv + cw[t:t + 1] * ext[pl.ds(SUBLANES - (kconv - 1) + t, q), :]
    ext[0:SUBLANES, :] = xbc[q - SUBLANES:, :]
    act = _silu(conv)
    xs = act[:, :width]
    bm = act[:, width:width + groups * dstate]
    cm = act[:, width + groups * dstate:]

    expand = e_ref[...]
    dt_h = _softplus(dt_ref[0] + dtb_ref[...])
    a_h = dt_h * (-jnp.exp(alog_ref[...]))
    row = lax.broadcasted_iota(jnp.int32, (q, q), 0)
    col = lax.broadcasted_iota(jnp.int32, (q, q), 1)
    causal = col <= row
    tri = jnp.where(causal, 1.0, 0.0).astype(BF16)
    acum_h = _dot_exact_lhs01(tri, a_h)
    acum_ht = acum_h.T
    dt_x = _dot_exact_rhs01(dt_h, expand)
    acum_x = _dot_exact_rhs01(acum_h, expand)
    xdt = xs * dt_x
    acum_last = acum_x[q - 1:q, :]
    xd = xdt * jnp.exp(acum_last - acum_x)
    chunk_decay = jnp.exp(acum_last)
    exp_acum = jnp.exp(acum_x)

    lane = lax.broadcasted_iota(jnp.int32, (q, LANES), 1)
    gw = width // groups
    heads_per_group = gw // hdim
    pair = LANES // hdim
    y_parts = []
    for g in range(groups):
        b_g = bm[:, g * dstate:(g + 1) * dstate].astype(BF16)
        c_g = cm[:, g * dstate:(g + 1) * dstate].astype(BF16)
        cbm = _dot_nt(c_g, b_g)
        h_g = ht[:, g * gw:(g + 1) * gw]
        y_off = _dot(c_g, h_g.astype(BF16)) * exp_acum[:, g * gw:(g + 1) * gw]
        diag_parts = []
        for jp in range(heads_per_group // pair):
            l0 = g * gw + jp * LANES
            x_pair = xdt[:, l0:l0 + LANES].astype(BF16)
            out = jnp.zeros((q, LANES), F32)
            for half in range(pair):
                hd = g * heads_per_group + jp * pair + half
                decay = jnp.exp(jnp.where(causal, acum_h[:, hd:hd + 1] - acum_ht[hd:hd + 1, :], -jnp.inf))
                res = _dot((cbm * decay).astype(BF16), x_pair)
                in_head = (lane >= half * hdim) & (lane < (half + 1) * hdim)
                out = jnp.where(in_head, res, out)
            diag_parts.append(out)
        y_parts.append(jnp.concatenate(diag_parts, axis=1) + y_off)
        s_t = _dot_tn(b_g, xd[:, g * gw:(g + 1) * gw].astype(BF16))
        ht[:, g * gw:(g + 1) * gw] = h_g * chunk_decay[:, g * gw:(g + 1) * gw] + s_t
    y = jnp.concatenate(y_parts, axis=1) + xs * dsk_ref[...]
    y_ref[0] = _gated_group_norm(y, z_ref[0], nw_ref[...], groups)

    @pl.when(j == pl.num_programs(1) - 1)
    def _():
        hfin_ref[0] = ht[...].T


def _ssd_prompt(proj, blocks, params, dims):
    nb, l, _ = proj.shape
    q = SSD_CHUNK
    width, dstate, hdim, groups, xbcw = dims
    xbc_block, z_block, dt_block = blocks
    cw, cb, dtb, alog, dsk, nw, expand = params
    kern = functools.partial(_ssd_prompt_kernel, q=q, width=width, dstate=dstate, hdim=hdim, groups=groups)
    full = lambda a: pl.BlockSpec(a.shape, lambda b, j: (0,) * a.ndim)
    return pl.pallas_call(
        kern,
        grid=(nb, l // q),
        in_specs=[
            pl.BlockSpec((1, q, xbcw), lambda b, j: (b, j, xbc_block)),
            pl.BlockSpec((1, q, width), lambda b, j: (b, j, z_block)),
            pl.BlockSpec((1, q, LANES), lambda b, j: (b, j, dt_block)),
            full(cw), full(cb), full(dtb), full(alog), full(dsk), full(nw), full(expand),
        ],
        out_specs=[
            pl.BlockSpec((1, q, width), lambda b, j: (b, j, 0)),
            pl.BlockSpec((1, width, dstate), lambda b, j: (b, 0, 0)),
        ],
        out_shape=[
            jax.ShapeDtypeStruct((nb, l, width), F32),
            jax.ShapeDtypeStruct((nb, width, dstate), F32),
        ],
        scratch_shapes=[pltpu.VMEM((q + SUBLANES, xbcw), F32), pltpu.VMEM((dstate, width), F32)],
        compiler_params=_cparams("arbitrary", "arbitrary"),
        name="ssd_prompt",
    )(proj, proj, proj, cw, cb, dtb, alog, dsk, nw, expand)


def _ssd_sample_pre_kernel(xbc_ref, dt_ref, st_ref, cw_ref, cb_ref, dtb_ref, alog_ref, dsk_ref, e_ref,
                           ypart_ref, expa_ref, xd_ref, b_ref, c_ref, cd_ref,
                           *, nt, nbat, width, dstate, groups):
    kconv = cw_ref.shape[0]
    cw = cw_ref[...]
    expand = e_ref[...]
    neg_a = -jnp.exp(alog_ref[...])
    full = [st_ref[t] for t in range(kconv - 1)]
    for t in range(nt):
        full.append(xbc_ref[pl.ds(t * nbat, nbat), :])
    xs, bm, cm, dt_x, acum_x, xdt = [], [], [], [], [], []
    acum_h = None
    for t in range(nt):
        conv = cb_ref[...] + cw[0:1] * full[t]
        for jj in range(1, kconv):
            conv = conv + cw[jj:jj + 1] * full[t + jj]
        act = _silu(conv)
        xs.append(act[:, :width])
        bm.append(act[:, width:width + groups * dstate])
        cm.append(act[:, width + groups * dstate:])
        dt_h = _softplus(dt_ref[pl.ds(t * nbat, nbat), :] + dtb_ref[...])
        a_h = dt_h * neg_a
        acum_h = a_h if acum_h is None else acum_h + a_h
        dt_x.append(_dot_exact_rhs01(dt_h, expand))
        acum_x.append(_dot_exact_rhs01(acum_h, expand))
        xdt.append(xs[t] * dt_x[t])
    cd_ref[...] = jnp.exp(acum_h)
    gw = width // groups
    for t in range(nt):
        y = xs[t] * dsk_ref[...]
        for s in range(t + 1):
            cb_parts = []
            for g in range(groups):
                prod = cm[t][:, g * dstate:(g + 1) * dstate] * bm[s][:, g * dstate:(g + 1) * dstate]
                cb_parts.append(jnp.broadcast_to(jnp.sum(prod, axis=-1, keepdims=True), (nbat, gw)))
            cb_x = jnp.concatenate(cb_parts, axis=1)
            y = y + cb_x * jnp.exp(acum_x[t] - acum_x[s]) * xdt[s]
        ypart_ref[t] = y
        expa_ref[t] = jnp.exp(acum_x[t])
        xd_ref[t] = xdt[t] * jnp.exp(acum_x[nt - 1] - acum_x[t])
        b_ref[t] = bm[t]
        c_ref[t] = cm[t]


def _ssd_sample_pre(xbc_tm, dt_tm, state_tm, params, nt, nbat, dims):
    width, dstate, hdim, groups, xbcw = dims
    cw, cb, dtb, alog, dsk, expand = params
    kern = functools.partial(_ssd_sample_pre_kernel, nt=nt, nbat=nbat, width=width, dstate=dstate, groups=groups)
    sd = jax.ShapeDtypeStruct
    return pl.pallas_call(
        kern,
        out_shape=[
            sd((nt, nbat, width), F32), sd((nt, nbat, width), F32), sd((nt, nbat, width), F32),
            sd((nt, nbat, groups * dstate), F32), sd((nt, nbat, groups * dstate), F32),
            sd((nbat, LANES), F32),
        ],
        compiler_params=pltpu.CompilerParams(vmem_limit_bytes=VMEM_LIMIT_BYTES),
        name="ssd_sample_pre",
    )(xbc_tm, dt_tm, state_tm, cw, cb, dtb, alog, dsk, expand)


def _ssd_sample_state_kernel(cd_ref, c_ref, b_ref, xd_ref, h0_ref, yoff_ref, hnew_ref,
                             *, heads, hdim, dstate, groups):
    b = pl.program_id(0)
    hpg = heads // groups
    gw = hpg * hdim
    for g in range(groups):
        hm = h0_ref[0, g * hpg:(g + 1) * hpg].reshape(gw, dstate)
        c_g = c_ref[0, :, g * dstate:(g + 1) * dstate].astype(BF16)
        b_g = b_ref[0, :, g * dstate:(g + 1) * dstate].astype(BF16)
        yoff_ref[0, :, g * gw:(g + 1) * gw] = _dot_nt(c_g, hm.astype(BF16))
        upd = _dot_tn(xd_ref[0, :, g * gw:(g + 1) * gw].astype(BF16), b_g)
        for hh in range(hpg):
            hd = g * hpg + hh
            hnew_ref[0, hd] = h0_ref[0, hd] * cd_ref[b * heads + hd] + upd[hh * hdim:(hh + 1) * hdim, :]


def _ssd_sample_state(cd_flat, c_bm, b_bm, xd_bm, h0, dims):
    width, dstate, hdim, groups, _ = dims
    nbat, heads = h0.shape[0], h0.shape[1]
    rows = c_bm.shape[1]
    kern = functools.partial(_ssd_sample_state_kernel, heads=heads, hdim=hdim, dstate=dstate, groups=groups)
    return pl.pallas_call(
        kern,
        grid=(nbat,),
        in_specs=[
            pl.BlockSpec(memory_space=pltpu.SMEM),
            pl.BlockSpec((1, rows, groups * dstate), lambda b: (b, 0, 0)),
            pl.BlockSpec((1, rows, groups * dstate), lambda b: (b, 0, 0)),
            pl.BlockSpec((1, rows, width), lambda b: (b, 0, 0)),
            pl.BlockSpec((1, heads, hdim, dstate), lambda b: (b, 0, 0, 0)),
        ],
        out_specs=[
            pl.BlockSpec((1, rows, width), lambda b: (b, 0, 0)),
            pl.BlockSpec((1, heads, hdim, dstate), lambda b: (b, 0, 0, 0)),
        ],
        out_shape=[
            jax.ShapeDtypeStruct((nbat, rows, width), F32),
            jax.ShapeDtypeStruct(h0.shape, F32),
        ],
        compiler_params=_cparams("arbitrary"),
        name="ssd_sample_state",
    )(cd_flat, c_bm, b_bm, xd_bm, h0)


def _ssd_sample_post_kernel(ypart_ref, yoff_ref, expa_ref, z_ref, nw_ref, y_ref, *, groups):
    y = ypart_ref[...] + yoff_ref[...] * expa_ref[...]
    y_ref[...] = _gated_group_norm(y, z_ref[...], nw_ref[...], groups)


def _ssd_sample_post(ypart, yoff, expa, z, nw, groups):
    kern = functools.partial(_ssd_sample_post_kernel, groups=groups)
    return pl.pallas_call(
        kern,
        out_shape=jax.ShapeDtypeStruct(ypart.shape, F32),
        compiler_params=pltpu.CompilerParams(vmem_limit_bytes=VMEM_LIMIT_BYTES),
        name="ssd_sample_post",
    )(ypart, yoff, expa, z, nw)


def _merge_kernel(x_ref, g_ref, yc_ref, ya_ref, ys_ref, g1_ref, wc_ref, wa_ref, ws_ref, wo_ref, o_ref, *, d):
    gates = g_ref[0]
    merged = (_sigmoid(gates[:, :d]) * _dot(yc_ref[0].astype(BF16), wc_ref[...])
              + _sigmoid(gates[:, d:2 * d]) * _dot(ya_ref[0].astype(BF16), wa_ref[...])
              + _sigmoid(gates[:, 2 * d:]) * _dot(ys_ref[0].astype(BF16), ws_ref[...]))
    o_ref[0] = x_ref[0] + g1_ref[0] * _dot(merged.astype(BF16), wo_ref[...])


def _merge(x, proj, gate_block, yc, ya, ys, mod, wc, wa, ws, wo):
    nb, l, d = x.shape
    tm = _tile(l, 256)
    kern = functools.partial(_merge_kernel, d=d)
    tok = lambda w: pl.BlockSpec((1, tm, w), lambda b, i: (b, i, 0))
    full = lambda a: pl.BlockSpec(a.shape, lambda b, i: (0, 0))
    return pl.pallas_call(
        kern,
        grid=(nb, l // tm),
        in_specs=[
            tok(d),
            pl.BlockSpec((1, tm, 3 * d), lambda b, i: (b, i, gate_block)),
            tok(yc.shape[2]), tok(ya.shape[2]), tok(ys.shape[2]),
            _mod_spec(mod, tm, 2, d),
            full(wc), full(wa), full(ws), full(wo),
        ],
        out_specs=tok(d),
        out_shape=jax.ShapeDtypeStruct((nb, l, d), F32),
        compiler_params=_cparams("arbitrary", "arbitrary"),
        name="merge",
    )(x, proj, yc, ya, ys, mod, wc, wa, ws, wo)


def _finish(x, gate, f, nf_ref, final_norm):
    out = x + gate * f
    if final_norm:
        out = out * lax.rsqrt(jnp.mean(out * out, axis=-1, keepdims=True) + EPS) * nf_ref[...]
    return out


def _ffn_kernel(x_ref, sh_ref, sc_ref, g2_ref, nw_ref, nf_ref, wg_ref, wu_ref, wd_ref, o_ref, h_scr, acc,
                *, final_norm):
    f = pl.program_id(2)

    @pl.when(f == 0)
    def _():
        h_scr[...] = _rms_mod(x_ref[0], nw_ref[...], sc_ref[0], sh_ref[0]).astype(BF16)
        acc[...] = jnp.zeros_like(acc)

    h = h_scr[...]
    a = _silu(_dot(h, wg_ref[...])) * _dot(h, wu_ref[...])
    acc[...] += _dot(a.astype(BF16), wd_ref[...])

    @pl.when(f == pl.num_programs(2) - 1)
    def _():
        o_ref[0] = _finish(x_ref[0], g2_ref[0], acc[...], nf_ref, final_norm)


def _ffn(x, mod, norm_w, norm_final, wg, wu, wd, final_norm):
    nb, l, d = x.shape
    ff = wg.shape[1]
    tm = _tile(l, 1024)
    tf = _tile(ff, 512)
    kern = functools.partial(_ffn_kernel, final_norm=final_norm)
    vec = pl.BlockSpec((1, d), lambda b, i, f: (0, 0))
    return pl.pallas_call(
        kern,
        grid=(nb, l // tm, ff // tf),
        in_specs=[
            pl.BlockSpec((1, tm, d), lambda b, i, f: (b, i, 0)),
            _mod_spec(mod, tm, 3, d), _mod_spec(mod, tm, 4, d), _mod_spec(mod, tm, 5, d),
            vec, vec,
            pl.BlockSpec((d, tf), lambda b, i, f: (0, f)),
            pl.BlockSpec((d, tf), lambda b, i, f: (0, f)),
            pl.BlockSpec((tf, d), lambda b, i, f: (f, 0)),
        ],
        out_specs=pl.BlockSpec((1, tm, d), lambda b, i, f: (b, i, 0)),
        out_shape=jax.ShapeDtypeStruct((nb, l, d), F32),
        scratch_shapes=[pltpu.VMEM((tm, d), BF16), pltpu.VMEM((tm, d), F32)],
        compiler_params=_cparams("arbitrary", "arbitrary", "arbitrary"),
        name="ffn",
    )(x, mod, mod, mod, norm_w.reshape(1, d), norm_final.reshape(1, d), wg, wu, wd)


def _route_top2(logits, n_experts):
    lane = lax.broadcasted_iota(jnp.int32, logits.shape, 1).astype(F32)
    neg = -jnp.inf
    lg = jnp.where(lane < n_experts, logits, neg)
    m1 = jnp.max(lg, axis=-1, keepdims=True)
    i1 = jnp.min(jnp.where(lg == m1, lane, float(LANES)), axis=-1, keepdims=True)
    rest = jnp.where(lane == i1, neg, lg)
    m2 = jnp.max(rest, axis=-1, keepdims=True)
    i2 = jnp.min(jnp.where(rest == m2, lane, float(LANES)), axis=-1, keepdims=True)
    e2 = jnp.exp(m2 - m1)
    w1 = 1.0 / (1.0 + e2)
    w2 = e2 / (1.0 + e2)
    return jnp.where(lane == i1, w1, 0.0) + jnp.where(lane == i2, w2, 0.0)


def _moe_kernel(x_ref, sh_ref, sc_ref, g2_ref, nw_ref, nf_ref, wr_ref, wg_ref, wu_ref, wd_ref, o_ref,
                h_scr, comb, acc_e, acc, *, n_experts, final_norm):
    e = pl.program_id(2)
    f = pl.program_id(3)
    last_f = pl.num_programs(3) - 1

    @pl.when((e == 0) & (f == 0))
    def _():
        h = _rms_mod(x_ref[0], nw_ref[...], sc_ref[0], sh_ref[0])
        h_scr[...] = h.astype(BF16)
        h_hi = h.astype(BF16)
        h_lo = (h - h_hi.astype(F32)).astype(BF16)
        wr = wr_ref[...]
        r_hi = wr.astype(BF16)
        r_lo = (wr - r_hi.astype(F32)).astype(BF16)
        logits = _dot(h_hi, r_hi) + _dot(h_lo, r_hi) + _dot(h_hi, r_lo)
        comb[...] = _route_top2(logits, n_experts)
        acc[...] = jnp.zeros_like(acc)

    @pl.when(f == 0)
    def _():
        acc_e[...] = jnp.zeros_like(acc_e)

    h = h_scr[...]
    a = _silu(_dot(h, wg_ref[0])) * _dot(h, wu_ref[0])
    acc_e[...] += _dot(a.astype(BF16), wd_ref[0])

    @pl.when(f == last_f)
    def _():
        lane = lax.broadcasted_iota(jnp.int32, comb.shape, 1)
        w_e = jnp.sum(jnp.where(lane == e, comb[...], 0.0), axis=-1, keepdims=True)
        acc[...] += w_e * acc_e[...]

    @pl.when((e == n_experts - 1) & (f == last_f))
    def _():
        o_ref[0] = _finish(x_ref[0], g2_ref[0], acc[...], nf_ref, final_norm)


def _moe(x, mod, norm_w, norm_final, w_router_pad, wg, wu, wd, final_norm):
    nb, l, d = x.shape
    n_experts, _, ff = wg.shape
    tm = _tile(l, 1024)
    tf = _tile(ff, 512)
    kern = functools.partial(_moe_kernel, n_experts=n_experts, final_norm=final_norm)
    vec = pl.BlockSpec((1, d), lambda b, i, e, f: (0, 0))
    return pl.pallas_call(
        kern,
        grid=(nb, l // tm, n_experts, ff // tf),
        in_specs=[
            pl.BlockSpec((1, tm, d), lambda b, i, e, f: (b, i, 0)),
            _mod_spec(mod, tm, 3, d), _mod_spec(mod, tm, 4, d), _mod_spec(mod, tm, 5, d),
            vec, vec,
            pl.BlockSpec((d, LANES), lambda b, i, e, f: (0, 0)),
            pl.BlockSpec((1, d, tf), lambda b, i, e, f: (e, 0, f)),
            pl.BlockSpec((1, d, tf), lambda b, i, e, f: (e, 0, f)),
            pl.BlockSpec((1, tf, d), lambda b, i, e, f: (e, f, 0)),
        ],
        out_specs=pl.BlockSpec((1, tm, d), lambda b, i, e, f: (b, i, 0)),
        out_shape=jax.ShapeDtypeStruct((nb, l, d), F32),
        scratch_shapes=[pltpu.VMEM((tm, d), BF16), pltpu.VMEM((tm, LANES), F32),
                        pltpu.VMEM((tm, d), F32), pltpu.VMEM((tm, d), F32)],
        compiler_params=_cparams("arbitrary", "arbitrary", "arbitrary", "arbitrary"),
        name="moe",
    )(x, mod, mod, mod, norm_w.reshape(1, d), norm_final.reshape(1, d), w_router_pad, wg, wu, wd)


def _pad_rows(a, rows):
    return jnp.pad(a, ((0, 0), (0, rows - a.shape[1]), (0, 0)))


def _to_batch_major(a_tm, nt, nbat, rows):
    w = a_tm.shape[-1]
    return _pad_rows(a_tm.reshape(nt, nbat, w).transpose(1, 0, 2), rows)


def kernel(x_prompt, x_sample, c_prompt, c_sample, cache_k, cache_v, state_conv, state_ssm_conv, state_ssm,
           w_mod, b_mod, norm_mix, norm_ffn, norm_final, w_in, w_sconv, sinks, ssm_conv_w, ssm_conv_b,
           dt_bias, a_log, d_skip, ssm_norm, w_br_conv, w_br_attn, w_br_ssm, w_o,
           ffn_w_gate, ffn_w_up, ffn_w_down, router, moe_w_gate, moe_w_up, moe_w_down):
    nbp, seq, d = x_prompt.shape
    nbat, nt, _ = x_sample.shape
    depth = w_mod.shape[0]
    cwid = w_sconv.shape[2]
    n_heads = sinks.shape[1]
    window, n_kv, head_dim = cache_k.shape[2:]
    heads, hdim, dstate = state_ssm.shape[2:]
    width = heads * hdim
    xbcw = ssm_conv_w.shape[2]
    groups = SSM_GROUPS
    n_experts = router.shape[2]
    aw = n_heads * head_dim
    kvw = n_kv * head_dim
    dims = (width, dstate, hdim, groups, xbcw)
    assert window == ATTN_BLOCK and seq % ATTN_BLOCK == 0 and xbcw == width + 2 * groups * dstate

    o_cv, o_q, o_k, o_z, o_xbc = 0, 3 * cwid, 3 * cwid + aw, 3 * cwid + aw + 2 * kvw, 3 * cwid + aw + 2 * kvw + width
    o_dt = o_xbc + xbcw
    o_g = o_dt + heads
    n_in = w_in.shape[2]
    dt_pad = 2 * LANES - heads

    def regroup(w):
        return jnp.concatenate(
            [w[:, o_g:n_in], w[:, o_cv:o_q], w[:, o_xbc:o_dt], w[:, o_z:o_xbc], w[:, o_q:o_k], w[:, o_k:o_z],
             w[:, o_dt:o_g], jnp.zeros((d, dt_pad), w.dtype)], axis=1).astype(BF16)

    p_gate, p_conv, p_xbc, p_z = 0, 3 * d, 3 * d + 3 * cwid, 3 * d + 3 * cwid + xbcw
    p_q = p_z + width
    p_kv = p_q + aw
    p_dt = p_kv + 2 * kvw
    blk = lambda off, w: off // w
    assert all(off % w == 0 for off, w in ((p_conv, 3 * cwid), (p_xbc, xbcw), (p_z, width), (p_q, aw),
                                           (p_kv, 2 * kvw), (p_dt, LANES)))

    n_c = nbp + nbat
    c_rows = -(-n_c // SUBLANES) * SUBLANES
    c_all = jnp.pad(jnp.concatenate([c_prompt, c_sample], axis=0), ((0, c_rows - n_c), (0, 0)))
    mod_all = _modulation(c_all, w_mod, b_mod)

    xs_tm = x_sample.transpose(1, 0, 2).reshape(1, nt * nbat, d)
    srows = 2 * SUBLANES

    pos_p = jnp.arange(seq, dtype=jnp.int32)
    pos_s = PAST_LEN + jnp.arange(srows, dtype=jnp.int32)
    tab_p = _rope_tables(pos_p, head_dim)
    tab_s = _rope_tables(pos_s, head_dim)

    expand = jnp.repeat(jnp.eye(LANES, heads, dtype=F32), hdim, axis=1).astype(BF16)
    pad_h = lambda v: jnp.pad(v, (0, LANES - heads)).reshape(1, LANES)

    xp, xs = x_prompt, xs_tm
    outs = {k: [] for k in ("kp", "vp", "cp", "scp", "sp", "ks", "vs", "cs", "scs", "ss")}
    for i in range(depth):
        w_in_i = regroup(w_in[i])
        wc, wa, ws, wo = (w[i].astype(BF16) for w in (w_br_conv, w_br_attn, w_br_ssm, w_o))
        mod_p = mod_all[i, :nbp].reshape(nbp, 1, 6 * d)
        mod_s = jnp.tile(mod_all[i, nbp:n_c], (nt, 1)).reshape(1, nt * nbat, 6 * d)
        ssm_params = (ssm_conv_w[i], ssm_conv_b[i].reshape(1, xbcw), pad_h(dt_bias[i]), pad_h(a_log[i]),
                      jnp.repeat(d_skip[i], hdim).reshape(1, width))
        nw_ssm = ssm_norm[i].reshape(1, width)

        proj = _inproj(xp, mod_p, norm_mix[i], w_in_i)
        y_conv, conv_tail = _conv_prompt(proj, w_sconv[i], blk(p_conv, 3 * cwid), cwid)
        y_attn, kv_last = _attn_prompt(proj, sinks[i], tab_p, blk(p_q, aw), blk(p_kv, 2 * kvw),
                                       n_heads, n_kv, head_dim)
        y_ssm, h_fin = _ssd_prompt(proj, (blk(p_xbc, xbcw), blk(p_z, width), blk(p_dt, LANES)),
                                   ssm_params + (nw_ssm, expand), dims)
        xp = _merge(xp, proj, blk(p_gate, 3 * d), y_conv, y_attn, y_ssm, mod_p, wc, wa, ws, wo)
        outs["kp"].append(kv_last[:, :, :kvw].reshape(nbp, window, n_kv, head_dim))
        outs["vp"].append(kv_last[:, :, kvw:].reshape(nbp, window, n_kv, head_dim))
        outs["cp"].append(conv_tail[:, SUBLANES - (w_sconv.shape[1] - 1):])
        outs["scp"].append(proj[:, seq - (ssm_conv_w.shape[1] - 1):, p_xbc:p_xbc + xbcw])
        outs["sp"].append(h_fin.reshape(nbp, heads, hdim, dstate))

        proj_s = _inproj(xs, mod_s, norm_mix[i], w_in_i)[0]
        y_conv_s, u_s = _conv_sample(proj_s[:, p_conv:p_conv + 3 * cwid], state_conv[i].transpose(1, 0, 2),
                                     w_sconv[i], nt, nbat, cwid)
        q_bm = _to_batch_major(proj_s[:, p_q:p_q + aw], nt, nbat, srows)
        kv_bm = _to_batch_major(proj_s[:, p_kv:p_kv + 2 * kvw], nt, nbat, srows)
        y_attn_bm, k_new = _attn_sample(q_bm, kv_bm, cache_k[i].reshape(nbat, window, kvw),
                                        cache_v[i].reshape(nbat, window, kvw), sinks[i], tab_s,
                                        n_heads, n_kv, head_dim)
        y_attn_s = y_attn_bm[:, :nt].transpose(1, 0, 2).reshape(nt * nbat, aw)
        xbc_s = proj_s[:, p_xbc:p_xbc + xbcw]
        ypart, expa, xd, b_tm, c_tm, cd = _ssd_sample_pre(
            xbc_s, proj_s[:, p_dt:p_dt + LANES], state_ssm_conv[i].transpose(1, 0, 2),
            ssm_params + (expand,), nt, nbat, dims)
        yoff_bm, h_new = _ssd_sample_state(
            cd[:, :heads].reshape(nbat * heads), _to_batch_major(c_tm, nt, nbat, srows),
            _to_batch_major(b_tm, nt, nbat, srows), _to_batch_major(xd, nt, nbat, srows), state_ssm[i], dims)
        yoff = yoff_bm[:, :nt].transpose(1, 0, 2).reshape(nt * nbat, width)
        y_ssm_s = _ssd_sample_post(ypart.reshape(nt * nbat, width), yoff, expa.reshape(nt * nbat, width),
                                   proj_s[:, p_z:p_z + width], nw_ssm, groups)
        xs = _merge(xs, proj_s[None], blk(p_gate, 3 * d), y_conv_s[None], y_attn_s[None], y_ssm_s[None],
                    mod_s, wc, wa, ws, wo)
        k_rows = k_new[:, :nt].reshape(nbat, nt, n_kv, head_dim)
        v_rows = kv_bm[:, :nt, kvw:].reshape(nbat, nt, n_kv, head_dim)
        outs["ks"].append(jnp.concatenate([cache_k[i][:, nt:], k_rows], axis=1))
        outs["vs"].append(jnp.concatenate([cache_v[i][:, nt:], v_rows], axis=1))
        outs["cs"].append(u_s[nt - (w_sconv.shape[1] - 1):].transpose(1, 0, 2))
        kc = ssm_conv_w.shape[1] - 1
        outs["scs"].append(xbc_s.reshape(nt, nbat, xbcw)[nt - kc:].transpose(1, 0, 2))
        outs["ss"].append(h_new)

        last = i == depth - 1
        jj = i // 2
        if i % 2 == 0:
            wg, wu, wd = (w[jj].astype(BF16) for w in (ffn_w_gate, ffn_w_up, ffn_w_down))
            xp = _ffn(xp, mod_p, norm_ffn[i], norm_final, wg, wu, wd, last)
            xs = _ffn(xs, mod_s, norm_ffn[i], norm_final, wg, wu, wd, last)
        else:
            wg, wu, wd = (w[jj].astype(BF16) for w in (moe_w_gate, moe_w_up, moe_w_down))
            wr = jnp.pad(router[jj], ((0, 0), (0, LANES - n_experts)))
            xp = _moe(xp, mod_p, norm_ffn[i], norm_final, wr, wg, wu, wd, last)
            xs = _moe(xs, mod_s, norm_ffn[i], norm_final, wr, wg, wu, wd, last)

    y_sample = xs.reshape(nt, nbat, d).transpose(1, 0, 2)
    st = lambda k: jnp.stack(outs[k])
    return (xp, y_sample, st("kp"), st("vp"), st("cp"), st("scp"), st("sp"),
            st("ks"), st("vs"), st("cs"), st("scs"), st("ss"))
```

```python
import functools
import math

import jax
import jax.numpy as jnp
from jax import lax
from jax.experimental import pallas as pl
from jax.experimental.pallas import tpu as pltpu

F32 = jnp.float32
BF16 = jnp.bfloat16

PAST_LEN = 8192
ROPE_THETA = 500000.0
EPS = 1e-6
TOP_K = 2
SSM_GROUPS = 2
ATTN_BLOCK = 128
SSD_CHUNK = 128
MOE_ROW_TILE = 512
ROUTE_TILE = 512

LANES = 128
SUBLANES = 8
VMEM_LIMIT_BYTES = 56 * 1024 * 1024


def _cparams(*semantics):
    return pltpu.CompilerParams(dimension_semantics=semantics, vmem_limit_bytes=VMEM_LIMIT_BYTES)


def _tile(n, pref):
    if n <= pref:
        return n
    t = pref
    while n % t:
        t //= 2
    return t


def _silu(x):
    return x / (1.0 + jnp.exp(-x))


def _sigmoid(x):
    return 1.0 / (1.0 + jnp.exp(-x))


def _softplus(x):
    return jnp.maximum(x, 0.0) + jnp.log1p(jnp.exp(-jnp.abs(x)))


def _dot(a, b):
    return jnp.dot(a, b, preferred_element_type=F32)


def _dot_nt(a, b):
    return lax.dot_general(a, b, (((1,), (1,)), ((), ())), preferred_element_type=F32)


def _dot_tn(a, b):
    return lax.dot_general(a, b, (((0,), (0,)), ((), ())), preferred_element_type=F32)


def _split3(x):
    hi = x.astype(BF16)
    r1 = x - hi.astype(F32)
    mid = r1.astype(BF16)
    lo = (r1 - mid.astype(F32)).astype(BF16)
    return hi, mid, lo


def _dot_exact_rhs01(x, m01):
    hi, mid, lo = _split3(x)
    return _dot(hi, m01) + _dot(mid, m01) + _dot(lo, m01)


def _dot_exact_lhs01(m01, x):
    hi, mid, lo = _split3(x)
    return _dot(m01, hi) + _dot(m01, mid) + _dot(m01, lo)


def _rms_mod(x, norm_w, scale, shift):
    xn = x * lax.rsqrt(jnp.mean(x * x, axis=-1, keepdims=True) + EPS)
    return (xn * norm_w) * (1.0 + scale) + shift


def _mod_kernel(c_ref, w_ref, b_ref, o_ref):
    a = _silu(c_ref[...]).astype(BF16)
    o_ref[0] = _dot(a, w_ref[0].astype(BF16)) + b_ref[0]


def _modulation(c_all, w_mod, b_mod):
    depth, d, n = w_mod.shape
    rows = c_all.shape[0]
    tn = _tile(n, 1024)
    return pl.pallas_call(
        _mod_kernel,
        grid=(depth, n // tn),
        in_specs=[
            pl.BlockSpec((rows, d), lambda i, j: (0, 0)),
            pl.BlockSpec((1, d, tn), lambda i, j: (i, 0, j)),
            pl.BlockSpec((1, 1, tn), lambda i, j: (i, 0, j)),
        ],
        out_specs=pl.BlockSpec((1, rows, tn), lambda i, j: (i, 0, j)),
        out_shape=jax.ShapeDtypeStruct((depth, rows, n), F32),
        compiler_params=_cparams("arbitrary", "arbitrary"),
        name="modulation",
    )(c_all, w_mod, b_mod.reshape(depth, 1, n))


def _mod_spec(mod, tm, chunk, d):
    if mod.shape[1] == 1:
        return pl.BlockSpec((1, 1, d), lambda b, i, *_: (b, 0, chunk))
    return pl.BlockSpec((1, tm, d), lambda b, i, *_: (b, i, chunk))


def _inproj_kernel(x_ref, sh_ref, sc_ref, nw_ref, w_ref, o_ref, h_scr):
    @pl.when(pl.program_id(2) == 0)
    def _():
        h_scr[...] = _rms_mod(x_ref[0], nw_ref[...], sc_ref[0], sh_ref[0]).astype(BF16)

    o_ref[0] = _dot(h_scr[...], w_ref[...])


def _inproj(x, mod, norm_w, w):
    nb, l, d = x.shape
    n = w.shape[1]
    tm = _tile(l, 1024)
    tn = _tile(n, 1024)
    return pl.pallas_call(
        _inproj_kernel,
        grid=(nb, l // tm, n // tn),
        in_specs=[
            pl.BlockSpec((1, tm, d), lambda b, i, j: (b, i, 0)),
            _mod_spec(mod, tm, 0, d),
            _mod_spec(mod, tm, 1, d),
            pl.BlockSpec((1, d), lambda b, i, j: (0, 0)),
            pl.BlockSpec((d, tn), lambda b, i, j: (0, j)),
        ],
        out_specs=pl.BlockSpec((1, tm, tn), lambda b, i, j: (b, i, j)),
        out_shape=jax.ShapeDtypeStruct((nb, l, n), F32),
        scratch_shapes=[pltpu.VMEM((tm, d), BF16)],
        compiler_params=_cparams("arbitrary", "arbitrary", "arbitrary"),
        name="inproj",
    )(x, mod, mod, norm_w.reshape(1, d), w)


def _conv_prompt_kernel(p_ref, w_ref, y_ref, st_ref, ext, *, tm, cw):
    @pl.when(pl.program_id(1) == 0)
    def _():
        ext[0:SUBLANES, :] = jnp.zeros((SUBLANES, cw), F32)

    p = p_ref[0]
    gate_b, c, xx = p[:, :cw], p[:, cw:2 * cw], p[:, 2 * cw:]
    u = c * xx
    ext[SUBLANES:, :] = u
    w = w_ref[...]
    y = w[0:1] * ext[pl.ds(SUBLANES - 2, tm), :] + w[1:2] * ext[pl.ds(SUBLANES - 1, tm), :] + w[2:3] * u
    y_ref[0] = gate_b * y
    tail = u[tm - SUBLANES:, :]
    ext[0:SUBLANES, :] = tail
    st_ref[0] = tail


def _conv_prompt(proj, w, col_block, cw):
    nb, l, _ = proj.shape
    tm = _tile(l, 512)
    kern = functools.partial(_conv_prompt_kernel, tm=tm, cw=cw)
    return pl.pallas_call(
        kern,
        grid=(nb, l // tm),
        in_specs=[
            pl.BlockSpec((1, tm, 3 * cw), lambda b, i: (b, i, col_block)),
            pl.BlockSpec(w.shape, lambda b, i: (0, 0)),
        ],
        out_specs=[
            pl.BlockSpec((1, tm, cw), lambda b, i: (b, i, 0)),
            pl.BlockSpec((1, SUBLANES, cw), lambda b, i: (b, 0, 0)),
        ],
        out_shape=[
            jax.ShapeDtypeStruct((nb, l, cw), F32),
            jax.ShapeDtypeStruct((nb, SUBLANES, cw), F32),
        ],
        scratch_shapes=[pltpu.VMEM((tm + SUBLANES, cw), F32)],
        compiler_params=_cparams("arbitrary", "arbitrary"),
        name="conv_prompt",
    )(proj, w)


def _conv_sample_kernel(p_ref, st_ref, w_ref, y_ref, u_ref, *, nt, nbat, cw, k):
    w = w_ref[...]
    full = [st_ref[j] for j in range(k - 1)]
    gates = []
    for t in range(nt):
        p = p_ref[pl.ds(t * nbat, nbat), :]
        gates.append(p[:, :cw])
        u = p[:, cw:2 * cw] * p[:, 2 * cw:]
        u_ref[t] = u
        full.append(u)
    for t in range(nt):
        acc = w[0:1] * full[t]
        for j in range(1, k):
            acc = acc + w[j:j + 1] * full[t + j]
        y_ref[pl.ds(t * nbat, nbat), :] = gates[t] * acc


def _conv_sample(proj_conv, state_tm, w, nt, nbat, cw):
    k = w.shape[0]
    kern = functools.partial(_conv_sample_kernel, nt=nt, nbat=nbat, cw=cw, k=k)
    return pl.pallas_call(
        kern,
        out_shape=[
            jax.ShapeDtypeStruct((nt * nbat, cw), F32),
            jax.ShapeDtypeStruct((nt, nbat, cw), F32),
        ],
        compiler_params=pltpu.CompilerParams(vmem_limit_bytes=VMEM_LIMIT_BYTES),
        name="conv_sample",
    )(proj_conv, state_tm, w)


def _rope(x, cos, sin_lo, sin_hi, half_rot):
    return (x * cos + pltpu.roll(x, LANES - half_rot, 1) * sin_lo
            + pltpu.roll(x, half_rot, 1) * sin_hi)


def _attn_core(q, kcat, vcat, sinks_ref, valid, cos, sin_lo, sin_hi, *, n_heads, group, head_dim):
    tq = q.shape[0]
    half_rot = head_dim // 8
    heads_per_slab = LANES // head_dim
    scale = head_dim ** -0.5
    lane = lax.broadcasted_iota(jnp.int32, (tq, LANES), 1)
    k_bf = [kcat.astype(BF16), pltpu.roll(kcat, head_dim, 1).astype(BF16)]
    v_bf = [vcat.astype(BF16), pltpu.roll(vcat, head_dim, 1).astype(BF16)]
    slabs = []
    for s in range(n_heads // heads_per_slab):
        qs = _rope(q[:, s * LANES:(s + 1) * LANES], cos, sin_lo, sin_hi, half_rot)
        out = jnp.zeros((tq, LANES), F32)
        for half in range(heads_per_slab):
            h = s * heads_per_slab + half
            g = h // group
            in_head = (lane >= half * head_dim) & (lane < (half + 1) * head_dim)
            qm = jnp.where(in_head, qs, 0.0).astype(BF16)
            swap = 0 if (g % heads_per_slab) == half else 1
            sc = _dot_nt(qm, k_bf[swap]) * scale
            sc = jnp.where(valid, sc, -1e30)
            sink = sinks_ref[h]
            m = jnp.maximum(jnp.max(sc, axis=-1, keepdims=True), sink)
            p = jnp.exp(sc - m)
            p = p / (jnp.sum(p, axis=-1, keepdims=True) + jnp.exp(sink - m))
            o = _dot(p.astype(BF16), v_bf[swap])
            out = jnp.where(in_head, o, out)
        slabs.append(out)
    return slabs


def _attn_prompt_kernel(sinks_ref, q_ref, kv_ref, cos_ref, slo_ref, shi_ref, y_ref, last_ref, kprev, vprev,
                        *, n_heads, group, head_dim):
    j = pl.program_id(1)
    tq = q_ref.shape[1]

    @pl.when(j == 0)
    def _():
        kprev[...] = jnp.zeros_like(kprev)
        vprev[...] = jnp.zeros_like(vprev)

    cos, slo, shi = cos_ref[...], slo_ref[...], shi_ref[...]
    kv = kv_ref[0]
    k_rot = _rope(kv[:, :LANES], cos, slo, shi, head_dim // 8)
    v = kv[:, LANES:]
    kcat = jnp.concatenate([kprev[...], k_rot], axis=0)
    vcat = jnp.concatenate([vprev[...], v], axis=0)
    nk = kcat.shape[0]
    r = lax.broadcasted_iota(jnp.int32, (tq, nk), 0)
    c = lax.broadcasted_iota(jnp.int32, (tq, nk), 1)
    first_key = jnp.where(j > 0, 0, tq)
    valid = (c >= r) & (c <= r + tq) & (c >= first_key)
    slabs = _attn_core(q_ref[0], kcat, vcat, sinks_ref, valid, cos, slo, shi,
                       n_heads=n_heads, group=group, head_dim=head_dim)
    for s, o in enumerate(slabs):
        y_ref[0, :, s * LANES:(s + 1) * LANES] = o
    kprev[...] = k_rot
    vprev[...] = v
    last_ref[0, :, :LANES] = k_rot
    last_ref[0, :, LANES:] = v


def _attn_prompt(proj, sinks, tables, q_block, kv_block, n_heads, n_kv, head_dim):
    nb, l, _ = proj.shape
    tq = ATTN_BLOCK
    qw = n_heads * head_dim
    kvw = 2 * n_kv * head_dim
    assert n_kv * head_dim == LANES
    kern = functools.partial(_attn_prompt_kernel, n_heads=n_heads, group=n_heads // n_kv, head_dim=head_dim)
    tab_spec = pl.BlockSpec((tq, LANES), lambda b, j: (j, 0))
    return pl.pallas_call(
        kern,
        grid=(nb, l // tq),
        in_specs=[
            pl.BlockSpec(memory_space=pltpu.SMEM),
            pl.BlockSpec((1, tq, qw), lambda b, j: (b, j, q_block)),
            pl.BlockSpec((1, tq, kvw), lambda b, j: (b, j, kv_block)),
            tab_spec, tab_spec, tab_spec,
        ],
        out_specs=[
            pl.BlockSpec((1, tq, qw), lambda b, j: (b, j, 0)),
            pl.BlockSpec((1, tq, kvw), lambda b, j: (b, 0, 0)),
        ],
        out_shape=[
            jax.ShapeDtypeStruct((nb, l, qw), F32),
            jax.ShapeDtypeStruct((nb, tq, kvw), F32),
        ],
        scratch_shapes=[pltpu.VMEM((tq, LANES), F32), pltpu.VMEM((tq, LANES), F32)],
        compiler_params=_cparams("arbitrary", "arbitrary"),
        name="attn_prompt",
    )(sinks, proj, proj, *tables)


def _attn_sample_kernel(sinks_ref, q_ref, kv_ref, ck_ref, cv_ref, cos_ref, slo_ref, shi_ref, y_ref, knew_ref,
                        *, n_heads, group, head_dim):
    tq = q_ref.shape[1]
    window = ck_ref.shape[1]
    cos, slo, shi = cos_ref[...], slo_ref[...], shi_ref[...]
    kv = kv_ref[0]
    k_rot = _rope(kv[:, :LANES], cos, slo, shi, head_dim // 8)
    kcat = jnp.concatenate([ck_ref[0], k_rot], axis=0)
    vcat = jnp.concatenate([cv_ref[0], kv[:, LANES:]], axis=0)
    nk = kcat.shape[0]
    r = lax.broadcasted_iota(jnp.int32, (tq, nk), 0)
    c = lax.broadcasted_iota(jnp.int32, (tq, nk), 1)
    valid = (c >= r) & (c <= r + window)
    slabs = _attn_core(q_ref[0], kcat, vcat, sinks_ref, valid, cos, slo, shi,
                       n_heads=n_heads, group=group, head_dim=head_dim)
    for s, o in enumerate(slabs):
        y_ref[0, :, s * LANES:(s + 1) * LANES] = o
    knew_ref[0] = k_rot


def _attn_sample(q_bm, kv_bm, ck, cv, sinks, tables, n_heads, n_kv, head_dim):
    nbat, tq, qw = q_bm.shape
    window = ck.shape[1]
    kern = functools.partial(_attn_sample_kernel, n_heads=n_heads, group=n_heads // n_kv, head_dim=head_dim)
    tab_spec = pl.BlockSpec((tq, LANES), lambda b: (0, 0))
    return pl.pallas_call(
        kern,
        grid=(nbat,),
        in_specs=[
            pl.BlockSpec(memory_space=pltpu.SMEM),
            pl.BlockSpec((1, tq, qw), lambda b: (b, 0, 0)),
            pl.BlockSpec((1, tq, 2 * LANES), lambda b: (b, 0, 0)),
            pl.BlockSpec((1, window, LANES), lambda b: (b, 0, 0)),
            pl.BlockSpec((1, window, LANES), lambda b: (b, 0, 0)),
            tab_spec, tab_spec, tab_spec,
        ],
        out_specs=[
            pl.BlockSpec((1, tq, qw), lambda b: (b, 0, 0)),
            pl.BlockSpec((1, tq, LANES), lambda b: (b, 0, 0)),
        ],
        out_shape=[
            jax.ShapeDtypeStruct((nbat, tq, qw), F32),
            jax.ShapeDtypeStruct((nbat, tq, LANES), F32),
        ],
        compiler_params=_cparams("arbitrary"),
        name="attn_sample",
    )(sinks, q_bm, kv_bm, ck, cv, *tables)


def _rope_tables(pos, head_dim):
    rot = head_dim // 4
    half = rot // 2
    inv = jnp.exp(-(2.0 * jnp.arange(half, dtype=F32) / rot) * math.log(ROPE_THETA))
    ang = pos.astype(F32)[:, None] * inv[None, :]
    cos, sin = jnp.cos(ang), jnp.sin(ang)
    n = pos.shape[0]
    pad = jnp.zeros((n, head_dim - rot), F32)
    zeros = jnp.zeros((n, half), F32)
    cos_h = jnp.concatenate([cos, cos, pad + 1.0], axis=1)
    lo_h = jnp.concatenate([-sin, zeros, pad], axis=1)
    hi_h = jnp.concatenate([zeros, sin, pad], axis=1)
    reps = LANES // head_dim
    return tuple(jnp.tile(t, (1, reps)) for t in (cos_h, lo_h, hi_h))


def _gated_group_norm(y, z, norm_w, groups):
    y = y * _silu(z)
    gw = y.shape[1] // groups
    parts = []
    for g in range(groups):
        yg = y[:, g * gw:(g + 1) * gw]
        parts.append(yg * lax.rsqrt(jnp.mean(yg * yg, axis=-1, keepdims=True) + EPS))
    return jnp.concatenate(parts, axis=1) * norm_w


def _ssd_prompt_kernel(xbc_ref, z_ref, dt_ref, cw_ref, cb_ref, dtb_ref, alog_ref, dsk_ref, nw_ref, e_ref,
                       y_ref, hfin_ref, ext, ht, *, q, width, dstate, hdim, groups):
    j = pl.program_id(1)
    kconv = cw_ref.shape[0]

    @pl.when(j == 0)
    def _():
        ext[0:SUBLANES, :] = jnp.zeros((SUBLANES, ext.shape[1]), F32)
        ht[...] = jnp.zeros_like(ht)

    xbc = xbc_ref[0]
    ext[SUBLANES:, :] = xbc
    cw = cw_ref[...]
    conv = cw[kconv - 1:kconv] * xbc + cb_ref[...]
    for t in range(kconv - 1):
        conv = conv + cw[t:t + 1] * ext[pl.ds(SUBLANES - (kconv - 1) + t, q), :]
    ext[0:SUBLANES, :] = xbc[q - SUBLANES:, :]
    act = _silu(conv)
    xs = act[:, :width]
    bm = act[:, width:width + groups * dstate]
    cm = act[:, width + groups * dstate:]

    expand = e_ref[...]
    dt_h = _softplus(dt_ref[0] + dtb_ref[...])
    a_h = dt_h * (-jnp.exp(alog_ref[...]))
    row = lax.broadcasted_iota(jnp.int32, (q, q), 0)
    col = lax.broadcasted_iota(jnp.int32, (q, q), 1)
    causal = col <= row
    tri = jnp.where(causal, 1.0, 0.0).astype(BF16)
    acum_h = _dot_exact_lhs01(tri, a_h)
    acum_ht = acum_h.T
    dt_x = _dot_exact_rhs01(dt_h, expand)
    acum_x = _dot_exact_rhs01(acum_h, expand)
    xdt = xs * dt_x
    acum_last = acum_x[q - 1:q, :]
    xd = xdt * jnp.exp(acum_last - acum_x)
    chunk_decay = jnp.exp(acum_last)
    exp_acum = jnp.exp(acum_x)

    lane = lax.broadcasted_iota(jnp.int32, (q, LANES), 1)
    gw = width // groups
    heads_per_group = gw // hdim
    pair = LANES // hdim
    y_parts = []
    for g in range(groups):
        b_g = bm[:, g * dstate:(g + 1) * dstate].astype(BF16)
        c_g = cm[:, g * dstate:(g + 1) * dstate].astype(BF16)
        cbm = _dot_nt(c_g, b_g)
        h_g = ht[:, g * gw:(g + 1) * gw]
        y_off = _dot(c_g, h_g.astype(BF16)) * exp_acum[:, g * gw:(g + 1) * gw]
        diag_parts = []
        for jp in range(heads_per_group // pair):
            l0 = g * gw + jp * LANES
            x_pair = xdt[:, l0:l0 + LANES].astype(BF16)
            out = jnp.zeros((q, LANES), F32)
            for half in range(pair):
                hd = g * heads_per_group + jp * pair + half
                decay = jnp.exp(jnp.where(causal, acum_h[:, hd:hd + 1] - acum_ht[hd:hd + 1, :], -jnp.inf))
                res = _dot((cbm * decay).astype(BF16), x_pair)
                in_head = (lane >= half * hdim) & (lane < (half + 1) * hdim)
                out = jnp.where(in_head, res, out)
            diag_parts.append(out)
        y_parts.append(jnp.concatenate(diag_parts, axis=1) + y_off)
        s_t = _dot_tn(b_g, xd[:, g * gw:(g + 1) * gw].astype(BF16))
        ht[:, g * gw:(g + 1) * gw] = h_g * chunk_decay[:, g * gw:(g + 1) * gw] + s_t
    y = jnp.concatenate(y_parts, axis=1) + xs * dsk_ref[...]
    y_ref[0] = _gated_group_norm(y, z_ref[0], nw_ref[...], groups)

    @pl.when(j == pl.num_programs(1) - 1)
    def _():
        hfin_ref[0] = ht[...].T


def _ssd_prompt(proj, blocks, params, dims):
    nb, l, _ = proj.shape
    q = SSD_CHUNK
    width, dstate, hdim, groups, xbcw = dims
    xbc_block, z_block, dt_block = blocks
    cw, cb, dtb, alog, dsk, nw, expand = params
    kern = functools.partial(_ssd_prompt_kernel, q=q, width=width, dstate=dstate, hdim=hdim, groups=groups)
    full = lambda a: pl.BlockSpec(a.shape, lambda b, j: (0,) * a.ndim)
    return pl.pallas_call(
        kern,
        grid=(nb, l // q),
        in_specs=[
            pl.BlockSpec((1, q, xbcw), lambda b, j: (b, j, xbc_block)),
            pl.BlockSpec((1, q, width), lambda b, j: (b, j, z_block)),
            pl.BlockSpec((1, q, LANES), lambda b, j: (b, j, dt_block)),
            full(cw), full(cb), full(dtb), full(alog), full(dsk), full(nw), full(expand),
        ],
        out_specs=[
            pl.BlockSpec((1, q, width), lambda b, j: (b, j, 0)),
            pl.BlockSpec((1, width, dstate), lambda b, j: (b, 0, 0)),
        ],
        out_shape=[
            jax.ShapeDtypeStruct((nb, l, width), F32),
            jax.ShapeDtypeStruct((nb, width, dstate), F32),
        ],
        scratch_shapes=[pltpu.VMEM((q + SUBLANES, xbcw), F32), pltpu.VMEM((dstate, width), F32)],
        compiler_params=_cparams("arbitrary", "arbitrary"),
        name="ssd_prompt",
    )(proj, proj, proj, cw, cb, dtb, alog, dsk, nw, expand)


def _ssd_sample_pre_kernel(xbc_ref, dt_ref, st_ref, cw_ref, cb_ref, dtb_ref, alog_ref, dsk_ref, e_ref,
                           ypart_ref, expa_ref, xd_ref, b_ref, c_ref, cd_ref,
                           *, nt, nbat, width, dstate, groups):
    kconv = cw_ref.shape[0]
    cw = cw_ref[...]
    expand = e_ref[...]
    neg_a = -jnp.exp(alog_ref[...])
    full = [st_ref[t] for t in range(kconv - 1)]
    for t in range(nt):
        full.append(xbc_ref[pl.ds(t * nbat, nbat), :])
    xs, bm, cm, dt_x, acum_x, xdt = [], [], [], [], [], []
    acum_h = None
    for t in range(nt):
        conv = cb_ref[...] + cw[0:1] * full[t]
        for jj in range(1, kconv):
            conv = conv + cw[jj:jj + 1] * full[t + jj]
        act = _silu(conv)
        xs.append(act[:, :width])
        bm.append(act[:, width:width + groups * dstate])
        cm.append(act[:, width + groups * dstate:])
        dt_h = _softplus(dt_ref[pl.ds(t * nbat, nbat), :] + dtb_ref[...])
        a_h = dt_h * neg_a
        acum_h = a_h if acum_h is None else acum_h + a_h
        dt_x.append(_dot_exact_rhs01(dt_h, expand))
        acum_x.append(_dot_exact_rhs01(acum_h, expand))
        xdt.append(xs[t] * dt_x[t])
    cd_ref[...] = jnp.exp(acum_h)
    gw = width // groups
    for t in range(nt):
        y = xs[t] * dsk_ref[...]
        for s in range(t + 1):
            cb_parts = []
            for g in range(groups):
                prod = cm[t][:, g * dstate:(g + 1) * dstate] * bm[s][:, g * dstate:(g + 1) * dstate]
                cb_parts.append(jnp.broadcast_to(jnp.sum(prod, axis=-1, keepdims=True), (nbat, gw)))
            cb_x = jnp.concatenate(cb_parts, axis=1)
            y = y + cb_x * jnp.exp(acum_x[t] - acum_x[s]) * xdt[s]
        ypart_ref[t] = y
        expa_ref[t] = jnp.exp(acum_x[t])
        xd_ref[t] = xdt[t] * jnp.exp(acum_x[nt - 1] - acum_x[t])
        b_ref[t] = bm[t]
        c_ref[t] = cm[t]


def _ssd_sample_pre(xbc_tm, dt_tm, state_tm, params, nt, nbat, dims):
    width, dstate, hdim, groups, xbcw = dims
    cw, cb, dtb, alog, dsk, expand = params
    kern = functools.partial(_ssd_sample_pre_kernel, nt=nt, nbat=nbat, width=width, dstate=dstate, groups=groups)
    sd = jax.ShapeDtypeStruct
    return pl.pallas_call(
        kern,
        out_shape=[
            sd((nt, nbat, width), F32), sd((nt, nbat, width), F32), sd((nt, nbat, width), F32),
            sd((nt, nbat, groups * dstate), F32), sd((nt, nbat, groups * dstate), F32),
            sd((nbat, LANES), F32),
        ],
        compiler_params=pltpu.CompilerParams(vmem_limit_bytes=VMEM_LIMIT_BYTES),
        name="ssd_sample_pre",
    )(xbc_tm, dt_tm, state_tm, cw, cb, dtb, alog, dsk, expand)


def _ssd_sample_state_kernel(cd_ref, c_ref, b_ref, xd_ref, h0_ref, yoff_ref, hnew_ref,
                             *, heads, hdim, dstate, groups):
    b = pl.program_id(0)
    hpg = heads // groups
    gw = hpg * hdim
    for g in range(groups):
        hm = h0_ref[0, g * hpg:(g + 1) * hpg].reshape(gw, dstate)
        c_g = c_ref[0, :, g * dstate:(g + 1) * dstate].astype(BF16)
        b_g = b_ref[0, :, g * dstate:(g + 1) * dstate].astype(BF16)
        yoff_ref[0, :, g * gw:(g + 1) * gw] = _dot_nt(c_g, hm.astype(BF16))
        upd = _dot_tn(xd_ref[0, :, g * gw:(g + 1) * gw].astype(BF16), b_g)
        for hh in range(hpg):
            hd = g * hpg + hh
            hnew_ref[0, hd] = h0_ref[0, hd] * cd_ref[b * heads + hd] + upd[hh * hdim:(hh + 1) * hdim, :]


def _ssd_sample_state(cd_flat, c_bm, b_bm, xd_bm, h0, dims):
    width, dstate, hdim, groups, _ = dims
    nbat, heads = h0.shape[0], h0.shape[1]
    rows = c_bm.shape[1]
    kern = functools.partial(_ssd_sample_state_kernel, heads=heads, hdim=hdim, dstate=dstate, groups=groups)
    return pl.pallas_call(
        kern,
        grid=(nbat,),
        in_specs=[
            pl.BlockSpec(memory_space=pltpu.SMEM),
            pl.BlockSpec((1, rows, groups * dstate), lambda b: (b, 0, 0)),
            pl.BlockSpec((1, rows, groups * dstate), lambda b: (b, 0, 0)),
            pl.BlockSpec((1, rows, width), lambda b: (b, 0, 0)),
            pl.BlockSpec((1, heads, hdim, dstate), lambda b: (b, 0, 0, 0)),
        ],
        out_specs=[
            pl.BlockSpec((1, rows, width), lambda b: (b, 0, 0)),
            pl.BlockSpec((1, heads, hdim, dstate), lambda b: (b, 0, 0, 0)),
        ],
        out_shape=[
            jax.ShapeDtypeStruct((nbat, rows, width), F32),
            jax.ShapeDtypeStruct(h0.shape, F32),
        ],
        compiler_params=_cparams("arbitrary"),
        name="ssd_sample_state",
    )(cd_flat, c_bm, b_bm, xd_bm, h0)


def _ssd_sample_post_kernel(ypart_ref, yoff_ref, expa_ref, z_ref, nw_ref, y_ref, *, groups):
    y = ypart_ref[...] + yoff_ref[...] * expa_ref[...]
    y_ref[...] = _gated_group_norm(y, z_ref[...], nw_ref[...], groups)


def _ssd_sample_post(ypart, yoff, expa, z, nw, groups):
    kern = functools.partial(_ssd_sample_post_kernel, groups=groups)
    return pl.pallas_call(
        kern,
        out_shape=jax.ShapeDtypeStruct(ypart.shape, F32),
        compiler_params=pltpu.CompilerParams(vmem_limit_bytes=VMEM_LIMIT_BYTES),
        name="ssd_sample_post",
    )(ypart, yoff, expa, z, nw)


def _merge_kernel(x_ref, g_ref, yc_ref, ya_ref, ys_ref, g1_ref, wc_ref, wa_ref, ws_ref, wo_ref, o_ref, *, d):
    gates = g_ref[0]
    merged = (_sigmoid(gates[:, :d]) * _dot(yc_ref[0].astype(BF16), wc_ref[...])
              + _sigmoid(gates[:, d:2 * d]) * _dot(ya_ref[0].astype(BF16), wa_ref[...])
              + _sigmoid(gates[:, 2 * d:]) * _dot(ys_ref[0].astype(BF16), ws_ref[...]))
    o_ref[0] = x_ref[0] + g1_ref[0] * _dot(merged.astype(BF16), wo_ref[...])


def _merge(x, proj, gate_block, yc, ya, ys, mod, wc, wa, ws, wo):
    nb, l, d = x.shape
    tm = _tile(l, 256)
    kern = functools.partial(_merge_kernel, d=d)
    tok = lambda w: pl.BlockSpec((1, tm, w), lambda b, i: (b, i, 0))
    full = lambda a: pl.BlockSpec(a.shape, lambda b, i: (0, 0))
    return pl.pallas_call(
        kern,
        grid=(nb, l // tm),
        in_specs=[
            tok(d),
            pl.BlockSpec((1, tm, 3 * d), lambda b, i: (b, i, gate_block)),
            tok(yc.shape[2]), tok(ya.shape[2]), tok(ys.shape[2]),
            _mod_spec(mod, tm, 2, d),
            full(wc), full(wa), full(ws), full(wo),
        ],
        out_specs=tok(d),
        out_shape=jax.ShapeDtypeStruct((nb, l, d), F32),
        compiler_params=_cparams("arbitrary", "arbitrary"),
        name="merge",
    )(x, proj, yc, ya, ys, mod, wc, wa, ws, wo)


def _finish(x, gate, f, nf_ref, final_norm):
    out = x + gate * f
    if final_norm:
        out = out * lax.rsqrt(jnp.mean(out * out, axis=-1, keepdims=True) + EPS) * nf_ref[...]
    return out


def _ffn_kernel(x_ref, sh_ref, sc_ref, g2_ref, nw_ref, nf_ref, wg_ref, wu_ref, wd_ref, o_ref, h_scr, acc,
                *, final_norm):
    f = pl.program_id(2)

    @pl.when(f == 0)
    def _():
        h_scr[...] = _rms_mod(x_ref[0], nw_ref[...], sc_ref[0], sh_ref[0]).astype(BF16)
        acc[...] = jnp.zeros_like(acc)

    h = h_scr[...]
    a = _silu(_dot(h, wg_ref[...])) * _dot(h, wu_ref[...])
    acc[...] += _dot(a.astype(BF16), wd_ref[...])

    @pl.when(f == pl.num_programs(2) - 1)
    def _():
        o_ref[0] = _finish(x_ref[0], g2_ref[0], acc[...], nf_ref, final_norm)


def _ffn(x, mod, norm_w, norm_final, wg, wu, wd, final_norm):
    nb, l, d = x.shape
    ff = wg.shape[1]
    tm = _tile(l, 1024)
    tf = _tile(ff, 512)
    kern = functools.partial(_ffn_kernel, final_norm=final_norm)
    vec = pl.BlockSpec((1, d), lambda b, i, f: (0, 0))
    return pl.pallas_call(
        kern,
        grid=(nb, l // tm, ff // tf),
        in_specs=[
            pl.BlockSpec((1, tm, d), lambda b, i, f: (b, i, 0)),
            _mod_spec(mod, tm, 3, d), _mod_spec(mod, tm, 4, d), _mod_spec(mod, tm, 5, d),
            vec, vec,
            pl.BlockSpec((d, tf), lambda b, i, f: (0, f)),
            pl.BlockSpec((d, tf), lambda b, i, f: (0, f)),
            pl.BlockSpec((tf, d), lambda b, i, f: (f, 0)),
        ],
        out_specs=pl.BlockSpec((1, tm, d), lambda b, i, f: (b, i, 0)),
        out_shape=jax.ShapeDtypeStruct((nb, l, d), F32),
        scratch_shapes=[pltpu.VMEM((tm, d), BF16), pltpu.VMEM((tm, d), F32)],
        compiler_params=_cparams("arbitrary", "arbitrary", "arbitrary"),
        name="ffn",
    )(x, mod, mod, mod, norm_w.reshape(1, d), norm_final.reshape(1, d), wg, wu, wd)


def _route_kernel(x_ref, sh_ref, sc_ref, nw_ref, wr_ref, w_ref, e_ref, *, n_experts):
    h = _rms_mod(x_ref[0], nw_ref[...], sc_ref[0], sh_ref[0])
    h_hi = h.astype(BF16)
    h_lo = (h - h_hi.astype(F32)).astype(BF16)
    wr = wr_ref[...]
    r_hi = wr.astype(BF16)
    r_lo = (wr - r_hi.astype(F32)).astype(BF16)
    logits = _dot(h_hi, r_hi) + _dot(h_lo, r_hi) + _dot(h_hi, r_lo)
    lane = lax.broadcasted_iota(jnp.int32, logits.shape, 1).astype(F32)
    neg = -jnp.inf
    lg = jnp.where(lane < n_experts, logits, neg)
    m1 = jnp.max(lg, axis=-1, keepdims=True)
    i1 = jnp.min(jnp.where(lg == m1, lane, float(LANES)), axis=-1, keepdims=True)
    rest = jnp.where(lane == i1, neg, lg)
    m2 = jnp.max(rest, axis=-1, keepdims=True)
    i2 = jnp.min(jnp.where(rest == m2, lane, float(LANES)), axis=-1, keepdims=True)
    e2 = jnp.exp(m2 - m1)
    w1 = 1.0 / (1.0 + e2)
    w2 = e2 / (1.0 + e2)
    w_ref[...] = jnp.where(lane == 0.0, w1, jnp.where(lane == 1.0, w2, 0.0))
    chosen = jnp.where(lane == 0.0, i1, jnp.where(lane == 1.0, i2, 0.0))
    e_ref[...] = chosen.T[:SUBLANES, :]


def _route(x, mod, norm_w, w_router_pad, n_experts):
    nb, l, d = x.shape
    tm = _tile(l, ROUTE_TILE)
    nt = l // tm
    kern = functools.partial(_route_kernel, n_experts=n_experts)
    return pl.pallas_call(
        kern,
        grid=(nb, nt),
        in_specs=[
            pl.BlockSpec((1, tm, d), lambda b, i: (b, i, 0)),
            _mod_spec(mod, tm, 3, d), _mod_spec(mod, tm, 4, d),
            pl.BlockSpec((1, d), lambda b, i: (0, 0)),
            pl.BlockSpec((d, LANES), lambda b, i: (0, 0)),
        ],
        out_specs=[
            pl.BlockSpec((tm, LANES), lambda b, i: (b * nt + i, 0)),
            pl.BlockSpec((SUBLANES, tm), lambda b, i: (0, b * nt + i)),
        ],
        out_shape=[
            jax.ShapeDtypeStruct((nb * l, LANES), F32),
            jax.ShapeDtypeStruct((SUBLANES, nb * l), F32),
        ],
        compiler_params=_cparams("arbitrary", "arbitrary"),
        name="moe_route",
    )(x, mod, mod, norm_w.reshape(1, d), w_router_pad)


def _plan_kernel(e_ref, pos_ref, te_ref, *, n_experts, tile, row_tile):
    steps = e_ref.shape[1] // tile
    sub = lax.broadcasted_iota(jnp.int32, (SUBLANES, tile), 0).astype(F32)
    sub_col = lax.broadcasted_iota(jnp.int32, (SUBLANES, 1), 0)

    def member(i):
        blk = e_ref[:, pl.ds(pl.multiple_of(i * tile, tile), tile)]
        e1, e2 = blk[0:1, :], blk[1:2, :]
        return e1, e2, jnp.where((sub == e1) | (sub == e2), 1.0, 0.0)

    def count_body(i, cnt):
        return cnt + jnp.sum(member(i)[2], axis=1, keepdims=True)

    cnt = lax.fori_loop(0, steps, count_body, jnp.zeros((SUBLANES, 1), F32))
    padded = jnp.floor((cnt + (row_tile - 1)) * (1.0 / row_tile)) * row_tile
    off = jnp.zeros((SUBLANES, 1), F32)
    run = jnp.zeros((1, 1), F32)
    for e in range(n_experts):
        off = jnp.where(sub_col == e, run, off)
        run = run + padded[e:e + 1, :]
    seg_end = off + padded

    r = lax.broadcasted_iota(jnp.int32, (tile, tile), 0)
    c = lax.broadcasted_iota(jnp.int32, (tile, tile), 1)
    before = jnp.where(r < c, 1.0, 0.0).astype(BF16)

    def pos_body(i, carry):
        e1, e2, m = member(i)
        val = off + carry + _dot(m.astype(BF16), before)
        p1 = jnp.sum(jnp.where(sub == e1, val, 0.0), axis=0, keepdims=True)
        p2 = jnp.sum(jnp.where(sub == e2, val, 0.0), axis=0, keepdims=True)
        rows = jnp.where(sub == 0.0, p1, jnp.where(sub == 1.0, p2, 0.0))
        pos_ref[:, pl.ds(pl.multiple_of(i * tile, tile), tile)] = rows.astype(jnp.int32)
        return carry + jnp.sum(m, axis=1, keepdims=True)

    lax.fori_loop(0, steps, pos_body, jnp.zeros((SUBLANES, 1), F32))

    sub_l = lax.broadcasted_iota(jnp.int32, (SUBLANES, LANES), 0)
    start = lax.broadcasted_iota(jnp.int32, (SUBLANES, LANES), 1).astype(F32) * row_tile
    owner = jnp.sum(jnp.where((seg_end <= start) & (sub_l < n_experts), 1.0, 0.0), axis=0, keepdims=True)
    owner = jnp.minimum(owner, n_experts - 1.0)
    used = run * (1.0 / row_tile)
    te_ref[...] = jnp.where(sub_l == 0, owner, jnp.where(sub_l == 1, used, 0.0)).astype(jnp.int32)


def _plan(e_all, n_experts, row_tile):
    t = e_all.shape[1]
    assert t % LANES == 0 and n_experts <= SUBLANES
    tile = _tile(t, ROUTE_TILE)
    kern = functools.partial(_plan_kernel, n_experts=n_experts, tile=tile, row_tile=row_tile)
    return pl.pallas_call(
        kern,
        out_shape=[
            jax.ShapeDtypeStruct((SUBLANES, t), jnp.int32),
            jax.ShapeDtypeStruct((SUBLANES, LANES), jnp.int32),
        ],
        compiler_params=pltpu.CompilerParams(vmem_limit_bytes=VMEM_LIMIT_BYTES),
        name="moe_plan",
    )(e_all)


def _scatter_kernel(p1_ref, p2_ref, x_ref, sh_ref, sc_ref, nw_ref, xs_in_ref, xs_ref, h_scr, sem, *, tm):
    del xs_in_ref
    h_scr[...] = _rms_mod(x_ref[0], nw_ref[...], sc_ref[0], sh_ref[0])

    def row_copy(t, p):
        return pltpu.make_async_copy(h_scr.at[pl.ds(t, 1), :], xs_ref.at[pl.ds(p, 1), :], sem)

    def issue(t, carry):
        row_copy(t, p1_ref[t]).start()
        row_copy(t, p2_ref[t]).start()
        return carry

    def drain(t, carry):
        row_copy(0, 0).wait()
        row_copy(0, 0).wait()
        return carry

    lax.fori_loop(0, tm, issue, 0, unroll=8)
    lax.fori_loop(0, tm, drain, 0, unroll=8)


def _scatter(x, mod, norm_w, p1, p2, xs_sorted):
    nb, l, d = x.shape
    tm = _tile(l, ROUTE_TILE)
    nt = l // tm
    kern = functools.partial(_scatter_kernel, tm=tm)
    idx = pl.BlockSpec((tm,), lambda b, i: (b * nt + i,), memory_space=pltpu.SMEM)
    return pl.pallas_call(
        kern,
        grid=(nb, nt),
        in_specs=[
            idx, idx,
            pl.BlockSpec((1, tm, d), lambda b, i: (b, i, 0)),
            _mod_spec(mod, tm, 3, d), _mod_spec(mod, tm, 4, d),
            pl.BlockSpec((1, d), lambda b, i: (0, 0)),
            pl.BlockSpec(memory_space=pl.ANY),
        ],
        out_specs=pl.BlockSpec(memory_space=pl.ANY),
        out_shape=jax.ShapeDtypeStruct(xs_sorted.shape, F32),
        scratch_shapes=[pltpu.VMEM((tm, d), F32), pltpu.SemaphoreType.DMA],
        input_output_aliases={6: 0},
        compiler_params=_cparams("arbitrary", "arbitrary"),
        name="moe_scatter",
    )(p1, p2, x, mod, mod, norm_w.reshape(1, d), xs_sorted)


def _group_ffn_kernel(te_ref, xs_ref, wg_ref, wu_ref, wd_ref, y_ref, h_scr, acc):
    j = pl.program_id(0)
    f = pl.program_id(1)

    @pl.when(j < te_ref[LANES])
    def _():
        @pl.when(f == 0)
        def _():
            h_scr[...] = xs_ref[...].astype(BF16)
            acc[...] = jnp.zeros_like(acc)

        h = h_scr[...]
        a = _silu(_dot(h, wg_ref[0])) * _dot(h, wu_ref[0])
        acc[...] += _dot(a.astype(BF16), wd_ref[0])

        @pl.when(f == pl.num_programs(1) - 1)
        def _():
            y_ref[...] = acc[...]

    @pl.when((j >= te_ref[LANES]) & (f == 0))
    def _():
        y_ref[...] = jnp.zeros_like(y_ref)


def _group_ffn(te_flat, xs_sorted, wg, wu, wd, row_tile):
    rows, d = xs_sorted.shape
    ff = wg.shape[2]
    tf = _tile(ff, 512)
    nf = ff // tf

    def tile_of(j, te):
        return jnp.minimum(j, te[LANES] - 1)

    def f_of(j, f, te):
        return jnp.where(j < te[LANES], f, nf - 1)

    grid_spec = pltpu.PrefetchScalarGridSpec(
        num_scalar_prefetch=1,
        grid=(rows // row_tile, nf),
        in_specs=[
            pl.BlockSpec((row_tile, d), lambda j, f, te: (tile_of(j, te), 0)),
            pl.BlockSpec((1, d, tf), lambda j, f, te: (te[tile_of(j, te)], 0, f_of(j, f, te))),
            pl.BlockSpec((1, d, tf), lambda j, f, te: (te[tile_of(j, te)], 0, f_of(j, f, te))),
            pl.BlockSpec((1, tf, d), lambda j, f, te: (te[tile_of(j, te)], f_of(j, f, te), 0)),
        ],
        out_specs=pl.BlockSpec((row_tile, d), lambda j, f, te: (j, 0)),
        scratch_shapes=[pltpu.VMEM((row_tile, d), BF16), pltpu.VMEM((row_tile, d), F32)],
    )
    return pl.pallas_call(
        _group_ffn_kernel,
        grid_spec=grid_spec,
        out_shape=jax.ShapeDtypeStruct((rows, d), F32),
        compiler_params=_cparams("arbitrary", "arbitrary"),
        name="moe_group_ffn",
    )(te_flat, xs_sorted, wg, wu, wd)


def _combine_kernel(p1_ref, p2_ref, x_ref, g2_ref, w_ref, nf_ref, y_hbm, o_ref, buf, sem, *, tm, final_norm):
    def row_copy(k, t, p):
        return pltpu.make_async_copy(y_hbm.at[pl.ds(p, 1), :], buf.at[k, pl.ds(t, 1), :], sem)

    def issue(t, carry):
        row_copy(0, t, p1_ref[t]).start()
        row_copy(1, t, p2_ref[t]).start()
        return carry

    def drain(t, carry):
        row_copy(0, 0, 0).wait()
        row_copy(0, 0, 0).wait()
        return carry

    lax.fori_loop(0, tm, issue, 0, unroll=8)
    lax.fori_loop(0, tm, drain, 0, unroll=8)
    w = w_ref[...]
    f = w[:, 0:1] * buf[0] + w[:, 1:2] * buf[1]
    o_ref[0] = _finish(x_ref[0], g2_ref[0], f, nf_ref, final_norm)


def _combine(x, mod, w_cols, p1, p2, y_sorted, norm_final, final_norm):
    nb, l, d = x.shape
    tm = _tile(l, ROUTE_TILE)
    nt = l // tm
    kern = functools.partial(_combine_kernel, tm=tm, final_norm=final_norm)
    idx = pl.BlockSpec((tm,), lambda b, i: (b * nt + i,), memory_space=pltpu.SMEM)
    return pl.pallas_call(
        kern,
        grid=(nb, nt),
        in_specs=[
            idx, idx,
            pl.BlockSpec((1, tm, d), lambda b, i: (b, i, 0)),
            _mod_spec(mod, tm, 5, d),
            pl.BlockSpec((tm, LANES), lambda b, i: (b * nt + i, 0)),
            pl.BlockSpec((1, d), lambda b, i: (0, 0)),
            pl.BlockSpec(memory_space=pl.ANY),
        ],
        out_specs=pl.BlockSpec((1, tm, d), lambda b, i: (b, i, 0)),
        out_shape=jax.ShapeDtypeStruct((nb, l, d), F32),
        scratch_shapes=[pltpu.VMEM((2, tm, d), F32), pltpu.SemaphoreType.DMA],
        compiler_params=_cparams("arbitrary", "arbitrary"),
        name="moe_combine",
    )(p1, p2, x, mod, w_cols, norm_final.reshape(1, d), y_sorted)


def _moe(groups, norm_w, norm_final, w_router_pad, wg, wu, wd, final_norm):
    n_experts = wg.shape[0]
    d = groups[0][0].shape[2]
    routed = [_route(x, mod, norm_w, w_router_pad, n_experts) for x, mod in groups]
    e_all = jnp.concatenate([e for _, e in routed], axis=1)
    t = e_all.shape[1]
    pos, te = _plan(e_all, n_experts, MOE_ROW_TILE)
    te_flat = te.reshape(SUBLANES * LANES)
    n_tiles = TOP_K * t // MOE_ROW_TILE + n_experts
    assert TOP_K * t % MOE_ROW_TILE == 0 and n_tiles <= LANES
    xs_sorted = jnp.zeros((n_tiles * MOE_ROW_TILE, d), F32)
    spans, start = [], 0
    for x, _ in groups:
        n = x.shape[0] * x.shape[1]
        spans.append((start, start + n))
        start += n
    for (x, mod), (lo, hi) in zip(groups, spans):
        xs_sorted = _scatter(x, mod, norm_w, pos[0, lo:hi], pos[1, lo:hi], xs_sorted)
    y_sorted = _group_ffn(te_flat, xs_sorted, wg, wu, wd, MOE_ROW_TILE)
    return [_combine(x, mod, w_cols, pos[0, lo:hi], pos[1, lo:hi], y_sorted, norm_final, final_norm)
            for (x, mod), (w_cols, _), (lo, hi) in zip(groups, routed, spans)]


def _pad_rows(a, rows):
    return jnp.pad(a, ((0, 0), (0, rows - a.shape[1]), (0, 0)))


def _to_batch_major(a_tm, nt, nbat, rows):
    w = a_tm.shape[-1]
    return _pad_rows(a_tm.reshape(nt, nbat, w).transpose(1, 0, 2), rows)


def kernel(x_prompt, x_sample, c_prompt, c_sample, cache_k, cache_v, state_conv, state_ssm_conv, state_ssm,
           w_mod, b_mod, norm_mix, norm_ffn, norm_final, w_in, w_sconv, sinks, ssm_conv_w, ssm_conv_b,
           dt_bias, a_log, d_skip, ssm_norm, w_br_conv, w_br_attn, w_br_ssm, w_o,
           ffn_w_gate, ffn_w_up, ffn_w_down, router, moe_w_gate, moe_w_up, moe_w_down):
    nbp, seq, d = x_prompt.shape
    nbat, nt, _ = x_sample.shape
    depth = w_mod.shape[0]
    cwid = w_sconv.shape[2]
    n_heads = sinks.shape[1]
    window, n_kv, head_dim = cache_k.shape[2:]
    heads, hdim, dstate = state_ssm.shape[2:]
    width = heads * hdim
    xbcw = ssm_conv_w.shape[2]
    groups = SSM_GROUPS
    n_experts = router.shape[2]
    aw = n_heads * head_dim
    kvw = n_kv * head_dim
    dims = (width, dstate, hdim, groups, xbcw)
    assert window == ATTN_BLOCK and seq % ATTN_BLOCK == 0 and xbcw == width + 2 * groups * dstate

    o_cv, o_q, o_k, o_z, o_xbc = 0, 3 * cwid, 3 * cwid + aw, 3 * cwid + aw + 2 * kvw, 3 * cwid + aw + 2 * kvw + width
    o_dt = o_xbc + xbcw
    o_g = o_dt + heads
    n_in = w_in.shape[2]
    dt_pad = 2 * LANES - heads

    def regroup(w):
        return jnp.concatenate(
            [w[:, o_g:n_in], w[:, o_cv:o_q], w[:, o_xbc:o_dt], w[:, o_z:o_xbc], w[:, o_q:o_k], w[:, o_k:o_z],
             w[:, o_dt:o_g], jnp.zeros((d, dt_pad), w.dtype)], axis=1).astype(BF16)

    p_gate, p_conv, p_xbc, p_z = 0, 3 * d, 3 * d + 3 * cwid, 3 * d + 3 * cwid + xbcw
    p_q = p_z + width
    p_kv = p_q + aw
    p_dt = p_kv + 2 * kvw
    blk = lambda off, w: off // w
    assert all(off % w == 0 for off, w in ((p_conv, 3 * cwid), (p_xbc, xbcw), (p_z, width), (p_q, aw),
                                           (p_kv, 2 * kvw), (p_dt, LANES)))

    n_c = nbp + nbat
    c_rows = -(-n_c // SUBLANES) * SUBLANES
    c_all = jnp.pad(jnp.concatenate([c_prompt, c_sample], axis=0), ((0, c_rows - n_c), (0, 0)))
    mod_all = _modulation(c_all, w_mod, b_mod)

    xs_tm = x_sample.transpose(1, 0, 2).reshape(1, nt * nbat, d)
    srows = 2 * SUBLANES

    pos_p = jnp.arange(seq, dtype=jnp.int32)
    pos_s = PAST_LEN + jnp.arange(srows, dtype=jnp.int32)
    tab_p = _rope_tables(pos_p, head_dim)
    tab_s = _rope_tables(pos_s, head_dim)

    expand = jnp.repeat(jnp.eye(LANES, heads, dtype=F32), hdim, axis=1).astype(BF16)
    pad_h = lambda v: jnp.pad(v, (0, LANES - heads)).reshape(1, LANES)

    xp, xs = x_prompt, xs_tm
    outs = {k: [] for k in ("kp", "vp", "cp", "scp", "sp", "ks", "vs", "cs", "scs", "ss")}
    for i in range(depth):
        w_in_i = regroup(w_in[i])
        wc, wa, ws, wo = (w[i].astype(BF16) for w in (w_br_conv, w_br_attn, w_br_ssm, w_o))
        mod_p = mod_all[i, :nbp].reshape(nbp, 1, 6 * d)
        mod_s = jnp.tile(mod_all[i, nbp:n_c], (nt, 1)).reshape(1, nt * nbat, 6 * d)
        ssm_params = (ssm_conv_w[i], ssm_conv_b[i].reshape(1, xbcw), pad_h(dt_bias[i]), pad_h(a_log[i]),
                      jnp.repeat(d_skip[i], hdim).reshape(1, width))
        nw_ssm = ssm_norm[i].reshape(1, width)

        proj = _inproj(xp, mod_p, norm_mix[i], w_in_i)
        y_conv, conv_tail = _conv_prompt(proj, w_sconv[i], blk(p_conv, 3 * cwid), cwid)
        y_attn, kv_last = _attn_prompt(proj, sinks[i], tab_p, blk(p_q, aw), blk(p_kv, 2 * kvw),
                                       n_heads, n_kv, head_dim)
        y_ssm, h_fin = _ssd_prompt(proj, (blk(p_xbc, xbcw), blk(p_z, width), blk(p_dt, LANES)),
                                   ssm_params + (nw_ssm, expand), dims)
        xp = _merge(xp, proj, blk(p_gate, 3 * d), y_conv, y_attn, y_ssm, mod_p, wc, wa, ws, wo)
        outs["kp"].append(kv_last[:, :, :kvw].reshape(nbp, window, n_kv, head_dim))
        outs["vp"].append(kv_last[:, :, kvw:].reshape(nbp, window, n_kv, head_dim))
        outs["cp"].append(conv_tail[:, SUBLANES - (w_sconv.shape[1] - 1):])
        outs["scp"].append(proj[:, seq - (ssm_conv_w.shape[1] - 1):, p_xbc:p_xbc + xbcw])
        outs["sp"].append(h_fin.reshape(nbp, heads, hdim, dstate))

        proj_s = _inproj(xs, mod_s, norm_mix[i], w_in_i)[0]
        y_conv_s, u_s = _conv_sample(proj_s[:, p_conv:p_conv + 3 * cwid], state_conv[i].transpose(1, 0, 2),
                                     w_sconv[i], nt, nbat, cwid)
        q_bm = _to_batch_major(proj_s[:, p_q:p_q + aw], nt, nbat, srows)
        kv_bm = _to_batch_major(proj_s[:, p_kv:p_kv + 2 * kvw], nt, nbat, srows)
        y_attn_bm, k_new = _attn_sample(q_bm, kv_bm, cache_k[i].reshape(nbat, window, kvw),
                                        cache_v[i].reshape(nbat, window, kvw), sinks[i], tab_s,
                                        n_heads, n_kv, head_dim)
        y_attn_s = y_attn_bm[:, :nt].transpose(1, 0, 2).reshape(nt * nbat, aw)
        xbc_s = proj_s[:, p_xbc:p_xbc + xbcw]
        ypart, expa, xd, b_tm, c_tm, cd = _ssd_sample_pre(
            xbc_s, proj_s[:, p_dt:p_dt + LANES], state_ssm_conv[i].transpose(1, 0, 2),
            ssm_params + (expand,), nt, nbat, dims)
        yoff_bm, h_new = _ssd_sample_state(
            cd[:, :heads].reshape(nbat * heads), _to_batch_major(c_tm, nt, nbat, srows),
            _to_batch_major(b_tm, nt, nbat, srows), _to_batch_major(xd, nt, nbat, srows), state_ssm[i], dims)
        yoff = yoff_bm[:, :nt].transpose(1, 0, 2).reshape(nt * nbat, width)
        y_ssm_s = _ssd_sample_post(ypart.reshape(nt * nbat, width), yoff, expa.reshape(nt * nbat, width),
                                   proj_s[:, p_z:p_z + width], nw_ssm, groups)
        xs = _merge(xs, proj_s[None], blk(p_gate, 3 * d), y_conv_s[None], y_attn_s[None], y_ssm_s[None],
                    mod_s, wc, wa, ws, wo)
        k_rows = k_new[:, :nt].reshape(nbat, nt, n_kv, head_dim)
        v_rows = kv_bm[:, :nt, kvw:].reshape(nbat, nt, n_kv, head_dim)
        outs["ks"].append(jnp.concatenate([cache_k[i][:, nt:], k_rows], axis=1))
        outs["vs"].append(jnp.concatenate([cache_v[i][:, nt:], v_rows], axis=1))
        outs["cs"].append(u_s[nt - (w_sconv.shape[1] - 1):].transpose(1, 0, 2))
        kc = ssm_conv_w.shape[1] - 1
        outs["scs"].append(xbc_s.reshape(nt, nbat, xbcw)[nt - kc:].transpose(1, 0, 2))
        outs["ss"].append(h_new)

        last = i == depth - 1
        jj = i // 2
        if i % 2 == 0:
            wg, wu, wd = (w[jj].astype(BF16) for w in (ffn_w_gate, ffn_w_up, ffn_w_down))
            xp = _ffn(xp, mod_p, norm_ffn[i], norm_final, wg, wu, wd, last)
            xs = _ffn(xs, mod_s, norm_ffn[i], norm_final, wg, wu, wd, last)
        else:
            wg, wu, wd = (w[jj].astype(BF16) for w in (moe_w_gate, moe_w_up, moe_w_down))
            wr = jnp.pad(router[jj], ((0, 0), (0, LANES - n_experts)))
            xp, xs = _moe([(xp, mod_p), (xs, mod_s)], norm_ffn[i], norm_final, wr, wg, wu, wd, last)

    y_sample = xs.reshape(nt, nbat, d).transpose(1, 0, 2)
    st = lambda k: jnp.stack(outs[k])
    return (xp, y_sample, st("kp"), st("vp"), st("cp"), st("scp"), st("sp"),
            st("ks"), st("vs"), st("cs"), st("scs"), st("ss"))
```

```python
import functools
import math

import jax
import jax.numpy as jnp
from jax import lax
from jax.experimental import pallas as pl
from jax.experimental.pallas import tpu as pltpu

F32 = jnp.float32
BF16 = jnp.bfloat16

PAST_LEN = 8192
ROPE_THETA = 500000.0
EPS = 1e-6
TOP_K = 2
SSM_GROUPS = 2
ATTN_BLOCK = 128
SSD_CHUNK = 128
MOE_ROW_TILE = 512
ROUTE_TILE = 512
DMA_UNROLL = 8

LANES = 128
SUBLANES = 8
VMEM_LIMIT_BYTES = 56 * 1024 * 1024


def _cparams(*semantics):
    return pltpu.CompilerParams(dimension_semantics=semantics, vmem_limit_bytes=VMEM_LIMIT_BYTES)


def _tile(n, pref):
    if n <= pref:
        return n
    t = pref
    while n % t:
        t //= 2
    return t


def _silu(x):
    return x / (1.0 + jnp.exp(-x))


def _sigmoid(x):
    return 1.0 / (1.0 + jnp.exp(-x))


def _softplus(x):
    return jnp.maximum(x, 0.0) + jnp.log1p(jnp.exp(-jnp.abs(x)))


def _dot(a, b):
    return jnp.dot(a, b, preferred_element_type=F32)


def _dot_nt(a, b):
    return lax.dot_general(a, b, (((1,), (1,)), ((), ())), preferred_element_type=F32)


def _dot_tn(a, b):
    return lax.dot_general(a, b, (((0,), (0,)), ((), ())), preferred_element_type=F32)


def _split3(x):
    hi = x.astype(BF16)
    r1 = x - hi.astype(F32)
    mid = r1.astype(BF16)
    lo = (r1 - mid.astype(F32)).astype(BF16)
    return hi, mid, lo


def _dot_exact_rhs01(x, m01):
    hi, mid, lo = _split3(x)
    return _dot(hi, m01) + _dot(mid, m01) + _dot(lo, m01)


def _dot_exact_lhs01(m01, x):
    hi, mid, lo = _split3(x)
    return _dot(m01, hi) + _dot(m01, mid) + _dot(m01, lo)


def _rms_mod(x, norm_w, scale, shift):
    xn = x * lax.rsqrt(jnp.mean(x * x, axis=-1, keepdims=True) + EPS)
    return (xn * norm_w) * (1.0 + scale) + shift


def _mod_kernel(c_ref, w_ref, b_ref, o_ref):
    a = _silu(c_ref[...]).astype(BF16)
    o_ref[0] = _dot(a, w_ref[0].astype(BF16)) + b_ref[0]


def _modulation(c_all, w_mod, b_mod):
    depth, d, n = w_mod.shape
    rows = c_all.shape[0]
    tn = _tile(n, 1024)
    return pl.pallas_call(
        _mod_kernel,
        grid=(depth, n // tn),
        in_specs=[
            pl.BlockSpec((rows, d), lambda i, j: (0, 0)),
            pl.BlockSpec((1, d, tn), lambda i, j: (i, 0, j)),
            pl.BlockSpec((1, 1, tn), lambda i, j: (i, 0, j)),
        ],
        out_specs=pl.BlockSpec((1, rows, tn), lambda i, j: (i, 0, j)),
        out_shape=jax.ShapeDtypeStruct((depth, rows, n), F32),
        compiler_params=_cparams("arbitrary", "arbitrary"),
        name="modulation",
    )(c_all, w_mod, b_mod.reshape(depth, 1, n))


def _mod_spec(mod, tm, chunk, d):
    if mod.shape[1] == 1:
        return pl.BlockSpec((1, 1, d), lambda b, i, *_: (b, 0, chunk))
    return pl.BlockSpec((1, tm, d), lambda b, i, *_: (b, i, chunk))


def _inproj_kernel(x_ref, sh_ref, sc_ref, nw_ref, w_ref, o_ref, h_scr):
    @pl.when(pl.program_id(2) == 0)
    def _():
        h_scr[...] = _rms_mod(x_ref[0], nw_ref[...], sc_ref[0], sh_ref[0]).astype(BF16)

    o_ref[0] = _dot(h_scr[...], w_ref[...])


def _inproj(x, mod, norm_w, w):
    nb, l, d = x.shape
    n = w.shape[1]
    tm = _tile(l, 1024)
    tn = _tile(n, 1024)
    return pl.pallas_call(
        _inproj_kernel,
        grid=(nb, l // tm, n // tn),
        in_specs=[
            pl.BlockSpec((1, tm, d), lambda b, i, j: (b, i, 0)),
            _mod_spec(mod, tm, 0, d),
            _mod_spec(mod, tm, 1, d),
            pl.BlockSpec((1, d), lambda b, i, j: (0, 0)),
            pl.BlockSpec((d, tn), lambda b, i, j: (0, j)),
        ],
        out_specs=pl.BlockSpec((1, tm, tn), lambda b, i, j: (b, i, j)),
        out_shape=jax.ShapeDtypeStruct((nb, l, n), F32),
        scratch_shapes=[pltpu.VMEM((tm, d), BF16)],
        compiler_params=_cparams("arbitrary", "arbitrary", "arbitrary"),
        name="inproj",
    )(x, mod, mod, norm_w.reshape(1, d), w)


def _conv_prompt_kernel(p_ref, w_ref, y_ref, st_ref, ext, *, tm, cw):
    @pl.when(pl.program_id(1) == 0)
    def _():
        ext[0:SUBLANES, :] = jnp.zeros((SUBLANES, cw), F32)

    p = p_ref[0]
    gate_b, c, xx = p[:, :cw], p[:, cw:2 * cw], p[:, 2 * cw:]
    u = c * xx
    ext[SUBLANES:, :] = u
    w = w_ref[...]
    y = w[0:1] * ext[pl.ds(SUBLANES - 2, tm), :] + w[1:2] * ext[pl.ds(SUBLANES - 1, tm), :] + w[2:3] * u
    y_ref[0] = gate_b * y
    tail = u[tm - SUBLANES:, :]
    ext[0:SUBLANES, :] = tail
    st_ref[0] = tail


def _conv_prompt(proj, w, col_block, cw):
    nb, l, _ = proj.shape
    tm = _tile(l, 512)
    kern = functools.partial(_conv_prompt_kernel, tm=tm, cw=cw)
    return pl.pallas_call(
        kern,
        grid=(nb, l // tm),
        in_specs=[
            pl.BlockSpec((1, tm, 3 * cw), lambda b, i: (b, i, col_block)),
            pl.BlockSpec(w.shape, lambda b, i: (0, 0)),
        ],
        out_specs=[
            pl.BlockSpec((1, tm, cw), lambda b, i: (b, i, 0)),
            pl.BlockSpec((1, SUBLANES, cw), lambda b, i: (b, 0, 0)),
        ],
        out_shape=[
            jax.ShapeDtypeStruct((nb, l, cw), F32),
            jax.ShapeDtypeStruct((nb, SUBLANES, cw), F32),
        ],
        scratch_shapes=[pltpu.VMEM((tm + SUBLANES, cw), F32)],
        compiler_params=_cparams("arbitrary", "arbitrary"),
        name="conv_prompt",
    )(proj, w)


def _conv_sample_kernel(p_ref, st_ref, w_ref, y_ref, u_ref, *, nt, nbat, cw, k):
    w = w_ref[...]
    full = [st_ref[j] for j in range(k - 1)]
    gates = []
    for t in range(nt):
        p = p_ref[pl.ds(t * nbat, nbat), :]
        gates.append(p[:, :cw])
        u = p[:, cw:2 * cw] * p[:, 2 * cw:]
        u_ref[t] = u
        full.append(u)
    for t in range(nt):
        acc = w[0:1] * full[t]
        for j in range(1, k):
            acc = acc + w[j:j + 1] * full[t + j]
        y_ref[pl.ds(t * nbat, nbat), :] = gates[t] * acc


def _whole(shape):
    return pl.BlockSpec(shape, lambda i: (0,) * len(shape))


def _conv_sample(proj_s, col_block, state_tm, w, nt, nbat, cw):
    k = w.shape[0]
    rows = nt * nbat
    kern = functools.partial(_conv_sample_kernel, nt=nt, nbat=nbat, cw=cw, k=k)
    return pl.pallas_call(
        kern,
        grid=(1,),
        in_specs=[pl.BlockSpec((rows, 3 * cw), lambda i: (0, col_block)), _whole(state_tm.shape), _whole(w.shape)],
        out_specs=[_whole((rows, cw)), _whole((nt, nbat, cw))],
        out_shape=[
            jax.ShapeDtypeStruct((rows, cw), F32),
            jax.ShapeDtypeStruct((nt, nbat, cw), F32),
        ],
        compiler_params=_cparams("arbitrary"),
        name="conv_sample",
    )(proj_s, state_tm, w)


def _rope(x, cos, sin_lo, sin_hi, half_rot):
    return (x * cos + pltpu.roll(x, LANES - half_rot, 1) * sin_lo
            + pltpu.roll(x, half_rot, 1) * sin_hi)


def _attn_core(q, kcat, vcat, sinks_ref, valid, cos, sin_lo, sin_hi, *, n_heads, group, head_dim):
    tq = q.shape[0]
    half_rot = head_dim // 8
    heads_per_slab = LANES // head_dim
    scale = head_dim ** -0.5
    lane = lax.broadcasted_iota(jnp.int32, (tq, LANES), 1)
    k_bf = [kcat.astype(BF16), pltpu.roll(kcat, head_dim, 1).astype(BF16)]
    v_bf = [vcat.astype(BF16), pltpu.roll(vcat, head_dim, 1).astype(BF16)]
    slabs = []
    for s in range(n_heads // heads_per_slab):
        qs = _rope(q[:, s * LANES:(s + 1) * LANES], cos, sin_lo, sin_hi, half_rot)
        out = jnp.zeros((tq, LANES), F32)
        for half in range(heads_per_slab):
            h = s * heads_per_slab + half
            g = h // group
            in_head = (lane >= half * head_dim) & (lane < (half + 1) * head_dim)
            qm = jnp.where(in_head, qs, 0.0).astype(BF16)
            swap = 0 if (g % heads_per_slab) == half else 1
            sc = _dot_nt(qm, k_bf[swap]) * scale
            sc = jnp.where(valid, sc, -1e30)
            sink = sinks_ref[h]
            m = jnp.maximum(jnp.max(sc, axis=-1, keepdims=True), sink)
            p = jnp.exp(sc - m)
            p = p / (jnp.sum(p, axis=-1, keepdims=True) + jnp.exp(sink - m))
            o = _dot(p.astype(BF16), v_bf[swap])
            out = jnp.where(in_head, o, out)
        slabs.append(out)
    return slabs


def _attn_prompt_kernel(sinks_ref, q_ref, kv_ref, cos_ref, slo_ref, shi_ref, y_ref, last_ref, kprev, vprev,
                        *, n_heads, group, head_dim):
    j = pl.program_id(1)
    tq = q_ref.shape[1]

    @pl.when(j == 0)
    def _():
        kprev[...] = jnp.zeros_like(kprev)
        vprev[...] = jnp.zeros_like(vprev)

    cos, slo, shi = cos_ref[...], slo_ref[...], shi_ref[...]
    kv = kv_ref[0]
    k_rot = _rope(kv[:, :LANES], cos, slo, shi, head_dim // 8)
    v = kv[:, LANES:]
    kcat = jnp.concatenate([kprev[...], k_rot], axis=0)
    vcat = jnp.concatenate([vprev[...], v], axis=0)
    nk = kcat.shape[0]
    r = lax.broadcasted_iota(jnp.int32, (tq, nk), 0)
    c = lax.broadcasted_iota(jnp.int32, (tq, nk), 1)
    first_key = jnp.where(j > 0, 0, tq)
    valid = (c >= r) & (c <= r + tq) & (c >= first_key)
    slabs = _attn_core(q_ref[0], kcat, vcat, sinks_ref, valid, cos, slo, shi,
                       n_heads=n_heads, group=group, head_dim=head_dim)
    for s, o in enumerate(slabs):
        y_ref[0, :, s * LANES:(s + 1) * LANES] = o
    kprev[...] = k_rot
    vprev[...] = v
    last_ref[0, :, :LANES] = k_rot
    last_ref[0, :, LANES:] = v


def _attn_prompt(proj, sinks, tables, q_block, kv_block, n_heads, n_kv, head_dim):
    nb, l, _ = proj.shape
    tq = ATTN_BLOCK
    qw = n_heads * head_dim
    kvw = 2 * n_kv * head_dim
    assert n_kv * head_dim == LANES
    kern = functools.partial(_attn_prompt_kernel, n_heads=n_heads, group=n_heads // n_kv, head_dim=head_dim)
    tab_spec = pl.BlockSpec((tq, LANES), lambda b, j: (j, 0))
    return pl.pallas_call(
        kern,
        grid=(nb, l // tq),
        in_specs=[
            pl.BlockSpec(memory_space=pltpu.SMEM),
            pl.BlockSpec((1, tq, qw), lambda b, j: (b, j, q_block)),
            pl.BlockSpec((1, tq, kvw), lambda b, j: (b, j, kv_block)),
            tab_spec, tab_spec, tab_spec,
        ],
        out_specs=[
            pl.BlockSpec((1, tq, qw), lambda b, j: (b, j, 0)),
            pl.BlockSpec((1, tq, kvw), lambda b, j: (b, 0, 0)),
        ],
        out_shape=[
            jax.ShapeDtypeStruct((nb, l, qw), F32),
            jax.ShapeDtypeStruct((nb, tq, kvw), F32),
        ],
        scratch_shapes=[pltpu.VMEM((tq, LANES), F32), pltpu.VMEM((tq, LANES), F32)],
        compiler_params=_cparams("arbitrary", "arbitrary"),
        name="attn_prompt",
    )(sinks, proj, proj, *tables)


def _attn_sample_kernel(sinks_ref, q_ref, kv_ref, ck_ref, cv_ref, cos_ref, slo_ref, shi_ref, y_ref, knew_ref,
                        qh, kc, vc, ob, *, n_heads, group, head_dim, nt):
    gb, window = ck_ref.shape[0], ck_ref.shape[1]
    half_rot = head_dim // 8
    heads_per_slab = LANES // head_dim
    nq = n_heads * SUBLANES
    nk = kc.shape[1]
    lane = lax.broadcasted_iota(jnp.int32, (gb, LANES), 1)

    @pl.when(pl.program_id(0) == 0)
    def _():
        qh[...] = jnp.zeros_like(qh)
        kc[...] = jnp.zeros_like(kc)
        vc[...] = jnp.zeros_like(vc)

    kc[:, 0:window, :] = ck_ref[...]
    vc[:, 0:window, :] = cv_ref[...]
    for t in range(nt):
        cos, slo, shi = cos_ref[t:t + 1, :], slo_ref[t:t + 1, :], shi_ref[t:t + 1, :]
        kv = kv_ref[t]
        k_rot = _rope(kv[:, :LANES], cos, slo, shi, half_rot)
        knew_ref[t] = k_rot
        kc[:, window + t, :] = k_rot
        vc[:, window + t, :] = kv[:, LANES:]
        for s in range(n_heads // heads_per_slab):
            qs = _rope(q_ref[t][:, s * LANES:(s + 1) * LANES], cos, slo, shi, half_rot)
            qs_swapped = pltpu.roll(qs, head_dim, 1)
            for half in range(heads_per_slab):
                h = s * heads_per_slab + half
                g = (h // group) % heads_per_slab
                in_kv_half = (lane >= g * head_dim) & (lane < (g + 1) * head_dim)
                qh[:, h * SUBLANES + t, :] = jnp.where(in_kv_half, qs if g == half else qs_swapped, 0.0)

    sc = jnp.einsum("bqd,bkd->bqk", qh[...].astype(BF16), kc[...].astype(BF16),
                    preferred_element_type=F32) * (head_dim ** -0.5)
    r = lax.broadcasted_iota(jnp.int32, (nq, nk), 0) % SUBLANES
    c = lax.broadcasted_iota(jnp.int32, (nq, nk), 1)
    valid = (c >= r) & (c <= r + window)
    sc = jnp.where(valid[None], sc, -1e30)
    row_head = lax.broadcasted_iota(jnp.int32, (nq, 1), 0) // SUBLANES
    sink = jnp.zeros((nq, 1), F32)
    for h in range(n_heads):
        sink = jnp.where(row_head == h, sinks_ref[h], sink)
    m = jnp.maximum(jnp.max(sc, axis=-1, keepdims=True), sink[None])
    p = jnp.exp(sc - m)
    p = p / (jnp.sum(p, axis=-1, keepdims=True) + jnp.exp(sink[None] - m))
    ob[...] = jnp.einsum("bqk,bkd->bqd", p.astype(BF16), vc[...].astype(BF16), preferred_element_type=F32)

    for t in range(nt):
        for s in range(n_heads // heads_per_slab):
            out = jnp.zeros((gb, LANES), F32)
            for half in range(heads_per_slab):
                h = s * heads_per_slab + half
                g = (h // group) % heads_per_slab
                o = ob[:, h * SUBLANES + t, :]
                if g != half:
                    o = pltpu.roll(o, head_dim, 1)
                out = jnp.where((lane >= half * head_dim) & (lane < (half + 1) * head_dim), o, out)
            y_ref[t, :, s * LANES:(s + 1) * LANES] = out


def _attn_sample(proj_tm, ck, cv, sinks, tables, q_block, kv_block, n_heads, n_kv, head_dim):
    nt, nbat, _ = proj_tm.shape
    window = ck.shape[1]
    qw = n_heads * head_dim
    assert n_kv * head_dim == LANES and nt <= SUBLANES
    gb = _tile(nbat, 16)
    nk = window + 2 * SUBLANES
    kern = functools.partial(_attn_sample_kernel, n_heads=n_heads, group=n_heads // n_kv, head_dim=head_dim, nt=nt)
    tab_spec = pl.BlockSpec((SUBLANES, LANES), lambda b: (0, 0))
    return pl.pallas_call(
        kern,
        grid=(nbat // gb,),
        in_specs=[
            pl.BlockSpec(memory_space=pltpu.SMEM),
            pl.BlockSpec((nt, gb, qw), lambda b: (0, b, q_block)),
            pl.BlockSpec((nt, gb, 2 * LANES), lambda b: (0, b, kv_block)),
            pl.BlockSpec((gb, window, LANES), lambda b: (b, 0, 0)),
            pl.BlockSpec((gb, window, LANES), lambda b: (b, 0, 0)),
            tab_spec, tab_spec, tab_spec,
        ],
        out_specs=[
            pl.BlockSpec((nt, gb, qw), lambda b: (0, b, 0)),
            pl.BlockSpec((nt, gb, LANES), lambda b: (0, b, 0)),
        ],
        out_shape=[
            jax.ShapeDtypeStruct((nt, nbat, qw), F32),
            jax.ShapeDtypeStruct((nt, nbat, LANES), F32),
        ],
        scratch_shapes=[
            pltpu.VMEM((gb, n_heads * SUBLANES, LANES), F32),
            pltpu.VMEM((gb, nk, LANES), F32), pltpu.VMEM((gb, nk, LANES), F32),
            pltpu.VMEM((gb, n_heads * SUBLANES, LANES), F32),
        ],
        compiler_params=_cparams("arbitrary"),
        name="attn_sample",
    )(sinks, proj_tm, proj_tm, ck, cv, *tables)


def _rope_tables(pos, head_dim):
    rot = head_dim // 4
    half = rot // 2
    inv = jnp.exp(-(2.0 * jnp.arange(half, dtype=F32) / rot) * math.log(ROPE_THETA))
    ang = pos.astype(F32)[:, None] * inv[None, :]
    cos, sin = jnp.cos(ang), jnp.sin(ang)
    n = pos.shape[0]
    pad = jnp.zeros((n, head_dim - rot), F32)
    zeros = jnp.zeros((n, half), F32)
    cos_h = jnp.concatenate([cos, cos, pad + 1.0], axis=1)
    lo_h = jnp.concatenate([-sin, zeros, pad], axis=1)
    hi_h = jnp.concatenate([zeros, sin, pad], axis=1)
    reps = LANES // head_dim
    return tuple(jnp.tile(t, (1, reps)) for t in (cos_h, lo_h, hi_h))


def _gated_group_norm(y, z, norm_w, groups):
    y = y * _silu(z)
    gw = y.shape[1] // groups
    parts = []
    for g in range(groups):
        yg = y[:, g * gw:(g + 1) * gw]
        parts.append(yg * lax.rsqrt(jnp.mean(yg * yg, axis=-1, keepdims=True) + EPS))
    return jnp.concatenate(parts, axis=1) * norm_w


def _ssd_prompt_kernel(xbc_ref, z_ref, dt_ref, cw_ref, cb_ref, dtb_ref, alog_ref, dsk_ref, nw_ref, e_ref,
                       y_ref, hfin_ref, ext, ht, *, q, width, dstate, hdim, groups):
    j = pl.program_id(1)
    kconv = cw_ref.shape[0]

    @pl.when(j == 0)
    def _():
        ext[0:SUBLANES, :] = jnp.zeros((SUBLANES, ext.shape[1]), F32)
        ht[...] = jnp.zeros_like(ht)

    xbc = xbc_ref[0]
    ext[SUBLANES:, :] = xbc
    cw = cw_ref[...]
    conv = cw[kconv - 1:kconv] * xbc + cb_ref[...]
    for t in range(kconv - 1):
        conv = conv + cw[t:t + 1] * ext[pl.ds(SUBLANES - (kconv - 1) + t, q), :]
    ext[0:SUBLANES, :] = xbc[q - SUBLANES:, :]
    act = _silu(conv)
    xs = act[:, :width]
    bm = act[:, width:width + groups * dstate]
    cm = act[:, width + groups * dstate:]

    expand = e_ref[...]
    nh = expand.shape[0]
    dt_h = _softplus(dt_ref[0] + dtb_ref[...])
    a_h = dt_h * (-jnp.exp(alog_ref[...]))
    row = lax.broadcasted_iota(jnp.int32, (q, q), 0)
    col = lax.broadcasted_iota(jnp.int32, (q, q), 1)
    causal = col <= row
    tri = jnp.where(causal, 1.0, 0.0).astype(BF16)
    acum_h = _dot_exact_lhs01(tri, a_h)
    acum_ht = acum_h.T
    dt_x = _dot_exact_rhs01(dt_h[:, :nh], expand)
    acum_x = _dot_exact_rhs01(acum_h[:, :nh], expand)
    xdt = xs * dt_x
    acum_last = acum_x[q - 1:q, :]
    xd = xdt * jnp.exp(acum_last - acum_x)
    chunk_decay = jnp.exp(acum_last)
    exp_acum = jnp.exp(acum_x)

    lane = lax.broadcasted_iota(jnp.int32, (q, LANES), 1)
    gw = width // groups
    heads_per_group = gw // hdim
    pair = LANES // hdim
    y_parts = []
    for g in range(groups):
        b_g = bm[:, g * dstate:(g + 1) * dstate].astype(BF16)
        c_g = cm[:, g * dstate:(g + 1) * dstate].astype(BF16)
        cbm = _dot_nt(c_g, b_g)
        h_g = ht[:, g * gw:(g + 1) * gw]
        y_off = _dot(c_g, h_g.astype(BF16)) * exp_acum[:, g * gw:(g + 1) * gw]
        diag_parts = []
        for jp in range(heads_per_group // pair):
            l0 = g * gw + jp * LANES
            x_pair = xdt[:, l0:l0 + LANES].astype(BF16)
            out = jnp.zeros((q, LANES), F32)
            for half in range(pair):
                hd = g * heads_per_group + jp * pair + half
                decay = jnp.exp(jnp.where(causal, acum_h[:, hd:hd + 1] - acum_ht[hd:hd + 1, :], -jnp.inf))
                res = _dot((cbm * decay).astype(BF16), x_pair)
                in_head = (lane >= half * hdim) & (lane < (half + 1) * hdim)
                out = jnp.where(in_head, res, out)
            diag_parts.append(out)
        y_parts.append(jnp.concatenate(diag_parts, axis=1) + y_off)
        s_t = _dot_tn(b_g, xd[:, g * gw:(g + 1) * gw].astype(BF16))
        ht[:, g * gw:(g + 1) * gw] = h_g * chunk_decay[:, g * gw:(g + 1) * gw] + s_t
    y = jnp.concatenate(y_parts, axis=1) + xs * dsk_ref[...]
    y_ref[0] = _gated_group_norm(y, z_ref[0], nw_ref[...], groups)

    @pl.when(j == pl.num_programs(1) - 1)
    def _():
        hfin_ref[0] = ht[...].T


def _ssd_prompt(proj, blocks, params, dims):
    nb, l, _ = proj.shape
    q = SSD_CHUNK
    width, dstate, hdim, groups, xbcw = dims
    xbc_block, z_block, dt_block = blocks
    cw, cb, dtb, alog, dsk, nw, expand = params
    kern = functools.partial(_ssd_prompt_kernel, q=q, width=width, dstate=dstate, hdim=hdim, groups=groups)
    full = lambda a: pl.BlockSpec(a.shape, lambda b, j: (0,) * a.ndim)
    return pl.pallas_call(
        kern,
        grid=(nb, l // q),
        in_specs=[
            pl.BlockSpec((1, q, xbcw), lambda b, j: (b, j, xbc_block)),
            pl.BlockSpec((1, q, width), lambda b, j: (b, j, z_block)),
            pl.BlockSpec((1, q, LANES), lambda b, j: (b, j, dt_block)),
            full(cw), full(cb), full(dtb), full(alog), full(dsk), full(nw), full(expand),
        ],
        out_specs=[
            pl.BlockSpec((1, q, width), lambda b, j: (b, j, 0)),
            pl.BlockSpec((1, width, dstate), lambda b, j: (b, 0, 0)),
        ],
        out_shape=[
            jax.ShapeDtypeStruct((nb, l, width), F32),
            jax.ShapeDtypeStruct((nb, width, dstate), F32),
        ],
        scratch_shapes=[pltpu.VMEM((q + SUBLANES, xbcw), F32), pltpu.VMEM((dstate, width), F32)],
        compiler_params=_cparams("arbitrary", "arbitrary"),
        name="ssd_prompt",
    )(proj, proj, proj, cw, cb, dtb, alog, dsk, nw, expand)


def _ssd_sample_pre_kernel(xbc_ref, dt_ref, st_ref, cw_ref, cb_ref, dtb_ref, alog_ref, dsk_ref, e_ref,
                           ypart_ref, expa_ref, xd_ref, b_ref, c_ref, cd_ref,
                           *, nt, nbat, width, dstate, groups):
    kconv = cw_ref.shape[0]
    cw = cw_ref[...]
    expand = e_ref[...]
    nh = expand.shape[0]
    neg_a = -jnp.exp(alog_ref[...])
    full = [st_ref[t] for t in range(kconv - 1)]
    for t in range(nt):
        full.append(xbc_ref[pl.ds(t * nbat, nbat), :])
    xs, bm, cm, dt_x, acum_x, xdt = [], [], [], [], [], []
    acum_h = None
    for t in range(nt):
        conv = cb_ref[...] + cw[0:1] * full[t]
        for jj in range(1, kconv):
            conv = conv + cw[jj:jj + 1] * full[t + jj]
        act = _silu(conv)
        xs.append(act[:, :width])
        bm.append(act[:, width:width + groups * dstate])
        cm.append(act[:, width + groups * dstate:])
        dt_h = _softplus(dt_ref[pl.ds(t * nbat, nbat), :] + dtb_ref[...])
        a_h = dt_h * neg_a
        acum_h = a_h if acum_h is None else acum_h + a_h
        dt_x.append(_dot_exact_rhs01(dt_h[:, :nh], expand))
        acum_x.append(_dot_exact_rhs01(acum_h[:, :nh], expand))
        xdt.append(xs[t] * dt_x[t])
    cd_ref[...] = jnp.exp(acum_h)
    gw = width // groups
    for t in range(nt):
        y = xs[t] * dsk_ref[...]
        for s in range(t + 1):
            cb_parts = []
            for g in range(groups):
                prod = cm[t][:, g * dstate:(g + 1) * dstate] * bm[s][:, g * dstate:(g + 1) * dstate]
                cb_parts.append(jnp.broadcast_to(jnp.sum(prod, axis=-1, keepdims=True), (nbat, gw)))
            cb_x = jnp.concatenate(cb_parts, axis=1)
            y = y + cb_x * jnp.exp(acum_x[t] - acum_x[s]) * xdt[s]
        ypart_ref[t] = y
        expa_ref[t] = jnp.exp(acum_x[t])
    xd_ref[...] = jnp.zeros_like(xd_ref)
    b_ref[...] = jnp.zeros_like(b_ref)
    c_ref[...] = jnp.zeros_like(c_ref)
    for t in range(nt):
        xd_ref[:, t, :] = xdt[t] * jnp.exp(acum_x[nt - 1] - acum_x[t])
        b_ref[:, t, :] = bm[t]
        c_ref[:, t, :] = cm[t]


def _ssd_sample_pre(proj_s, blocks, state_tm, params, nt, nbat, dims, srows):
    width, dstate, hdim, groups, xbcw = dims
    xbc_block, dt_block = blocks
    cw, cb, dtb, alog, dsk, expand = params
    rows = nt * nbat
    kern = functools.partial(_ssd_sample_pre_kernel, nt=nt, nbat=nbat, width=width, dstate=dstate, groups=groups)
    sd = jax.ShapeDtypeStruct
    out_shapes = [(nt, nbat, width), (nt, nbat, width), (nbat, srows, width),
                  (nbat, srows, groups * dstate), (nbat, srows, groups * dstate), (nbat, LANES)]
    return pl.pallas_call(
        kern,
        grid=(1,),
        in_specs=[pl.BlockSpec((rows, xbcw), lambda i: (0, xbc_block)),
                  pl.BlockSpec((rows, LANES), lambda i: (0, dt_block)),
                  _whole(state_tm.shape)] + [_whole(a.shape) for a in params],
        out_specs=[_whole(s) for s in out_shapes],
        out_shape=[sd(s, F32) for s in out_shapes],
        compiler_params=_cparams("arbitrary"),
        name="ssd_sample_pre",
    )(proj_s, proj_s, state_tm, cw, cb, dtb, alog, dsk, expand)


def _ssd_sample_state_kernel(cd_ref, c_ref, b_ref, xd_ref, h0_ref, yoff_ref, hnew_ref,
                             *, heads, hdim, dstate, groups):
    b = pl.program_id(0)
    hpg = heads // groups
    gw = hpg * hdim
    for g in range(groups):
        hm = h0_ref[0, g * hpg:(g + 1) * hpg].reshape(gw, dstate)
        c_g = c_ref[0, :, g * dstate:(g + 1) * dstate].astype(BF16)
        b_g = b_ref[0, :, g * dstate:(g + 1) * dstate].astype(BF16)
        yoff_ref[0, :, g * gw:(g + 1) * gw] = _dot_nt(c_g, hm.astype(BF16))
        upd = _dot_tn(xd_ref[0, :, g * gw:(g + 1) * gw].astype(BF16), b_g)
        for hh in range(hpg):
            hd = g * hpg + hh
            hnew_ref[0, hd] = h0_ref[0, hd] * cd_ref[b * heads + hd] + upd[hh * hdim:(hh + 1) * hdim, :]


def _ssd_sample_state(cd_flat, c_bm, b_bm, xd_bm, h0, dims):
    width, dstate, hdim, groups, _ = dims
    nbat, heads = h0.shape[0], h0.shape[1]
    rows = c_bm.shape[1]
    kern = functools.partial(_ssd_sample_state_kernel, heads=heads, hdim=hdim, dstate=dstate, groups=groups)
    return pl.pallas_call(
        kern,
        grid=(nbat,),
        in_specs=[
            pl.BlockSpec(memory_space=pltpu.SMEM),
            pl.BlockSpec((1, rows, groups * dstate), lambda b: (b, 0, 0)),
            pl.BlockSpec((1, rows, groups * dstate), lambda b: (b, 0, 0)),
            pl.BlockSpec((1, rows, width), lambda b: (b, 0, 0)),
            pl.BlockSpec((1, heads, hdim, dstate), lambda b: (b, 0, 0, 0)),
        ],
        out_specs=[
            pl.BlockSpec((1, rows, width), lambda b: (b, 0, 0)),
            pl.BlockSpec((1, heads, hdim, dstate), lambda b: (b, 0, 0, 0)),
        ],
        out_shape=[
            jax.ShapeDtypeStruct((nbat, rows, width), F32),
            jax.ShapeDtypeStruct(h0.shape, F32),
        ],
        compiler_params=_cparams("arbitrary"),
        name="ssd_sample_state",
    )(cd_flat, c_bm, b_bm, xd_bm, h0)


def _ssd_sample_post_kernel(ypart_ref, yoff_ref, expa_ref, z_ref, nw_ref, y_ref, *, groups, nt, nbat):
    for t in range(nt):
        rows = pl.ds(t * nbat, nbat)
        y = ypart_ref[t] + yoff_ref[:, t, :] * expa_ref[t]
        y_ref[rows, :] = _gated_group_norm(y, z_ref[rows, :], nw_ref[...], groups)


def _ssd_sample_post(ypart, yoff_bm, expa, proj_s, z_block, nw, groups):
    nt, nbat, width = ypart.shape
    rows = nt * nbat
    kern = functools.partial(_ssd_sample_post_kernel, groups=groups, nt=nt, nbat=nbat)
    return pl.pallas_call(
        kern,
        grid=(1,),
        in_specs=[_whole(ypart.shape), _whole(yoff_bm.shape), _whole(expa.shape),
                  pl.BlockSpec((rows, width), lambda i: (0, z_block)), _whole(nw.shape)],
        out_specs=_whole((rows, width)),
        out_shape=jax.ShapeDtypeStruct((rows, width), F32),
        compiler_params=_cparams("arbitrary"),
        name="ssd_sample_post",
    )(ypart, yoff_bm, expa, proj_s, nw)


def _merge_kernel(x_ref, g_ref, yc_ref, ya_ref, ys_ref, g1_ref, wc_ref, wa_ref, ws_ref, wo_ref, o_ref, *, d):
    gates = g_ref[0]
    merged = (_sigmoid(gates[:, :d]) * _dot(yc_ref[0].astype(BF16), wc_ref[...])
              + _sigmoid(gates[:, d:2 * d]) * _dot(ya_ref[0].astype(BF16), wa_ref[...])
              + _sigmoid(gates[:, 2 * d:]) * _dot(ys_ref[0].astype(BF16), ws_ref[...]))
    o_ref[0] = x_ref[0] + g1_ref[0] * _dot(merged.astype(BF16), wo_ref[...])


def _merge(x, proj, gate_block, yc, ya, ys, mod, wc, wa, ws, wo):
    nb, l, d = x.shape
    tm = _tile(l, 512)
    kern = functools.partial(_merge_kernel, d=d)
    tok = lambda w: pl.BlockSpec((1, tm, w), lambda b, i: (b, i, 0))
    full = lambda a: pl.BlockSpec(a.shape, lambda b, i: (0, 0))
    return pl.pallas_call(
        kern,
        grid=(nb, l // tm),
        in_specs=[
            tok(d),
            pl.BlockSpec((1, tm, 3 * d), lambda b, i: (b, i, gate_block)),
            tok(yc.shape[2]), tok(ya.shape[2]), tok(ys.shape[2]),
            _mod_spec(mod, tm, 2, d),
            full(wc), full(wa), full(ws), full(wo),
        ],
        out_specs=tok(d),
        out_shape=jax.ShapeDtypeStruct((nb, l, d), F32),
        compiler_params=_cparams("arbitrary", "arbitrary"),
        name="merge",
    )(x, proj, yc, ya, ys, mod, wc, wa, ws, wo)


def _finish(x, gate, f, nf_ref, final_norm):
    out = x + gate * f
    if final_norm:
        out = out * lax.rsqrt(jnp.mean(out * out, axis=-1, keepdims=True) + EPS) * nf_ref[...]
    return out


def _ffn_kernel(x_ref, sh_ref, sc_ref, g2_ref, nw_ref, nf_ref, wg_ref, wu_ref, wd_ref, o_ref, h_scr, acc,
                *, final_norm):
    f = pl.program_id(2)

    @pl.when(f == 0)
    def _():
        h_scr[...] = _rms_mod(x_ref[0], nw_ref[...], sc_ref[0], sh_ref[0]).astype(BF16)
        acc[...] = jnp.zeros_like(acc)

    h = h_scr[...]
    a = _silu(_dot(h, wg_ref[...])) * _dot(h, wu_ref[...])
    acc[...] += _dot(a.astype(BF16), wd_ref[...])

    @pl.when(f == pl.num_programs(2) - 1)
    def _():
        o_ref[0] = _finish(x_ref[0], g2_ref[0], acc[...], nf_ref, final_norm)


def _ffn(x, mod, norm_w, norm_final, wg, wu, wd, final_norm):
    nb, l, d = x.shape
    ff = wg.shape[1]
    tm = _tile(l, 1024)
    tf = _tile(ff, 512)
    kern = functools.partial(_ffn_kernel, final_norm=final_norm)
    vec = pl.BlockSpec((1, d), lambda b, i, f: (0, 0))
    return pl.pallas_call(
        kern,
        grid=(nb, l // tm, ff // tf),
        in_specs=[
            pl.BlockSpec((1, tm, d), lambda b, i, f: (b, i, 0)),
            _mod_spec(mod, tm, 3, d), _mod_spec(mod, tm, 4, d), _mod_spec(mod, tm, 5, d),
            vec, vec,
            pl.BlockSpec((d, tf), lambda b, i, f: (0, f)),
            pl.BlockSpec((d, tf), lambda b, i, f: (0, f)),
            pl.BlockSpec((tf, d), lambda b, i, f: (f, 0)),
        ],
        out_specs=pl.BlockSpec((1, tm, d), lambda b, i, f: (b, i, 0)),
        out_shape=jax.ShapeDtypeStruct((nb, l, d), F32),
        scratch_shapes=[pltpu.VMEM((tm, d), BF16), pltpu.VMEM((tm, d), F32)],
        compiler_params=_cparams("arbitrary", "arbitrary", "arbitrary"),
        name="ffn",
    )(x, mod, mod, mod, norm_w.reshape(1, d), norm_final.reshape(1, d), wg, wu, wd)


def _route_kernel(x_ref, sh_ref, sc_ref, nw_ref, wr_ref, w_ref, e_ref, *, n_experts):
    h = _rms_mod(x_ref[0], nw_ref[...], sc_ref[0], sh_ref[0])
    h_hi = h.astype(BF16)
    h_lo = (h - h_hi.astype(F32)).astype(BF16)
    wr = wr_ref[...]
    r_hi = wr.astype(BF16)
    r_lo = (wr - r_hi.astype(F32)).astype(BF16)
    logits = _dot(h_hi, r_hi) + _dot(h_lo, r_hi) + _dot(h_hi, r_lo)
    lane = lax.broadcasted_iota(jnp.int32, logits.shape, 1).astype(F32)
    neg = -jnp.inf
    lg = jnp.where(lane < n_experts, logits, neg)
    m1 = jnp.max(lg, axis=-1, keepdims=True)
    i1 = jnp.min(jnp.where(lg == m1, lane, float(LANES)), axis=-1, keepdims=True)
    rest = jnp.where(lane == i1, neg, lg)
    m2 = jnp.max(rest, axis=-1, keepdims=True)
    i2 = jnp.min(jnp.where(rest == m2, lane, float(LANES)), axis=-1, keepdims=True)
    e2 = jnp.exp(m2 - m1)
    w1 = 1.0 / (1.0 + e2)
    w2 = e2 / (1.0 + e2)
    w_ref[...] = jnp.where(lane == 0.0, w1, jnp.where(lane == 1.0, w2, 0.0))
    chosen = jnp.where(lane == 0.0, i1, jnp.where(lane == 1.0, i2, 0.0))
    e_ref[...] = chosen.T[:SUBLANES, :]


def _route(x, mod, norm_w, w_router_pad, n_experts):
    nb, l, d = x.shape
    tm = _tile(l, ROUTE_TILE)
    nt = l // tm
    kern = functools.partial(_route_kernel, n_experts=n_experts)
    return pl.pallas_call(
        kern,
        grid=(nb, nt),
        in_specs=[
            pl.BlockSpec((1, tm, d), lambda b, i: (b, i, 0)),
            _mod_spec(mod, tm, 3, d), _mod_spec(mod, tm, 4, d),
            pl.BlockSpec((1, d), lambda b, i: (0, 0)),
            pl.BlockSpec((d, LANES), lambda b, i: (0, 0)),
        ],
        out_specs=[
            pl.BlockSpec((tm, LANES), lambda b, i: (b * nt + i, 0)),
            pl.BlockSpec((SUBLANES, tm), lambda b, i: (0, b * nt + i)),
        ],
        out_shape=[
            jax.ShapeDtypeStruct((nb * l, LANES), F32),
            jax.ShapeDtypeStruct((SUBLANES, nb * l), F32),
        ],
        compiler_params=_cparams("arbitrary", "arbitrary"),
        name="moe_route",
    )(x, mod, mod, norm_w.reshape(1, d), w_router_pad)


def _plan_kernel(e_ref, pos_ref, te_ref, *, n_experts, tile, row_tile):
    steps = e_ref.shape[1] // tile
    sub = lax.broadcasted_iota(jnp.int32, (SUBLANES, tile), 0).astype(F32)
    sub_col = lax.broadcasted_iota(jnp.int32, (SUBLANES, 1), 0)

    def member(i):
        blk = e_ref[:, pl.ds(pl.multiple_of(i * tile, tile), tile)]
        e1, e2 = blk[0:1, :], blk[1:2, :]
        return e1, e2, jnp.where((sub == e1) | (sub == e2), 1.0, 0.0)

    def count_body(i, cnt):
        return cnt + jnp.sum(member(i)[2], axis=1, keepdims=True)

    cnt = lax.fori_loop(0, steps, count_body, jnp.zeros((SUBLANES, 1), F32))
    padded = jnp.floor((cnt + (row_tile - 1)) * (1.0 / row_tile)) * row_tile
    off = jnp.zeros((SUBLANES, 1), F32)
    run = jnp.zeros((1, 1), F32)
    for e in range(n_experts):
        off = jnp.where(sub_col == e, run, off)
        run = run + padded[e:e + 1, :]
    seg_end = off + padded

    r = lax.broadcasted_iota(jnp.int32, (tile, tile), 0)
    c = lax.broadcasted_iota(jnp.int32, (tile, tile), 1)
    before = jnp.where(r < c, 1.0, 0.0).astype(BF16)

    def pos_body(i, carry):
        e1, e2, m = member(i)
        val = off + carry + _dot(m.astype(BF16), before)
        p1 = jnp.sum(jnp.where(sub == e1, val, 0.0), axis=0, keepdims=True)
        p2 = jnp.sum(jnp.where(sub == e2, val, 0.0), axis=0, keepdims=True)
        rows = jnp.where(sub == 0.0, p1, jnp.where(sub == 1.0, p2, 0.0))
        pos_ref[:, pl.ds(pl.multiple_of(i * tile, tile), tile)] = rows.astype(jnp.int32)
        return carry + jnp.sum(m, axis=1, keepdims=True)

    lax.fori_loop(0, steps, pos_body, jnp.zeros((SUBLANES, 1), F32))

    sub_l = lax.broadcasted_iota(jnp.int32, (SUBLANES, LANES), 0)
    start = lax.broadcasted_iota(jnp.int32, (SUBLANES, LANES), 1).astype(F32) * row_tile
    owner = jnp.sum(jnp.where((seg_end <= start) & (sub_l < n_experts), 1.0, 0.0), axis=0, keepdims=True)
    owner = jnp.minimum(owner, n_experts - 1.0)
    used = run * (1.0 / row_tile)
    te_ref[...] = jnp.where(sub_l == 0, owner, jnp.where(sub_l == 1, used, 0.0)).astype(jnp.int32)


def _plan(e_all, n_experts, row_tile):
    t = e_all.shape[1]
    assert t % LANES == 0 and n_experts <= SUBLANES
    tile = _tile(t, ROUTE_TILE)
    kern = functools.partial(_plan_kernel, n_experts=n_experts, tile=tile, row_tile=row_tile)
    return pl.pallas_call(
        kern,
        out_shape=[
            jax.ShapeDtypeStruct((SUBLANES, t), jnp.int32),
            jax.ShapeDtypeStruct((SUBLANES, LANES), jnp.int32),
        ],
        compiler_params=pltpu.CompilerParams(vmem_limit_bytes=VMEM_LIMIT_BYTES),
        name="moe_plan",
    )(e_all)


def _scatter_kernel(p1_ref, p2_ref, x_ref, sh_ref, sc_ref, nw_ref, xs_in_ref, xs_ref, h_scr, sem, *, tm):
    del xs_in_ref
    h_scr[...] = _rms_mod(x_ref[0], nw_ref[...], sc_ref[0], sh_ref[0])

    def row_copy(t, p):
        return pltpu.make_async_copy(h_scr.at[pl.ds(t, 1), :], xs_ref.at[pl.ds(p, 1), :], sem)

    def issue(i, carry):
        for k in range(DMA_UNROLL):
            t = i * DMA_UNROLL + k
            row_copy(t, p1_ref[t]).start(priority=0)
            row_copy(t, p2_ref[t]).start(priority=1)
        return carry

    def drain(t, carry):
        row_copy(0, 0).wait()
        row_copy(0, 0).wait()
        return carry

    lax.fori_loop(0, tm // DMA_UNROLL, issue, 0)
    lax.fori_loop(0, tm, drain, 0, unroll=DMA_UNROLL)


def _scatter(x, mod, norm_w, p1, p2, xs_sorted):
    nb, l, d = x.shape
    tm = _tile(l, ROUTE_TILE)
    nt = l // tm
    kern = functools.partial(_scatter_kernel, tm=tm)
    idx = pl.BlockSpec((tm,), lambda b, i: (b * nt + i,), memory_space=pltpu.SMEM)
    return pl.pallas_call(
        kern,
        grid=(nb, nt),
        in_specs=[
            idx, idx,
            pl.BlockSpec((1, tm, d), lambda b, i: (b, i, 0)),
            _mod_spec(mod, tm, 3, d), _mod_spec(mod, tm, 4, d),
            pl.BlockSpec((1, d), lambda b, i: (0, 0)),
            pl.BlockSpec(memory_space=pl.ANY),
        ],
        out_specs=pl.BlockSpec(memory_space=pl.ANY),
        out_shape=jax.ShapeDtypeStruct(xs_sorted.shape, F32),
        scratch_shapes=[pltpu.VMEM((tm, d), F32), pltpu.SemaphoreType.DMA],
        input_output_aliases={6: 0},
        compiler_params=_cparams("arbitrary", "arbitrary"),
        name="moe_scatter",
    )(p1, p2, x, mod, mod, norm_w.reshape(1, d), xs_sorted)


def _group_ffn_kernel(te_ref, xs_ref, wg_ref, wu_ref, wd_ref, y_ref, h_scr, acc):
    j = pl.program_id(0)
    f = pl.program_id(1)

    @pl.when(j < te_ref[LANES])
    def _():
        @pl.when(f == 0)
        def _():
            h_scr[...] = xs_ref[...].astype(BF16)
            acc[...] = jnp.zeros_like(acc)

        h = h_scr[...]
        a = _silu(_dot(h, wg_ref[0])) * _dot(h, wu_ref[0])
        acc[...] += _dot(a.astype(BF16), wd_ref[0])

        @pl.when(f == pl.num_programs(1) - 1)
        def _():
            y_ref[...] = acc[...]

    @pl.when((j >= te_ref[LANES]) & (f == 0))
    def _():
        y_ref[...] = jnp.zeros_like(y_ref)


def _group_ffn(te_flat, xs_sorted, wg, wu, wd, row_tile):
    rows, d = xs_sorted.shape
    ff = wg.shape[2]
    tf = _tile(ff, 512)
    nf = ff // tf

    def tile_of(j, te):
        return jnp.minimum(j, te[LANES] - 1)

    def f_of(j, f, te):
        return jnp.where(j < te[LANES], f, nf - 1)

    grid_spec = pltpu.PrefetchScalarGridSpec(
        num_scalar_prefetch=1,
        grid=(rows // row_tile, nf),
        in_specs=[
            pl.BlockSpec((row_tile, d), lambda j, f, te: (tile_of(j, te), 0)),
            pl.BlockSpec((1, d, tf), lambda j, f, te: (te[tile_of(j, te)], 0, f_of(j, f, te))),
            pl.BlockSpec((1, d, tf), lambda j, f, te: (te[tile_of(j, te)], 0, f_of(j, f, te))),
            pl.BlockSpec((1, tf, d), lambda j, f, te: (te[tile_of(j, te)], f_of(j, f, te), 0)),
        ],
        out_specs=pl.BlockSpec((row_tile, d), lambda j, f, te: (j, 0)),
        scratch_shapes=[pltpu.VMEM((row_tile, d), BF16), pltpu.VMEM((row_tile, d), F32)],
    )
    return pl.pallas_call(
        _group_ffn_kernel,
        grid_spec=grid_spec,
        out_shape=jax.ShapeDtypeStruct((rows, d), F32),
        compiler_params=_cparams("arbitrary", "arbitrary"),
        name="moe_group_ffn",
    )(te_flat, xs_sorted, wg, wu, wd)


def _combine_kernel(p1_ref, p2_ref, x_ref, g2_ref, w_ref, nf_ref, y_hbm, o_ref, buf, sem, *, tm, final_norm):
    def row_copy(k, t, p):
        return pltpu.make_async_copy(y_hbm.at[pl.ds(p, 1), :], buf.at[k, pl.ds(t, 1), :], sem)

    def issue(i, carry):
        for k in range(DMA_UNROLL):
            t = i * DMA_UNROLL + k
            row_copy(0, t, p1_ref[t]).start(priority=0)
            row_copy(1, t, p2_ref[t]).start(priority=1)
        return carry

    def drain(t, carry):
        row_copy(0, 0, 0).wait()
        row_copy(0, 0, 0).wait()
        return carry

    lax.fori_loop(0, tm // DMA_UNROLL, issue, 0)
    lax.fori_loop(0, tm, drain, 0, unroll=DMA_UNROLL)
    w = w_ref[...]
    f = w[:, 0:1] * buf[0] + w[:, 1:2] * buf[1]
    o_ref[0] = _finish(x_ref[0], g2_ref[0], f, nf_ref, final_norm)


def _combine(x, mod, w_cols, p1, p2, y_sorted, norm_final, final_norm):
    nb, l, d = x.shape
    tm = _tile(l, ROUTE_TILE)
    nt = l // tm
    kern = functools.partial(_combine_kernel, tm=tm, final_norm=final_norm)
    idx = pl.BlockSpec((tm,), lambda b, i: (b * nt + i,), memory_space=pltpu.SMEM)
    return pl.pallas_call(
        kern,
        grid=(nb, nt),
        in_specs=[
            idx, idx,
            pl.BlockSpec((1, tm, d), lambda b, i: (b, i, 0)),
            _mod_spec(mod, tm, 5, d),
            pl.BlockSpec((tm, LANES), lambda b, i: (b * nt + i, 0)),
            pl.BlockSpec((1, d), lambda b, i: (0, 0)),
            pl.BlockSpec(memory_space=pl.ANY),
        ],
        out_specs=pl.BlockSpec((1, tm, d), lambda b, i: (b, i, 0)),
        out_shape=jax.ShapeDtypeStruct((nb, l, d), F32),
        scratch_shapes=[pltpu.VMEM((2, tm, d), F32), pltpu.SemaphoreType.DMA],
        compiler_params=_cparams("arbitrary", "arbitrary"),
        name="moe_combine",
    )(p1, p2, x, mod, w_cols, norm_final.reshape(1, d), y_sorted)


def _moe(groups, norm_w, norm_final, w_router_pad, wg, wu, wd, final_norm):
    n_experts = wg.shape[0]
    d = groups[0][0].shape[2]
    routed = [_route(x, mod, norm_w, w_router_pad, n_experts) for x, mod in groups]
    e_all = jnp.concatenate([e for _, e in routed], axis=1)
    t = e_all.shape[1]
    pos, te = _plan(e_all, n_experts, MOE_ROW_TILE)
    te_flat = te.reshape(SUBLANES * LANES)
    n_tiles = TOP_K * t // MOE_ROW_TILE + n_experts
    assert TOP_K * t % MOE_ROW_TILE == 0 and n_tiles <= LANES
    xs_sorted = jnp.zeros((n_tiles * MOE_ROW_TILE, d), F32)
    spans, start = [], 0
    for x, _ in groups:
        n = x.shape[0] * x.shape[1]
        spans.append((start, start + n))
        start += n
    for (x, mod), (lo, hi) in zip(groups, spans):
        xs_sorted = _scatter(x, mod, norm_w, pos[0, lo:hi], pos[1, lo:hi], xs_sorted)
    y_sorted = _group_ffn(te_flat, xs_sorted, wg, wu, wd, MOE_ROW_TILE)
    return [_combine(x, mod, w_cols, pos[0, lo:hi], pos[1, lo:hi], y_sorted, norm_final, final_norm)
            for (x, mod), (w_cols, _), (lo, hi) in zip(groups, routed, spans)]


def kernel(x_prompt, x_sample, c_prompt, c_sample, cache_k, cache_v, state_conv, state_ssm_conv, state_ssm,
           w_mod, b_mod, norm_mix, norm_ffn, norm_final, w_in, w_sconv, sinks, ssm_conv_w, ssm_conv_b,
           dt_bias, a_log, d_skip, ssm_norm, w_br_conv, w_br_attn, w_br_ssm, w_o,
           ffn_w_gate, ffn_w_up, ffn_w_down, router, moe_w_gate, moe_w_up, moe_w_down):
    nbp, seq, d = x_prompt.shape
    nbat, nt, _ = x_sample.shape
    depth = w_mod.shape[0]
    cwid = w_sconv.shape[2]
    n_heads = sinks.shape[1]
    window, n_kv, head_dim = cache_k.shape[2:]
    heads, hdim, dstate = state_ssm.shape[2:]
    width = heads * hdim
    xbcw = ssm_conv_w.shape[2]
    groups = SSM_GROUPS
    n_experts = router.shape[2]
    aw = n_heads * head_dim
    kvw = n_kv * head_dim
    dims = (width, dstate, hdim, groups, xbcw)
    assert window == ATTN_BLOCK and seq % ATTN_BLOCK == 0 and xbcw == width + 2 * groups * dstate

    o_cv, o_q, o_k, o_z, o_xbc = 0, 3 * cwid, 3 * cwid + aw, 3 * cwid + aw + 2 * kvw, 3 * cwid + aw + 2 * kvw + width
    o_dt = o_xbc + xbcw
    o_g = o_dt + heads
    n_in = w_in.shape[2]
    dt_pad = 2 * LANES - heads

    def regroup(w):
        return jnp.concatenate(
            [w[:, o_g:n_in], w[:, o_cv:o_q], w[:, o_xbc:o_dt], w[:, o_z:o_xbc], w[:, o_q:o_k], w[:, o_k:o_z],
             w[:, o_dt:o_g], jnp.zeros((d, dt_pad), w.dtype)], axis=1).astype(BF16)

    p_gate, p_conv, p_xbc, p_z = 0, 3 * d, 3 * d + 3 * cwid, 3 * d + 3 * cwid + xbcw
    p_q = p_z + width
    p_kv = p_q + aw
    p_dt = p_kv + 2 * kvw
    blk = lambda off, w: off // w
    assert all(off % w == 0 for off, w in ((p_conv, 3 * cwid), (p_xbc, xbcw), (p_z, width), (p_q, aw),
                                           (p_kv, 2 * kvw), (p_dt, LANES)))

    n_c = nbp + nbat
    c_rows = -(-n_c // SUBLANES) * SUBLANES
    c_all = jnp.pad(jnp.concatenate([c_prompt, c_sample], axis=0), ((0, c_rows - n_c), (0, 0)))
    mod_all = _modulation(c_all, w_mod, b_mod)

    xs_tm = x_sample.transpose(1, 0, 2).reshape(1, nt * nbat, d)
    srows = 2 * SUBLANES

    pos_p = jnp.arange(seq, dtype=jnp.int32)
    pos_s = PAST_LEN + jnp.arange(SUBLANES, dtype=jnp.int32)
    tab_p = _rope_tables(pos_p, head_dim)
    tab_s = _rope_tables(pos_s, head_dim)

    expand = jnp.repeat(jnp.eye(heads, dtype=F32), hdim, axis=1).astype(BF16)
    pad_h = lambda v: jnp.pad(v, (0, LANES - heads)).reshape(1, LANES)

    xp, xs = x_prompt, xs_tm
    outs = {k: [] for k in ("kp", "vp", "cp", "scp", "sp", "ks", "vs", "cs", "scs", "ss")}
    for i in range(depth):
        w_in_i = regroup(w_in[i])
        wc, wa, ws, wo = (w[i].astype(BF16) for w in (w_br_conv, w_br_attn, w_br_ssm, w_o))
        mod_p = mod_all[i, :nbp].reshape(nbp, 1, 6 * d)
        mod_s = jnp.tile(mod_all[i, nbp:n_c], (nt, 1)).reshape(1, nt * nbat, 6 * d)
        ssm_params = (ssm_conv_w[i], ssm_conv_b[i].reshape(1, xbcw), pad_h(dt_bias[i]), pad_h(a_log[i]),
                      jnp.repeat(d_skip[i], hdim).reshape(1, width))
        nw_ssm = ssm_norm[i].reshape(1, width)

        proj = _inproj(xp, mod_p, norm_mix[i], w_in_i)
        y_conv, conv_tail = _conv_prompt(proj, w_sconv[i], blk(p_conv, 3 * cwid), cwid)
        y_attn, kv_last = _attn_prompt(proj, sinks[i], tab_p, blk(p_q, aw), blk(p_kv, 2 * kvw),
                                       n_heads, n_kv, head_dim)
        y_ssm, h_fin = _ssd_prompt(proj, (blk(p_xbc, xbcw), blk(p_z, width), blk(p_dt, LANES)),
                                   ssm_params + (nw_ssm, expand), dims)
        xp = _merge(xp, proj, blk(p_gate, 3 * d), y_conv, y_attn, y_ssm, mod_p, wc, wa, ws, wo)
        outs["kp"].append(kv_last[:, :, :kvw].reshape(nbp, window, n_kv, head_dim))
        outs["vp"].append(kv_last[:, :, kvw:].reshape(nbp, window, n_kv, head_dim))
        outs["cp"].append(conv_tail[:, SUBLANES - (w_sconv.shape[1] - 1):])
        outs["scp"].append(proj[:, seq - (ssm_conv_w.shape[1] - 1):, p_xbc:p_xbc + xbcw])
        outs["sp"].append(h_fin.reshape(nbp, heads, hdim, dstate))

        proj_s = _inproj(xs, mod_s, norm_mix[i], w_in_i)[0]
        proj_tm = proj_s.reshape(nt, nbat, proj_s.shape[1])
        y_conv_s, u_s = _conv_sample(proj_s, blk(p_conv, 3 * cwid), state_conv[i].transpose(1, 0, 2),
                                     w_sconv[i], nt, nbat, cwid)
        y_attn_tm, k_new = _attn_sample(proj_tm, cache_k[i].reshape(nbat, window, kvw),
                                        cache_v[i].reshape(nbat, window, kvw), sinks[i], tab_s,
                                        blk(p_q, aw), blk(p_kv, 2 * kvw), n_heads, n_kv, head_dim)
        ypart, expa, xd_bm, b_bm, c_bm, cd = _ssd_sample_pre(
            proj_s, (blk(p_xbc, xbcw), blk(p_dt, LANES)), state_ssm_conv[i].transpose(1, 0, 2),
            ssm_params + (expand,), nt, nbat, dims, srows)
        yoff_bm, h_new = _ssd_sample_state(cd[:, :heads].reshape(nbat * heads), c_bm, b_bm, xd_bm,
                                           state_ssm[i], dims)
        y_ssm_s = _ssd_sample_post(ypart, yoff_bm, expa, proj_s, blk(p_z, width), nw_ssm, groups)
        xs = _merge(xs, proj_s[None], blk(p_gate, 3 * d), y_conv_s[None], y_attn_tm.reshape(1, nt * nbat, aw),
                    y_ssm_s[None], mod_s, wc, wa, ws, wo)
        k_rows = k_new.transpose(1, 0, 2).reshape(nbat, nt, n_kv, head_dim)
        v_rows = proj_tm[:, :, p_kv + kvw:p_kv + 2 * kvw].transpose(1, 0, 2).reshape(nbat, nt, n_kv, head_dim)
        outs["ks"].append(jnp.concatenate([cache_k[i][:, nt:], k_rows], axis=1))
        outs["vs"].append(jnp.concatenate([cache_v[i][:, nt:], v_rows], axis=1))
        outs["cs"].append(u_s[nt - (w_sconv.shape[1] - 1):].transpose(1, 0, 2))
        kc = ssm_conv_w.shape[1] - 1
        outs["scs"].append(proj_tm[nt - kc:, :, p_xbc:p_xbc + xbcw].transpose(1, 0, 2))
        outs["ss"].append(h_new)

        last = i == depth - 1
        jj = i // 2
        if i % 2 == 0:
            wg, wu, wd = (w[jj].astype(BF16) for w in (ffn_w_gate, ffn_w_up, ffn_w_down))
            xp = _ffn(xp, mod_p, norm_ffn[i], norm_final, wg, wu, wd, last)
            xs = _ffn(xs, mod_s, norm_ffn[i], norm_final, wg, wu, wd, last)
        else:
            wg, wu, wd = (w[jj].astype(BF16) for w in (moe_w_gate, moe_w_up, moe_w_down))
            wr = jnp.pad(router[jj], ((0, 0), (0, LANES - n_experts)))
            xp, xs = _moe([(xp, mod_p), (xs, mod_s)], norm_ffn[i], norm_final, wr, wg, wu, wd, last)

    y_sample = xs.reshape(nt, nbat, d).transpose(1, 0, 2)
    st = lambda k: jnp.stack(outs[k])
    return (xp, y_sample, st("kp"), st("vp"), st("cp"), st("scp"), st("sp"),
            st("ks"), st("vs"), st("cs"), st("scs"), st("ss"))
```

```python
import functools
import math

import jax
import jax.numpy as jnp
from jax import lax
from jax.experimental import pallas as pl
from jax.experimental.pallas import tpu as pltpu

F32 = jnp.float32
BF16 = jnp.bfloat16

PAST_LEN = 8192
ROPE_THETA = 500000.0
EPS = 1e-6
TOP_K = 2
SSM_GROUPS = 2
ATTN_BLOCK = 128
ATTN_BLOCKS_PER_STEP = 1
SSD_CHUNK = 128
MOE_ROW_TILE = 1024
ROUTE_TILE = 512
DMA_UNROLL = 8

LANES = 128
SUBLANES = 8
VMEM_LIMIT_BYTES = 56 * 1024 * 1024


def _cparams(*semantics):
    return pltpu.CompilerParams(dimension_semantics=semantics, vmem_limit_bytes=VMEM_LIMIT_BYTES)


def _tile(n, pref):
    if n <= pref:
        return n
    t = pref
    while n % t:
        t //= 2
    return t


def _silu(x):
    return x / (1.0 + jnp.exp(-x))


def _sigmoid(x):
    return 1.0 / (1.0 + jnp.exp(-x))


def _softplus(x):
    return jnp.maximum(x, 0.0) + jnp.log1p(jnp.exp(-jnp.abs(x)))


def _dot(a, b):
    return jnp.dot(a, b, preferred_element_type=F32)


def _dot_nt(a, b):
    return lax.dot_general(a, b, (((1,), (1,)), ((), ())), preferred_element_type=F32)


def _dot_tn(a, b):
    return lax.dot_general(a, b, (((0,), (0,)), ((), ())), preferred_element_type=F32)


def _split3(x):
    hi = x.astype(BF16)
    r1 = x - hi.astype(F32)
    mid = r1.astype(BF16)
    lo = (r1 - mid.astype(F32)).astype(BF16)
    return hi, mid, lo


def _dot_exact_rhs01(x, m01):
    hi, mid, lo = _split3(x)
    return _dot(hi, m01) + _dot(mid, m01) + _dot(lo, m01)


def _dot_exact_lhs01(m01, x):
    hi, mid, lo = _split3(x)
    return _dot(m01, hi) + _dot(m01, mid) + _dot(m01, lo)


def _rms_mod(x, norm_w, scale, shift):
    xn = x * lax.rsqrt(jnp.mean(x * x, axis=-1, keepdims=True) + EPS)
    return (xn * norm_w) * (1.0 + scale) + shift


def _mod_kernel(c_ref, w_ref, b_ref, o_ref):
    a = _silu(c_ref[...]).astype(BF16)
    o_ref[0] = _dot(a, w_ref[0].astype(BF16)) + b_ref[0]


def _modulation(c_all, w_mod, b_mod):
    depth, d, n = w_mod.shape
    rows = c_all.shape[0]
    tn = _tile(n, 1024)
    return pl.pallas_call(
        _mod_kernel,
        grid=(depth, n // tn),
        in_specs=[
            pl.BlockSpec((rows, d), lambda i, j: (0, 0)),
            pl.BlockSpec((1, d, tn), lambda i, j: (i, 0, j)),
            pl.BlockSpec((1, 1, tn), lambda i, j: (i, 0, j)),
        ],
        out_specs=pl.BlockSpec((1, rows, tn), lambda i, j: (i, 0, j)),
        out_shape=jax.ShapeDtypeStruct((depth, rows, n), F32),
        compiler_params=_cparams("arbitrary", "arbitrary"),
        name="modulation",
    )(c_all, w_mod, b_mod.reshape(depth, 1, n))


def _mod_spec(mod, tm, chunk, d):
    if mod.shape[1] == 1:
        return pl.BlockSpec((1, 1, d), lambda b, i, *_: (b, 0, chunk))
    return pl.BlockSpec((1, tm, d), lambda b, i, *_: (b, i, chunk))


def _inproj_kernel(x_ref, sh_ref, sc_ref, nw_ref, w_ref, o_ref, h_scr):
    @pl.when(pl.program_id(2) == 0)
    def _():
        h_scr[...] = _rms_mod(x_ref[0], nw_ref[...], sc_ref[0], sh_ref[0]).astype(BF16)

    o_ref[0] = _dot(h_scr[...], w_ref[...])


def _inproj(x, mod, norm_w, w):
    nb, l, d = x.shape
    n = w.shape[1]
    tm = _tile(l, 1024)
    tn = _tile(n, 1024)
    return pl.pallas_call(
        _inproj_kernel,
        grid=(nb, l // tm, n // tn),
        in_specs=[
            pl.BlockSpec((1, tm, d), lambda b, i, j: (b, i, 0)),
            _mod_spec(mod, tm, 0, d),
            _mod_spec(mod, tm, 1, d),
            pl.BlockSpec((1, d), lambda b, i, j: (0, 0)),
            pl.BlockSpec((d, tn), lambda b, i, j: (0, j)),
        ],
        out_specs=pl.BlockSpec((1, tm, tn), lambda b, i, j: (b, i, j)),
        out_shape=jax.ShapeDtypeStruct((nb, l, n), F32),
        scratch_shapes=[pltpu.VMEM((tm, d), BF16)],
        compiler_params=_cparams("arbitrary", "arbitrary", "arbitrary"),
        name="inproj",
    )(x, mod, mod, norm_w.reshape(1, d), w)


def _conv_prompt_kernel(p_ref, w_ref, y_ref, st_ref, ext, *, tm, cw):
    @pl.when(pl.program_id(1) == 0)
    def _():
        ext[0:SUBLANES, :] = jnp.zeros((SUBLANES, cw), F32)

    p = p_ref[0]
    gate_b, c, xx = p[:, :cw], p[:, cw:2 * cw], p[:, 2 * cw:]
    u = c * xx
    ext[SUBLANES:, :] = u
    w = w_ref[...]
    y = w[0:1] * ext[pl.ds(SUBLANES - 2, tm), :] + w[1:2] * ext[pl.ds(SUBLANES - 1, tm), :] + w[2:3] * u
    y_ref[0] = gate_b * y
    tail = u[tm - SUBLANES:, :]
    ext[0:SUBLANES, :] = tail
    st_ref[0] = tail


def _conv_prompt(proj, w, col_block, cw):
    nb, l, _ = proj.shape
    tm = _tile(l, 512)
    kern = functools.partial(_conv_prompt_kernel, tm=tm, cw=cw)
    return pl.pallas_call(
        kern,
        grid=(nb, l // tm),
        in_specs=[
            pl.BlockSpec((1, tm, 3 * cw), lambda b, i: (b, i, col_block)),
            pl.BlockSpec(w.shape, lambda b, i: (0, 0)),
        ],
        out_specs=[
            pl.BlockSpec((1, tm, cw), lambda b, i: (b, i, 0)),
            pl.BlockSpec((1, SUBLANES, cw), lambda b, i: (b, 0, 0)),
        ],
        out_shape=[
            jax.ShapeDtypeStruct((nb, l, cw), F32),
            jax.ShapeDtypeStruct((nb, SUBLANES, cw), F32),
        ],
        scratch_shapes=[pltpu.VMEM((tm + SUBLANES, cw), F32)],
        compiler_params=_cparams("arbitrary", "arbitrary"),
        name="conv_prompt",
    )(proj, w)


def _conv_sample_kernel(p_ref, st_ref, w_ref, y_ref, u_ref, *, nt, nbat, cw, k):
    w = w_ref[...]
    full = [st_ref[j] for j in range(k - 1)]
    gates = []
    for t in range(nt):
        p = p_ref[pl.ds(t * nbat, nbat), :]
        gates.append(p[:, :cw])
        u = p[:, cw:2 * cw] * p[:, 2 * cw:]
        u_ref[t] = u
        full.append(u)
    for t in range(nt):
        acc = w[0:1] * full[t]
        for j in range(1, k):
            acc = acc + w[j:j + 1] * full[t + j]
        y_ref[pl.ds(t * nbat, nbat), :] = gates[t] * acc


def _whole(shape):
    return pl.BlockSpec(shape, lambda i: (0,) * len(shape))


def _conv_sample(proj_s, col_block, state_tm, w, nt, nbat, cw):
    k = w.shape[0]
    rows = nt * nbat
    kern = functools.partial(_conv_sample_kernel, nt=nt, nbat=nbat, cw=cw, k=k)
    return pl.pallas_call(
        kern,
        grid=(1,),
        in_specs=[pl.BlockSpec((rows, 3 * cw), lambda i: (0, col_block)), _whole(state_tm.shape), _whole(w.shape)],
        out_specs=[_whole((rows, cw)), _whole((nt, nbat, cw))],
        out_shape=[
            jax.ShapeDtypeStruct((rows, cw), F32),
            jax.ShapeDtypeStruct((nt, nbat, cw), F32),
        ],
        compiler_params=_cparams("arbitrary"),
        name="conv_sample",
    )(proj_s, state_tm, w)


def _rope(x, cos, sin_lo, sin_hi, half_rot):
    return (x * cos + pltpu.roll(x, LANES - half_rot, 1) * sin_lo
            + pltpu.roll(x, half_rot, 1) * sin_hi)


def _attn_core(q, kcat, vcat, sinks_ref, valid, cos, sin_lo, sin_hi, *, n_heads, group, head_dim):
    tq = q.shape[0]
    half_rot = head_dim // 8
    heads_per_slab = LANES // head_dim
    scale = head_dim ** -0.5
    lane = lax.broadcasted_iota(jnp.int32, (tq, LANES), 1)
    k_bf = [kcat.astype(BF16), pltpu.roll(kcat, head_dim, 1).astype(BF16)]
    v_bf = [vcat.astype(BF16), pltpu.roll(vcat, head_dim, 1).astype(BF16)]
    slabs = []
    for s in range(n_heads // heads_per_slab):
        qs = _rope(q[:, s * LANES:(s + 1) * LANES], cos, sin_lo, sin_hi, half_rot)
        out = jnp.zeros((tq, LANES), F32)
        for half in range(heads_per_slab):
            h = s * heads_per_slab + half
            g = h // group
            in_head = (lane >= half * head_dim) & (lane < (half + 1) * head_dim)
            qm = jnp.where(in_head, qs, 0.0).astype(BF16)
            swap = 0 if (g % heads_per_slab) == half else 1
            sc = _dot_nt(qm, k_bf[swap]) * scale
            sc = jnp.where(valid, sc, -1e30)
            sink = sinks_ref[h]
            m = jnp.maximum(jnp.max(sc, axis=-1, keepdims=True), sink)
            p = jnp.exp(sc - m)
            p = p / (jnp.sum(p, axis=-1, keepdims=True) + jnp.exp(sink - m))
            o = _dot(p.astype(BF16), v_bf[swap])
            out = jnp.where(in_head, o, out)
        slabs.append(out)
    return slabs


def _attn_prompt_kernel(sinks_ref, q_ref, kv_ref, cos_ref, slo_ref, shi_ref, y_ref, last_ref, kprev, vprev,
                        *, n_heads, group, head_dim):
    j = pl.program_id(1)
    tq = kprev.shape[0]

    @pl.when(j == 0)
    def _():
        kprev[...] = jnp.zeros_like(kprev)
        vprev[...] = jnp.zeros_like(vprev)

    r = lax.broadcasted_iota(jnp.int32, (tq, 2 * tq), 0)
    c = lax.broadcasted_iota(jnp.int32, (tq, 2 * tq), 1)
    band = (c >= r) & (c <= r + tq)
    k_prev, v_prev = kprev[...], vprev[...]
    for blk in range(q_ref.shape[1] // tq):
        rows = pl.ds(blk * tq, tq)
        cos, slo, shi = cos_ref[rows, :], slo_ref[rows, :], shi_ref[rows, :]
        kv = kv_ref[0, rows, :]
        k_rot = _rope(kv[:, :LANES], cos, slo, shi, head_dim // 8)
        v = kv[:, LANES:]
        kcat = jnp.concatenate([k_prev, k_rot], axis=0)
        vcat = jnp.concatenate([v_prev, v], axis=0)
        if blk == 0:
            valid = band & (c >= jnp.where(j > 0, 0, tq))
        else:
            valid = band
        slabs = _attn_core(q_ref[0, rows, :], kcat, vcat, sinks_ref, valid, cos, slo, shi,
                           n_heads=n_heads, group=group, head_dim=head_dim)
        for s, o in enumerate(slabs):
            y_ref[0, rows, s * LANES:(s + 1) * LANES] = o
        k_prev, v_prev = k_rot, v
    kprev[...] = k_prev
    vprev[...] = v_prev
    last_ref[0, :, :LANES] = k_prev
    last_ref[0, :, LANES:] = v_prev


def _attn_prompt(proj, sinks, tables, q_block, kv_block, n_heads, n_kv, head_dim):
    nb, l, _ = proj.shape
    tq = ATTN_BLOCK
    qw = n_heads * head_dim
    kvw = 2 * n_kv * head_dim
    assert n_kv * head_dim == LANES
    kern = functools.partial(_attn_prompt_kernel, n_heads=n_heads, group=n_heads // n_kv, head_dim=head_dim)
    ts = _tile(l, ATTN_BLOCKS_PER_STEP * tq)
    tab_spec = pl.BlockSpec((ts, LANES), lambda b, j: (j, 0))
    return pl.pallas_call(
        kern,
        grid=(nb, l // ts),
        in_specs=[
            pl.BlockSpec(memory_space=pltpu.SMEM),
            pl.BlockSpec((1, ts, qw), lambda b, j: (b, j, q_block)),
            pl.BlockSpec((1, ts, kvw), lambda b, j: (b, j, kv_block)),
            tab_spec, tab_spec, tab_spec,
        ],
        out_specs=[
            pl.BlockSpec((1, ts, qw), lambda b, j: (b, j, 0)),
            pl.BlockSpec((1, tq, kvw), lambda b, j: (b, 0, 0)),
        ],
        out_shape=[
            jax.ShapeDtypeStruct((nb, l, qw), F32),
            jax.ShapeDtypeStruct((nb, tq, kvw), F32),
        ],
        scratch_shapes=[pltpu.VMEM((tq, LANES), F32), pltpu.VMEM((tq, LANES), F32)],
        compiler_params=_cparams("arbitrary", "arbitrary"),
        name="attn_prompt",
    )(sinks, proj, proj, *tables)


def _attn_sample_kernel(sinks_ref, q_ref, kv_ref, ck_ref, cv_ref, cos_ref, slo_ref, shi_ref, y_ref, knew_ref,
                        qh, kc, vc, ob, *, n_heads, group, head_dim, nt):
    gb, window = ck_ref.shape[0], ck_ref.shape[1]
    half_rot = head_dim // 8
    heads_per_slab = LANES // head_dim
    nq = n_heads * SUBLANES
    nk = kc.shape[1]
    lane = lax.broadcasted_iota(jnp.int32, (gb, LANES), 1)

    @pl.when(pl.program_id(0) == 0)
    def _():
        qh[...] = jnp.zeros_like(qh)
        kc[...] = jnp.zeros_like(kc)
        vc[...] = jnp.zeros_like(vc)

    kc[:, 0:window, :] = ck_ref[...]
    vc[:, 0:window, :] = cv_ref[...]
    for t in range(nt):
        cos, slo, shi = cos_ref[t:t + 1, :], slo_ref[t:t + 1, :], shi_ref[t:t + 1, :]
        kv = kv_ref[t]
        k_rot = _rope(kv[:, :LANES], cos, slo, shi, half_rot)
        knew_ref[t] = k_rot
        kc[:, window + t, :] = k_rot
        vc[:, window + t, :] = kv[:, LANES:]
        for s in range(n_heads // heads_per_slab):
            qs = _rope(q_ref[t][:, s * LANES:(s + 1) * LANES], cos, slo, shi, half_rot)
            qs_swapped = pltpu.roll(qs, head_dim, 1)
            for half in range(heads_per_slab):
                h = s * heads_per_slab + half
                g = (h // group) % heads_per_slab
                in_kv_half = (lane >= g * head_dim) & (lane < (g + 1) * head_dim)
                qh[:, h * SUBLANES + t, :] = jnp.where(in_kv_half, qs if g == half else qs_swapped, 0.0)

    sc = jnp.einsum("bqd,bkd->bqk", qh[...].astype(BF16), kc[...].astype(BF16),
                    preferred_element_type=F32) * (head_dim ** -0.5)
    r = lax.broadcasted_iota(jnp.int32, (nq, nk), 0) % SUBLANES
    c = lax.broadcasted_iota(jnp.int32, (nq, nk), 1)
    valid = (c >= r) & (c <= r + window)
    sc = jnp.where(valid[None], sc, -1e30)
    row_head = lax.broadcasted_iota(jnp.int32, (nq, 1), 0) // SUBLANES
    sink = jnp.zeros((nq, 1), F32)
    for h in range(n_heads):
        sink = jnp.where(row_head == h, sinks_ref[h], sink)
    m = jnp.maximum(jnp.max(sc, axis=-1, keepdims=True), sink[None])
    p = jnp.exp(sc - m)
    p = p / (jnp.sum(p, axis=-1, keepdims=True) + jnp.exp(sink[None] - m))
    ob[...] = jnp.einsum("bqk,bkd->bqd", p.astype(BF16), vc[...].astype(BF16), preferred_element_type=F32)

    for t in range(nt):
        for s in range(n_heads // heads_per_slab):
            out = jnp.zeros((gb, LANES), F32)
            for half in range(heads_per_slab):
                h = s * heads_per_slab + half
                g = (h // group) % heads_per_slab
                o = ob[:, h * SUBLANES + t, :]
                if g != half:
                    o = pltpu.roll(o, head_dim, 1)
                out = jnp.where((lane >= half * head_dim) & (lane < (half + 1) * head_dim), o, out)
            y_ref[t, :, s * LANES:(s + 1) * LANES] = out


def _attn_sample(proj_tm, ck, cv, sinks, tables, q_block, kv_block, n_heads, n_kv, head_dim):
    nt, nbat, _ = proj_tm.shape
    window = ck.shape[1]
    qw = n_heads * head_dim
    assert n_kv * head_dim == LANES and nt <= SUBLANES
    gb = _tile(nbat, 16)
    nk = window + 2 * SUBLANES
    kern = functools.partial(_attn_sample_kernel, n_heads=n_heads, group=n_heads // n_kv, head_dim=head_dim, nt=nt)
    tab_spec = pl.BlockSpec((SUBLANES, LANES), lambda b: (0, 0))
    return pl.pallas_call(
        kern,
        grid=(nbat // gb,),
        in_specs=[
            pl.BlockSpec(memory_space=pltpu.SMEM),
            pl.BlockSpec((nt, gb, qw), lambda b: (0, b, q_block)),
            pl.BlockSpec((nt, gb, 2 * LANES), lambda b: (0, b, kv_block)),
            pl.BlockSpec((gb, window, LANES), lambda b: (b, 0, 0)),
            pl.BlockSpec((gb, window, LANES), lambda b: (b, 0, 0)),
            tab_spec, tab_spec, tab_spec,
        ],
        out_specs=[
            pl.BlockSpec((nt, gb, qw), lambda b: (0, b, 0)),
            pl.BlockSpec((nt, gb, LANES), lambda b: (0, b, 0)),
        ],
        out_shape=[
            jax.ShapeDtypeStruct((nt, nbat, qw), F32),
            jax.ShapeDtypeStruct((nt, nbat, LANES), F32),
        ],
        scratch_shapes=[
            pltpu.VMEM((gb, n_heads * SUBLANES, LANES), F32),
            pltpu.VMEM((gb, nk, LANES), F32), pltpu.VMEM((gb, nk, LANES), F32),
            pltpu.VMEM((gb, n_heads * SUBLANES, LANES), F32),
        ],
        compiler_params=_cparams("arbitrary"),
        name="attn_sample",
    )(sinks, proj_tm, proj_tm, ck, cv, *tables)


def _rope_tables(pos, head_dim):
    rot = head_dim // 4
    half = rot // 2
    inv = jnp.exp(-(2.0 * jnp.arange(half, dtype=F32) / rot) * math.log(ROPE_THETA))
    ang = pos.astype(F32)[:, None] * inv[None, :]
    cos, sin = jnp.cos(ang), jnp.sin(ang)
    n = pos.shape[0]
    pad = jnp.zeros((n, head_dim - rot), F32)
    zeros = jnp.zeros((n, half), F32)
    cos_h = jnp.concatenate([cos, cos, pad + 1.0], axis=1)
    lo_h = jnp.concatenate([-sin, zeros, pad], axis=1)
    hi_h = jnp.concatenate([zeros, sin, pad], axis=1)
    reps = LANES // head_dim
    return tuple(jnp.tile(t, (1, reps)) for t in (cos_h, lo_h, hi_h))


def _gated_group_norm(y, z, norm_w, groups):
    y = y * _silu(z)
    gw = y.shape[1] // groups
    parts = []
    for g in range(groups):
        yg = y[:, g * gw:(g + 1) * gw]
        parts.append(yg * lax.rsqrt(jnp.mean(yg * yg, axis=-1, keepdims=True) + EPS))
    return jnp.concatenate(parts, axis=1) * norm_w


def _ssd_prompt_kernel(xbc_ref, z_ref, dt_ref, cw_ref, cb_ref, dtb_ref, alog_ref, dsk_ref, nw_ref, e_ref,
                       y_ref, hfin_ref, ext, ht, *, q, width, dstate, hdim, groups):
    j = pl.program_id(1)
    kconv = cw_ref.shape[0]

    @pl.when(j == 0)
    def _():
        ext[0:SUBLANES, :] = jnp.zeros((SUBLANES, ext.shape[1]), F32)
        ht[...] = jnp.zeros_like(ht)

    xbc = xbc_ref[0]
    ext[SUBLANES:, :] = xbc
    cw = cw_ref[...]
    conv = cw[kconv - 1:kconv] * xbc + cb_ref[...]
    for t in range(kconv - 1):
        conv = conv + cw[t:t + 1] * ext[pl.ds(SUBLANES - (kconv - 1) + t, q), :]
    ext[0:SUBLANES, :] = xbc[q - SUBLANES:, :]
    act = _silu(conv)
    xs = act[:, :width]
    bm = act[:, width:width + groups * dstate]
    cm = act[:, width + groups * dstate:]

    expand = e_ref[...]
    nh = expand.shape[0]
    dt_h = _softplus(dt_ref[0] + dtb_ref[...])
    a_h = dt_h * (-jnp.exp(alog_ref[...]))
    row = lax.broadcasted_iota(jnp.int32, (q, q), 0)
    col = lax.broadcasted_iota(jnp.int32, (q, q), 1)
    causal = col <= row
    tri = jnp.where(causal, 1.0, 0.0).astype(BF16)
    acum_h = _dot_exact_lhs01(tri, a_h)
    acum_ht = acum_h.T
    dt_x = _dot_exact_rhs01(dt_h[:, :nh], expand)
    acum_x = _dot_exact_rhs01(acum_h[:, :nh], expand)
    xdt = xs * dt_x
    acum_last = acum_x[q - 1:q, :]
    xd = xdt * jnp.exp(acum_last - acum_x)
    chunk_decay = jnp.exp(acum_last)
    exp_acum = jnp.exp(acum_x)

    lane = lax.broadcasted_iota(jnp.int32, (q, LANES), 1)
    gw = width // groups
    heads_per_group = gw // hdim
    pair = LANES // hdim
    y_parts = []
    for g in range(groups):
        b_g = bm[:, g * dstate:(g + 1) * dstate].astype(BF16)
        c_g = cm[:, g * dstate:(g + 1) * dstate].astype(BF16)
        cbm = _dot_nt(c_g, b_g)
        h_g = ht[:, g * gw:(g + 1) * gw]
        y_off = _dot(c_g, h_g.astype(BF16)) * exp_acum[:, g * gw:(g + 1) * gw]
        diag_parts = []
        for jp in range(heads_per_group // pair):
            l0 = g * gw + jp * LANES
            x_pair = xdt[:, l0:l0 + LANES].astype(BF16)
            out = jnp.zeros((q, LANES), F32)
            for half in range(pair):
                hd = g * heads_per_group + jp * pair + half
                decay = jnp.exp(jnp.where(causal, acum_h[:, hd:hd + 1] - acum_ht[hd:hd + 1, :], -jnp.inf))
                res = _dot((cbm * decay).astype(BF16), x_pair)
                in_head = (lane >= half * hdim) & (lane < (half + 1) * hdim)
                out = jnp.where(in_head, res, out)
            diag_parts.append(out)
        y_parts.append(jnp.concatenate(diag_parts, axis=1) + y_off)
        s_t = _dot_tn(b_g, xd[:, g * gw:(g + 1) * gw].astype(BF16))
        ht[:, g * gw:(g + 1) * gw] = h_g * chunk_decay[:, g * gw:(g + 1) * gw] + s_t
    y = jnp.concatenate(y_parts, axis=1) + xs * dsk_ref[...]
    y_ref[0] = _gated_group_norm(y, z_ref[0], nw_ref[...], groups)

    @pl.when(j == pl.num_programs(1) - 1)
    def _():
        hfin_ref[0] = ht[...].T


def _ssd_prompt(proj, blocks, params, dims):
    nb, l, _ = proj.shape
    q = SSD_CHUNK
    width, dstate, hdim, groups, xbcw = dims
    xbc_block, z_block, dt_block = blocks
    cw, cb, dtb, alog, dsk, nw, expand = params
    kern = functools.partial(_ssd_prompt_kernel, q=q, width=width, dstate=dstate, hdim=hdim, groups=groups)
    full = lambda a: pl.BlockSpec(a.shape, lambda b, j: (0,) * a.ndim)
    return pl.pallas_call(
        kern,
        grid=(nb, l // q),
        in_specs=[
            pl.BlockSpec((1, q, xbcw), lambda b, j: (b, j, xbc_block)),
            pl.BlockSpec((1, q, width), lambda b, j: (b, j, z_block)),
            pl.BlockSpec((1, q, LANES), lambda b, j: (b, j, dt_block)),
            full(cw), full(cb), full(dtb), full(alog), full(dsk), full(nw), full(expand),
        ],
        out_specs=[
            pl.BlockSpec((1, q, width), lambda b, j: (b, j, 0)),
            pl.BlockSpec((1, width, dstate), lambda b, j: (b, 0, 0)),
        ],
        out_shape=[
            jax.ShapeDtypeStruct((nb, l, width), F32),
            jax.ShapeDtypeStruct((nb, width, dstate), F32),
        ],
        scratch_shapes=[pltpu.VMEM((q + SUBLANES, xbcw), F32), pltpu.VMEM((dstate, width), F32)],
        compiler_params=_cparams("arbitrary", "arbitrary"),
        name="ssd_prompt",
    )(proj, proj, proj, cw, cb, dtb, alog, dsk, nw, expand)


def _ssd_sample_pre_kernel(xbc_ref, dt_ref, st_ref, cw_ref, cb_ref, dtb_ref, alog_ref, dsk_ref, e_ref,
                           ypart_ref, expa_ref, xd_ref, b_ref, c_ref, cd_ref,
                           *, nt, nbat, width, dstate, groups):
    kconv = cw_ref.shape[0]
    cw = cw_ref[...]
    expand = e_ref[...]
    nh = expand.shape[0]
    neg_a = -jnp.exp(alog_ref[...])
    full = [st_ref[t] for t in range(kconv - 1)]
    for t in range(nt):
        full.append(xbc_ref[pl.ds(t * nbat, nbat), :])
    xs, bm, cm, dt_x, acum_x, xdt = [], [], [], [], [], []
    acum_h = None
    for t in range(nt):
        conv = cb_ref[...] + cw[0:1] * full[t]
        for jj in range(1, kconv):
            conv = conv + cw[jj:jj + 1] * full[t + jj]
        act = _silu(conv)
        xs.append(act[:, :width])
        bm.append(act[:, width:width + groups * dstate])
        cm.append(act[:, width + groups * dstate:])
        dt_h = _softplus(dt_ref[pl.ds(t * nbat, nbat), :] + dtb_ref[...])
        a_h = dt_h * neg_a
        acum_h = a_h if acum_h is None else acum_h + a_h
        dt_x.append(_dot_exact_rhs01(dt_h[:, :nh], expand))
        acum_x.append(_dot_exact_rhs01(acum_h[:, :nh], expand))
        xdt.append(xs[t] * dt_x[t])
    cd_ref[...] = jnp.exp(acum_h)
    gw = width // groups
    for t in range(nt):
        y = xs[t] * dsk_ref[...]
        for s in range(t + 1):
            cb_parts = []
            for g in range(groups):
                prod = cm[t][:, g * dstate:(g + 1) * dstate] * bm[s][:, g * dstate:(g + 1) * dstate]
                cb_parts.append(jnp.broadcast_to(jnp.sum(prod, axis=-1, keepdims=True), (nbat, gw)))
            cb_x = jnp.concatenate(cb_parts, axis=1)
            y = y + cb_x * jnp.exp(acum_x[t] - acum_x[s]) * xdt[s]
        ypart_ref[t] = y
        expa_ref[t] = jnp.exp(acum_x[t])
    xd_ref[...] = jnp.zeros_like(xd_ref)
    b_ref[...] = jnp.zeros_like(b_ref)
    c_ref[...] = jnp.zeros_like(c_ref)
    for t in range(nt):
        xd_ref[:, t, :] = xdt[t] * jnp.exp(acum_x[nt - 1] - acum_x[t])
        b_ref[:, t, :] = bm[t]
        c_ref[:, t, :] = cm[t]


def _ssd_sample_pre(proj_s, blocks, state_tm, params, nt, nbat, dims, srows):
    width, dstate, hdim, groups, xbcw = dims
    xbc_block, dt_block = blocks
    cw, cb, dtb, alog, dsk, expand = params
    rows = nt * nbat
    kern = functools.partial(_ssd_sample_pre_kernel, nt=nt, nbat=nbat, width=width, dstate=dstate, groups=groups)
    sd = jax.ShapeDtypeStruct
    out_shapes = [(nt, nbat, width), (nt, nbat, width), (nbat, srows, width),
                  (nbat, srows, groups * dstate), (nbat, srows, groups * dstate), (nbat, LANES)]
    return pl.pallas_call(
        kern,
        grid=(1,),
        in_specs=[pl.BlockSpec((rows, xbcw), lambda i: (0, xbc_block)),
                  pl.BlockSpec((rows, LANES), lambda i: (0, dt_block)),
                  _whole(state_tm.shape)] + [_whole(a.shape) for a in params],
        out_specs=[_whole(s) for s in out_shapes],
        out_shape=[sd(s, F32) for s in out_shapes],
        compiler_params=_cparams("arbitrary"),
        name="ssd_sample_pre",
    )(proj_s, proj_s, state_tm, cw, cb, dtb, alog, dsk, expand)


def _ssd_sample_state_kernel(cd_ref, c_ref, b_ref, xd_ref, h0_ref, yoff_ref, hnew_ref,
                             *, heads, hdim, dstate, groups):
    gb = h0_ref.shape[0]
    hpg = heads // groups
    gw = hpg * hdim
    for i in range(gb):
        b = pl.program_id(0) * gb + i
        for g in range(groups):
            hm = h0_ref[i, g * hpg:(g + 1) * hpg].reshape(gw, dstate)
            c_g = c_ref[i, :, g * dstate:(g + 1) * dstate].astype(BF16)
            b_g = b_ref[i, :, g * dstate:(g + 1) * dstate].astype(BF16)
            yoff_ref[i, :, g * gw:(g + 1) * gw] = _dot_nt(c_g, hm.astype(BF16))
            upd = _dot_tn(xd_ref[i, :, g * gw:(g + 1) * gw].astype(BF16), b_g)
            for hh in range(hpg):
                hd = g * hpg + hh
                hnew_ref[i, hd] = h0_ref[i, hd] * cd_ref[b * heads + hd] + upd[hh * hdim:(hh + 1) * hdim, :]


def _ssd_sample_state(cd_flat, c_bm, b_bm, xd_bm, h0, dims):
    width, dstate, hdim, groups, _ = dims
    nbat, heads = h0.shape[0], h0.shape[1]
    rows = c_bm.shape[1]
    gb = _tile(nbat, 4)
    kern = functools.partial(_ssd_sample_state_kernel, heads=heads, hdim=hdim, dstate=dstate, groups=groups)
    return pl.pallas_call(
        kern,
        grid=(nbat // gb,),
        in_specs=[
            pl.BlockSpec(memory_space=pltpu.SMEM),
            pl.BlockSpec((gb, rows, groups * dstate), lambda b: (b, 0, 0)),
            pl.BlockSpec((gb, rows, groups * dstate), lambda b: (b, 0, 0)),
            pl.BlockSpec((gb, rows, width), lambda b: (b, 0, 0)),
            pl.BlockSpec((gb, heads, hdim, dstate), lambda b: (b, 0, 0, 0)),
        ],
        out_specs=[
            pl.BlockSpec((gb, rows, width), lambda b: (b, 0, 0)),
            pl.BlockSpec((gb, heads, hdim, dstate), lambda b: (b, 0, 0, 0)),
        ],
        out_shape=[
            jax.ShapeDtypeStruct((nbat, rows, width), F32),
            jax.ShapeDtypeStruct(h0.shape, F32),
        ],
        compiler_params=_cparams("arbitrary"),
        name="ssd_sample_state",
    )(cd_flat, c_bm, b_bm, xd_bm, h0)


def _ssd_sample_post_kernel(ypart_ref, yoff_ref, expa_ref, z_ref, nw_ref, y_ref, *, groups, nt, nbat):
    for t in range(nt):
        rows = pl.ds(t * nbat, nbat)
        y = ypart_ref[t] + yoff_ref[:, t, :] * expa_ref[t]
        y_ref[rows, :] = _gated_group_norm(y, z_ref[rows, :], nw_ref[...], groups)


def _ssd_sample_post(ypart, yoff_bm, expa, proj_s, z_block, nw, groups):
    nt, nbat, width = ypart.shape
    rows = nt * nbat
    kern = functools.partial(_ssd_sample_post_kernel, groups=groups, nt=nt, nbat=nbat)
    return pl.pallas_call(
        kern,
        grid=(1,),
        in_specs=[_whole(ypart.shape), _whole(yoff_bm.shape), _whole(expa.shape),
                  pl.BlockSpec((rows, width), lambda i: (0, z_block)), _whole(nw.shape)],
        out_specs=_whole((rows, width)),
        out_shape=jax.ShapeDtypeStruct((rows, width), F32),
        compiler_params=_cparams("arbitrary"),
        name="ssd_sample_post",
    )(ypart, yoff_bm, expa, proj_s, nw)


def _merge_kernel(x_ref, g_ref, yc_ref, ya_ref, ys_ref, g1_ref, wc_ref, wa_ref, ws_ref, wo_ref, o_ref, *, d):
    gates = g_ref[0]
    merged = (_sigmoid(gates[:, :d]) * _dot(yc_ref[0].astype(BF16), wc_ref[...])
              + _sigmoid(gates[:, d:2 * d]) * _dot(ya_ref[0].astype(BF16), wa_ref[...])
              + _sigmoid(gates[:, 2 * d:]) * _dot(ys_ref[0].astype(BF16), ws_ref[...]))
    o_ref[0] = x_ref[0] + g1_ref[0] * _dot(merged.astype(BF16), wo_ref[...])


def _merge(x, proj, gate_block, yc, ya, ys, mod, wc, wa, ws, wo):
    nb, l, d = x.shape
    tm = _tile(l, 512)
    kern = functools.partial(_merge_kernel, d=d)
    tok = lambda w: pl.BlockSpec((1, tm, w), lambda b, i: (b, i, 0))
    full = lambda a: pl.BlockSpec(a.shape, lambda b, i: (0, 0))
    return pl.pallas_call(
        kern,
        grid=(nb, l // tm),
        in_specs=[
            tok(d),
            pl.BlockSpec((1, tm, 3 * d), lambda b, i: (b, i, gate_block)),
            tok(yc.shape[2]), tok(ya.shape[2]), tok(ys.shape[2]),
            _mod_spec(mod, tm, 2, d),
            full(wc), full(wa), full(ws), full(wo),
        ],
        out_specs=tok(d),
        out_shape=jax.ShapeDtypeStruct((nb, l, d), F32),
        compiler_params=_cparams("arbitrary", "arbitrary"),
        name="merge",
    )(x, proj, yc, ya, ys, mod, wc, wa, ws, wo)


def _finish(x, gate, f, nf_ref, final_norm):
    out = x + gate * f
    if final_norm:
        out = out * lax.rsqrt(jnp.mean(out * out, axis=-1, keepdims=True) + EPS) * nf_ref[...]
    return out


def _ffn_kernel(x_ref, sh_ref, sc_ref, g2_ref, nw_ref, nf_ref, wg_ref, wu_ref, wd_ref, o_ref, h_scr, acc,
                *, final_norm):
    f = pl.program_id(2)

    @pl.when(f == 0)
    def _():
        h_scr[...] = _rms_mod(x_ref[0], nw_ref[...], sc_ref[0], sh_ref[0]).astype(BF16)
        acc[...] = jnp.zeros_like(acc)

    h = h_scr[...]
    a = _silu(_dot(h, wg_ref[...].astype(BF16))) * _dot(h, wu_ref[...].astype(BF16))
    acc[...] += _dot(a.astype(BF16), wd_ref[...].astype(BF16))

    @pl.when(f == pl.num_programs(2) - 1)
    def _():
        o_ref[0] = _finish(x_ref[0], g2_ref[0], acc[...], nf_ref, final_norm)


def _ffn(x, mod, norm_w, norm_final, wg, wu, wd, final_norm):
    nb, l, d = x.shape
    ff = wg.shape[1]
    tm = _tile(l, 1024)
    tf = _tile(ff, 512)
    kern = functools.partial(_ffn_kernel, final_norm=final_norm)
    vec = pl.BlockSpec((1, d), lambda b, i, f: (0, 0))
    return pl.pallas_call(
        kern,
        grid=(nb, l // tm, ff // tf),
        in_specs=[
            pl.BlockSpec((1, tm, d), lambda b, i, f: (b, i, 0)),
            _mod_spec(mod, tm, 3, d), _mod_spec(mod, tm, 4, d), _mod_spec(mod, tm, 5, d),
            vec, vec,
            pl.BlockSpec((d, tf), lambda b, i, f: (0, f)),
            pl.BlockSpec((d, tf), lambda b, i, f: (0, f)),
            pl.BlockSpec((tf, d), lambda b, i, f: (f, 0)),
        ],
        out_specs=pl.BlockSpec((1, tm, d), lambda b, i, f: (b, i, 0)),
        out_shape=jax.ShapeDtypeStruct((nb, l, d), F32),
        scratch_shapes=[pltpu.VMEM((tm, d), BF16), pltpu.VMEM((tm, d), F32)],
        compiler_params=_cparams("arbitrary", "arbitrary", "arbitrary"),
        name="ffn",
    )(x, mod, mod, mod, norm_w.reshape(1, d), norm_final.reshape(1, d), wg, wu, wd)


def _route_kernel(x_ref, sh_ref, sc_ref, nw_ref, wr_ref, w_ref, e_ref, *, n_experts):
    h = _rms_mod(x_ref[0], nw_ref[...], sc_ref[0], sh_ref[0])
    h_hi = h.astype(BF16)
    h_lo = (h - h_hi.astype(F32)).astype(BF16)
    wr = wr_ref[...]
    r_hi = wr.astype(BF16)
    r_lo = (wr - r_hi.astype(F32)).astype(BF16)
    logits = _dot(h_hi, r_hi) + _dot(h_lo, r_hi) + _dot(h_hi, r_lo)
    lane = lax.broadcasted_iota(jnp.int32, logits.shape, 1).astype(F32)
    neg = -jnp.inf
    lg = jnp.where(lane < n_experts, logits, neg)
    m1 = jnp.max(lg, axis=-1, keepdims=True)
    i1 = jnp.min(jnp.where(lg == m1, lane, float(LANES)), axis=-1, keepdims=True)
    rest = jnp.where(lane == i1, neg, lg)
    m2 = jnp.max(rest, axis=-1, keepdims=True)
    i2 = jnp.min(jnp.where(rest == m2, lane, float(LANES)), axis=-1, keepdims=True)
    e2 = jnp.exp(m2 - m1)
    w1 = 1.0 / (1.0 + e2)
    w2 = e2 / (1.0 + e2)
    w_ref[...] = jnp.where(lane == 0.0, w1, jnp.where(lane == 1.0, w2, 0.0))
    chosen = jnp.where(lane == 0.0, i1, jnp.where(lane == 1.0, i2, 0.0))
    e_ref[...] = chosen.T[:SUBLANES, :]


def _route(x, mod, norm_w, w_router_pad, n_experts):
    nb, l, d = x.shape
    tm = _tile(l, ROUTE_TILE)
    nt = l // tm
    kern = functools.partial(_route_kernel, n_experts=n_experts)
    return pl.pallas_call(
        kern,
        grid=(nb, nt),
        in_specs=[
            pl.BlockSpec((1, tm, d), lambda b, i: (b, i, 0)),
            _mod_spec(mod, tm, 3, d), _mod_spec(mod, tm, 4, d),
            pl.BlockSpec((1, d), lambda b, i: (0, 0)),
            pl.BlockSpec((d, LANES), lambda b, i: (0, 0)),
        ],
        out_specs=[
            pl.BlockSpec((tm, LANES), lambda b, i: (b * nt + i, 0)),
            pl.BlockSpec((SUBLANES, tm), lambda b, i: (0, b * nt + i)),
        ],
        out_shape=[
            jax.ShapeDtypeStruct((nb * l, LANES), F32),
            jax.ShapeDtypeStruct((SUBLANES, nb * l), F32),
        ],
        compiler_params=_cparams("arbitrary", "arbitrary"),
        name="moe_route",
    )(x, mod, mod, norm_w.reshape(1, d), w_router_pad)


def _plan_kernel(e_ref, pos_ref, te_ref, *, n_experts, tile, row_tile):
    steps = e_ref.shape[1] // tile
    sub = lax.broadcasted_iota(jnp.int32, (SUBLANES, tile), 0).astype(F32)
    sub_col = lax.broadcasted_iota(jnp.int32, (SUBLANES, 1), 0)

    def member(i):
        blk = e_ref[:, pl.ds(pl.multiple_of(i * tile, tile), tile)]
        e1, e2 = blk[0:1, :], blk[1:2, :]
        return e1, e2, jnp.where((sub == e1) | (sub == e2), 1.0, 0.0)

    def count_body(i, cnt):
        return cnt + jnp.sum(member(i)[2], axis=1, keepdims=True)

    cnt = lax.fori_loop(0, steps, count_body, jnp.zeros((SUBLANES, 1), F32))
    padded = jnp.floor((cnt + (row_tile - 1)) * (1.0 / row_tile)) * row_tile
    off = jnp.zeros((SUBLANES, 1), F32)
    run = jnp.zeros((1, 1), F32)
    for e in range(n_experts):
        off = jnp.where(sub_col == e, run, off)
        run = run + padded[e:e + 1, :]
    seg_end = off + padded

    r = lax.broadcasted_iota(jnp.int32, (tile, tile), 0)
    c = lax.broadcasted_iota(jnp.int32, (tile, tile), 1)
    before = jnp.where(r < c, 1.0, 0.0).astype(BF16)

    def pos_body(i, carry):
        e1, e2, m = member(i)
        val = off + carry + _dot(m.astype(BF16), before)
        p1 = jnp.sum(jnp.where(sub == e1, val, 0.0), axis=0, keepdims=True)
        p2 = jnp.sum(jnp.where(sub == e2, val, 0.0), axis=0, keepdims=True)
        rows = jnp.where(sub == 0.0, p1, jnp.where(sub == 1.0, p2, 0.0))
        pos_ref[:, pl.ds(pl.multiple_of(i * tile, tile), tile)] = rows.astype(jnp.int32)
        return carry + jnp.sum(m, axis=1, keepdims=True)

    lax.fori_loop(0, steps, pos_body, jnp.zeros((SUBLANES, 1), F32))

    sub_l = lax.broadcasted_iota(jnp.int32, (SUBLANES, LANES), 0)
    start = lax.broadcasted_iota(jnp.int32, (SUBLANES, LANES), 1).astype(F32) * row_tile
    owner = jnp.sum(jnp.where((seg_end <= start) & (sub_l < n_experts), 1.0, 0.0), axis=0, keepdims=True)
    owner = jnp.minimum(owner, n_experts - 1.0)
    used = run * (1.0 / row_tile)
    te_ref[...] = jnp.where(sub_l == 0, owner, jnp.where(sub_l == 1, used, 0.0)).astype(jnp.int32)


def _plan(e_all, n_experts, row_tile):
    t = e_all.shape[1]
    assert t % LANES == 0 and n_experts <= SUBLANES
    tile = _tile(t, ROUTE_TILE)
    kern = functools.partial(_plan_kernel, n_experts=n_experts, tile=tile, row_tile=row_tile)
    return pl.pallas_call(
        kern,
        out_shape=[
            jax.ShapeDtypeStruct((SUBLANES, t), jnp.int32),
            jax.ShapeDtypeStruct((SUBLANES, LANES), jnp.int32),
        ],
        compiler_params=pltpu.CompilerParams(vmem_limit_bytes=VMEM_LIMIT_BYTES),
        name="moe_plan",
    )(e_all)


def _scatter_kernel(p1_ref, p2_ref, x_ref, sh_ref, sc_ref, nw_ref, xs_in_ref, xs_ref, h_scr, sem, *, tm):
    del xs_in_ref
    h_scr[...] = _rms_mod(x_ref[0], nw_ref[...], sc_ref[0], sh_ref[0])

    def row_copy(t, p):
        return pltpu.make_async_copy(h_scr.at[pl.ds(t, 1), :], xs_ref.at[pl.ds(p, 1), :], sem)

    def issue(i, carry):
        for k in range(DMA_UNROLL):
            t = i * DMA_UNROLL + k
            row_copy(t, p1_ref[t]).start(priority=0)
            row_copy(t, p2_ref[t]).start(priority=1)
        return carry

    def drain(t, carry):
        row_copy(0, 0).wait()
        row_copy(0, 0).wait()
        return carry

    lax.fori_loop(0, tm // DMA_UNROLL, issue, 0)
    lax.fori_loop(0, tm, drain, 0, unroll=DMA_UNROLL)


def _scatter(x, mod, norm_w, p1, p2, xs_sorted):
    nb, l, d = x.shape
    tm = _tile(l, ROUTE_TILE)
    nt = l // tm
    kern = functools.partial(_scatter_kernel, tm=tm)
    idx = pl.BlockSpec((tm,), lambda b, i: (b * nt + i,), memory_space=pltpu.SMEM)
    return pl.pallas_call(
        kern,
        grid=(nb, nt),
        in_specs=[
            idx, idx,
            pl.BlockSpec((1, tm, d), lambda b, i: (b, i, 0)),
            _mod_spec(mod, tm, 3, d), _mod_spec(mod, tm, 4, d),
            pl.BlockSpec((1, d), lambda b, i: (0, 0)),
            pl.BlockSpec(memory_space=pl.ANY),
        ],
        out_specs=pl.BlockSpec(memory_space=pl.ANY),
        out_shape=jax.ShapeDtypeStruct(xs_sorted.shape, F32),
        scratch_shapes=[pltpu.VMEM((tm, d), F32), pltpu.SemaphoreType.DMA],
        input_output_aliases={6: 0},
        compiler_params=_cparams("arbitrary", "arbitrary"),
        name="moe_scatter",
    )(p1, p2, x, mod, mod, norm_w.reshape(1, d), xs_sorted)


def _group_ffn_kernel(te_ref, xs_ref, wg_ref, wu_ref, wd_ref, y_ref, h_scr, acc):
    j = pl.program_id(0)
    f = pl.program_id(1)

    @pl.when(j < te_ref[LANES])
    def _():
        @pl.when(f == 0)
        def _():
            h_scr[...] = xs_ref[...].astype(BF16)
            acc[...] = jnp.zeros_like(acc)

        h = h_scr[...]
        a = _silu(_dot(h, wg_ref[0].astype(BF16))) * _dot(h, wu_ref[0].astype(BF16))
        acc[...] += _dot(a.astype(BF16), wd_ref[0].astype(BF16))

        @pl.when(f == pl.num_programs(1) - 1)
        def _():
            y_ref[...] = acc[...]

    @pl.when((j >= te_ref[LANES]) & (f == 0))
    def _():
        y_ref[...] = jnp.zeros_like(y_ref)


def _group_ffn(te_flat, xs_sorted, wg, wu, wd, row_tile):
    rows, d = xs_sorted.shape
    ff = wg.shape[2]
    tf = _tile(ff, 512)
    nf = ff // tf

    def tile_of(j, te):
        return jnp.minimum(j, te[LANES] - 1)

    def f_of(j, f, te):
        return jnp.where(j < te[LANES], f, nf - 1)

    grid_spec = pltpu.PrefetchScalarGridSpec(
        num_scalar_prefetch=1,
        grid=(rows // row_tile, nf),
        in_specs=[
            pl.BlockSpec((row_tile, d), lambda j, f, te: (tile_of(j, te), 0)),
            pl.BlockSpec((1, d, tf), lambda j, f, te: (te[tile_of(j, te)], 0, f_of(j, f, te))),
            pl.BlockSpec((1, d, tf), lambda j, f, te: (te[tile_of(j, te)], 0, f_of(j, f, te))),
            pl.BlockSpec((1, tf, d), lambda j, f, te: (te[tile_of(j, te)], f_of(j, f, te), 0)),
        ],
        out_specs=pl.BlockSpec((row_tile, d), lambda j, f, te: (j, 0)),
        scratch_shapes=[pltpu.VMEM((row_tile, d), BF16), pltpu.VMEM((row_tile, d), F32)],
    )
    return pl.pallas_call(
        _group_ffn_kernel,
        grid_spec=grid_spec,
        out_shape=jax.ShapeDtypeStruct((rows, d), F32),
        compiler_params=_cparams("arbitrary", "arbitrary"),
        name="moe_group_ffn",
    )(te_flat, xs_sorted, wg, wu, wd)


def _combine_kernel(p1_ref, p2_ref, x_ref, g2_ref, w_ref, nf_ref, y_hbm, o_ref, buf, sem, *, tm, final_norm):
    def row_copy(k, t, p):
        return pltpu.make_async_copy(y_hbm.at[pl.ds(p, 1), :], buf.at[k, pl.ds(t, 1), :], sem)

    def issue(i, carry):
        for k in range(DMA_UNROLL):
            t = i * DMA_UNROLL + k
            row_copy(0, t, p1_ref[t]).start(priority=0)
            row_copy(1, t, p2_ref[t]).start(priority=1)
        return carry

    def drain(t, carry):
        row_copy(0, 0, 0).wait()
        row_copy(0, 0, 0).wait()
        return carry

    lax.fori_loop(0, tm // DMA_UNROLL, issue, 0)
    lax.fori_loop(0, tm, drain, 0, unroll=DMA_UNROLL)
    w = w_ref[...]
    f = w[:, 0:1] * buf[0] + w[:, 1:2] * buf[1]
    o_ref[0] = _finish(x_ref[0], g2_ref[0], f, nf_ref, final_norm)


def _combine(x, mod, w_cols, p1, p2, y_sorted, norm_final, final_norm):
    nb, l, d = x.shape
    tm = _tile(l, ROUTE_TILE)
    nt = l // tm
    kern = functools.partial(_combine_kernel, tm=tm, final_norm=final_norm)
    idx = pl.BlockSpec((tm,), lambda b, i: (b * nt + i,), memory_space=pltpu.SMEM)
    return pl.pallas_call(
        kern,
        grid=(nb, nt),
        in_specs=[
            idx, idx,
            pl.BlockSpec((1, tm, d), lambda b, i: (b, i, 0)),
            _mod_spec(mod, tm, 5, d),
            pl.BlockSpec((tm, LANES), lambda b, i: (b * nt + i, 0)),
            pl.BlockSpec((1, d), lambda b, i: (0, 0)),
            pl.BlockSpec(memory_space=pl.ANY),
        ],
        out_specs=pl.BlockSpec((1, tm, d), lambda b, i: (b, i, 0)),
        out_shape=jax.ShapeDtypeStruct((nb, l, d), F32),
        scratch_shapes=[pltpu.VMEM((2, tm, d), F32), pltpu.SemaphoreType.DMA],
        compiler_params=_cparams("arbitrary", "arbitrary"),
        name="moe_combine",
    )(p1, p2, x, mod, w_cols, norm_final.reshape(1, d), y_sorted)


def _moe(groups, norm_w, norm_final, w_router_pad, wg, wu, wd, final_norm):
    n_experts = wg.shape[0]
    d = groups[0][0].shape[2]
    routed = [_route(x, mod, norm_w, w_router_pad, n_experts) for x, mod in groups]
    e_all = jnp.concatenate([e for _, e in routed], axis=1)
    t = e_all.shape[1]
    pos, te = _plan(e_all, n_experts, MOE_ROW_TILE)
    te_flat = te.reshape(SUBLANES * LANES)
    n_tiles = -(-TOP_K * t // MOE_ROW_TILE) + n_experts
    assert n_tiles <= LANES
    xs_sorted = jnp.zeros((n_tiles * MOE_ROW_TILE, d), F32)
    spans, start = [], 0
    for x, _ in groups:
        n = x.shape[0] * x.shape[1]
        spans.append((start, start + n))
        start += n
    for (x, mod), (lo, hi) in zip(groups, spans):
        xs_sorted = _scatter(x, mod, norm_w, pos[0, lo:hi], pos[1, lo:hi], xs_sorted)
    y_sorted = _group_ffn(te_flat, xs_sorted, wg, wu, wd, MOE_ROW_TILE)
    return [_combine(x, mod, w_cols, pos[0, lo:hi], pos[1, lo:hi], y_sorted, norm_final, final_norm)
            for (x, mod), (w_cols, _), (lo, hi) in zip(groups, routed, spans)]


def kernel(x_prompt, x_sample, c_prompt, c_sample, cache_k, cache_v, state_conv, state_ssm_conv, state_ssm,
           w_mod, b_mod, norm_mix, norm_ffn, norm_final, w_in, w_sconv, sinks, ssm_conv_w, ssm_conv_b,
           dt_bias, a_log, d_skip, ssm_norm, w_br_conv, w_br_attn, w_br_ssm, w_o,
           ffn_w_gate, ffn_w_up, ffn_w_down, router, moe_w_gate, moe_w_up, moe_w_down):
    nbp, seq, d = x_prompt.shape
    nbat, nt, _ = x_sample.shape
    depth = w_mod.shape[0]
    cwid = w_sconv.shape[2]
    n_heads = sinks.shape[1]
    window, n_kv, head_dim = cache_k.shape[2:]
    heads, hdim, dstate = state_ssm.shape[2:]
    width = heads * hdim
    xbcw = ssm_conv_w.shape[2]
    groups = SSM_GROUPS
    n_experts = router.shape[2]
    aw = n_heads * head_dim
    kvw = n_kv * head_dim
    dims = (width, dstate, hdim, groups, xbcw)
    assert window == ATTN_BLOCK and seq % ATTN_BLOCK == 0 and xbcw == width + 2 * groups * dstate

    o_cv, o_q, o_k, o_z, o_xbc = 0, 3 * cwid, 3 * cwid + aw, 3 * cwid + aw + 2 * kvw, 3 * cwid + aw + 2 * kvw + width
    o_dt = o_xbc + xbcw
    o_g = o_dt + heads
    n_in = w_in.shape[2]
    dt_pad = 2 * LANES - heads

    def regroup(w):
        return jnp.concatenate(
            [w[:, o_g:n_in], w[:, o_cv:o_q], w[:, o_xbc:o_dt], w[:, o_z:o_xbc], w[:, o_q:o_k], w[:, o_k:o_z],
             w[:, o_dt:o_g], jnp.zeros((d, dt_pad), w.dtype)], axis=1).astype(BF16)

    p_gate, p_conv, p_xbc, p_z = 0, 3 * d, 3 * d + 3 * cwid, 3 * d + 3 * cwid + xbcw
    p_q = p_z + width
    p_kv = p_q + aw
    p_dt = p_kv + 2 * kvw
    blk = lambda off, w: off // w
    assert all(off % w == 0 for off, w in ((p_conv, 3 * cwid), (p_xbc, xbcw), (p_z, width), (p_q, aw),
                                           (p_kv, 2 * kvw), (p_dt, LANES)))

    n_c = nbp + nbat
    c_rows = -(-n_c // SUBLANES) * SUBLANES
    c_all = jnp.pad(jnp.concatenate([c_prompt, c_sample], axis=0), ((0, c_rows - n_c), (0, 0)))
    mod_all = _modulation(c_all, w_mod, b_mod)

    xs_tm = x_sample.transpose(1, 0, 2).reshape(1, nt * nbat, d)
    srows = 2 * SUBLANES

    pos_p = jnp.arange(seq, dtype=jnp.int32)
    pos_s = PAST_LEN + jnp.arange(SUBLANES, dtype=jnp.int32)
    tab_p = _rope_tables(pos_p, head_dim)
    tab_s = _rope_tables(pos_s, head_dim)

    expand = jnp.repeat(jnp.eye(heads, dtype=F32), hdim, axis=1).astype(BF16)
    pad_h = lambda v: jnp.pad(v, (0, LANES - heads)).reshape(1, LANES)

    xp, xs = x_prompt, xs_tm
    outs = {k: [] for k in ("kp", "vp", "cp", "scp", "sp", "ks", "vs", "cs", "scs", "ss")}
    for i in range(depth):
        w_in_i = regroup(w_in[i])
        wc, wa, ws, wo = (w[i].astype(BF16) for w in (w_br_conv, w_br_attn, w_br_ssm, w_o))
        mod_p = mod_all[i, :nbp].reshape(nbp, 1, 6 * d)
        mod_s = jnp.tile(mod_all[i, nbp:n_c], (nt, 1)).reshape(1, nt * nbat, 6 * d)
        ssm_params = (ssm_conv_w[i], ssm_conv_b[i].reshape(1, xbcw), pad_h(dt_bias[i]), pad_h(a_log[i]),
                      jnp.repeat(d_skip[i], hdim).reshape(1, width))
        nw_ssm = ssm_norm[i].reshape(1, width)

        proj = _inproj(xp, mod_p, norm_mix[i], w_in_i)
        y_conv, conv_tail = _conv_prompt(proj, w_sconv[i], blk(p_conv, 3 * cwid), cwid)
        y_attn, kv_last = _attn_prompt(proj, sinks[i], tab_p, blk(p_q, aw), blk(p_kv, 2 * kvw),
                                       n_heads, n_kv, head_dim)
        y_ssm, h_fin = _ssd_prompt(proj, (blk(p_xbc, xbcw), blk(p_z, width), blk(p_dt, LANES)),
                                   ssm_params + (nw_ssm, expand), dims)
        xp = _merge(xp, proj, blk(p_gate, 3 * d), y_conv, y_attn, y_ssm, mod_p, wc, wa, ws, wo)
        outs["kp"].append(kv_last[:, :, :kvw].reshape(nbp, window, n_kv, head_dim))
        outs["vp"].append(kv_last[:, :, kvw:].reshape(nbp, window, n_kv, head_dim))
        outs["cp"].append(conv_tail[:, SUBLANES - (w_sconv.shape[1] - 1):])
        outs["scp"].append(proj[:, seq - (ssm_conv_w.shape[1] - 1):, p_xbc:p_xbc + xbcw])
        outs["sp"].append(h_fin.reshape(nbp, heads, hdim, dstate))

        proj_s = _inproj(xs, mod_s, norm_mix[i], w_in_i)[0]
        proj_tm = proj_s.reshape(nt, nbat, proj_s.shape[1])
        y_conv_s, u_s = _conv_sample(proj_s, blk(p_conv, 3 * cwid), state_conv[i].transpose(1, 0, 2),
                                     w_sconv[i], nt, nbat, cwid)
        y_attn_tm, k_new = _attn_sample(proj_tm, cache_k[i].reshape(nbat, window, kvw),
                                        cache_v[i].reshape(nbat, window, kvw), sinks[i], tab_s,
                                        blk(p_q, aw), blk(p_kv, 2 * kvw), n_heads, n_kv, head_dim)
        ypart, expa, xd_bm, b_bm, c_bm, cd = _ssd_sample_pre(
            proj_s, (blk(p_xbc, xbcw), blk(p_dt, LANES)), state_ssm_conv[i].transpose(1, 0, 2),
            ssm_params + (expand,), nt, nbat, dims, srows)
        yoff_bm, h_new = _ssd_sample_state(cd[:, :heads].reshape(nbat * heads), c_bm, b_bm, xd_bm,
                                           state_ssm[i], dims)
        y_ssm_s = _ssd_sample_post(ypart, yoff_bm, expa, proj_s, blk(p_z, width), nw_ssm, groups)
        xs = _merge(xs, proj_s[None], blk(p_gate, 3 * d), y_conv_s[None], y_attn_tm.reshape(1, nt * nbat, aw),
                    y_ssm_s[None], mod_s, wc, wa, ws, wo)
        k_rows = k_new.transpose(1, 0, 2).reshape(nbat, nt, n_kv, head_dim)
        v_rows = proj_tm[:, :, p_kv + kvw:p_kv + 2 * kvw].transpose(1, 0, 2).reshape(nbat, nt, n_kv, head_dim)
        outs["ks"].append(jnp.concatenate([cache_k[i][:, nt:], k_rows], axis=1))
        outs["vs"].append(jnp.concatenate([cache_v[i][:, nt:], v_rows], axis=1))
        outs["cs"].append(u_s[nt - (w_sconv.shape[1] - 1):].transpose(1, 0, 2))
        kc = ssm_conv_w.shape[1] - 1
        outs["scs"].append(proj_tm[nt - kc:, :, p_xbc:p_xbc + xbcw].transpose(1, 0, 2))
        outs["ss"].append(h_new)

        last = i == depth - 1
        jj = i // 2
        if i % 2 == 0:
            wg, wu, wd = ffn_w_gate[jj], ffn_w_up[jj], ffn_w_down[jj]
            xp = _ffn(xp, mod_p, norm_ffn[i], norm_final, wg, wu, wd, last)
            xs = _ffn(xs, mod_s, norm_ffn[i], norm_final, wg, wu, wd, last)
        else:
            wg, wu, wd = moe_w_gate[jj], moe_w_up[jj], moe_w_down[jj]
            wr = jnp.pad(router[jj], ((0, 0), (0, LANES - n_experts)))
            xp, xs = _moe([(xp, mod_p), (xs, mod_s)], norm_ffn[i], norm_final, wr, wg, wu, wd, last)

    y_sample = xs.reshape(nt, nbat, d).transpose(1, 0, 2)
    st = lambda k: jnp.stack(outs[k])
    return (xp, y_sample, st("kp"), st("vp"), st("cp"), st("scp"), st("sp"),
            st("ks"), st("vs"), st("cs"), st("scs"), st("ss"))
```

```python
import functools
import math

import jax
import jax.numpy as jnp
from jax import lax
from jax.experimental import pallas as pl
from jax.experimental.pallas import tpu as pltpu

F32 = jnp.float32
BF16 = jnp.bfloat16

PAST_LEN = 8192
ROPE_THETA = 500000.0
EPS = 1e-6
TOP_K = 2
SSM_GROUPS = 2
ATTN_BLOCK = 128
ATTN_BLOCKS_PER_STEP = 1
SSD_CHUNK = 128
MOE_ROW_TILE = 1024
ROUTE_TILE = 512
DMA_UNROLL = 8

LANES = 128
SUBLANES = 8
VMEM_LIMIT_BYTES = 56 * 1024 * 1024


def _cparams(*semantics):
    return pltpu.CompilerParams(dimension_semantics=semantics, vmem_limit_bytes=VMEM_LIMIT_BYTES)


def _tile(n, pref):
    if n <= pref:
        return n
    t = pref
    while n % t:
        t //= 2
    return t


def _silu(x):
    return x / (1.0 + jnp.exp(-x))


def _sigmoid(x):
    return 1.0 / (1.0 + jnp.exp(-x))


def _softplus(x):
    return jnp.maximum(x, 0.0) + jnp.log1p(jnp.exp(-jnp.abs(x)))


def _dot(a, b):
    return jnp.dot(a, b, preferred_element_type=F32)


def _dot_nt(a, b):
    return lax.dot_general(a, b, (((1,), (1,)), ((), ())), preferred_element_type=F32)


def _dot_tn(a, b):
    return lax.dot_general(a, b, (((0,), (0,)), ((), ())), preferred_element_type=F32)


def _split3(x):
    hi = x.astype(BF16)
    r1 = x - hi.astype(F32)
    mid = r1.astype(BF16)
    lo = (r1 - mid.astype(F32)).astype(BF16)
    return hi, mid, lo


def _dot_exact_rhs01(x, m01):
    hi, mid, lo = _split3(x)
    return _dot(hi, m01) + _dot(mid, m01) + _dot(lo, m01)


def _dot_exact_lhs01(m01, x):
    hi, mid, lo = _split3(x)
    return _dot(m01, hi) + _dot(m01, mid) + _dot(m01, lo)


def _rms_mod(x, norm_w, scale, shift):
    xn = x * lax.rsqrt(jnp.mean(x * x, axis=-1, keepdims=True) + EPS)
    return (xn * norm_w) * (1.0 + scale) + shift


def _mod_kernel(c_ref, w_ref, b_ref, o_ref):
    a = _silu(c_ref[...]).astype(BF16)
    o_ref[0] = _dot(a, w_ref[0].astype(BF16)) + b_ref[0]


def _modulation(c_all, w_mod, b_mod):
    depth, d, n = w_mod.shape
    rows = c_all.shape[0]
    tn = _tile(n, 1024)
    return pl.pallas_call(
        _mod_kernel,
        grid=(depth, n // tn),
        in_specs=[
            pl.BlockSpec((rows, d), lambda i, j: (0, 0)),
            pl.BlockSpec((1, d, tn), lambda i, j: (i, 0, j)),
            pl.BlockSpec((1, 1, tn), lambda i, j: (i, 0, j)),
        ],
        out_specs=pl.BlockSpec((1, rows, tn), lambda i, j: (i, 0, j)),
        out_shape=jax.ShapeDtypeStruct((depth, rows, n), F32),
        compiler_params=_cparams("arbitrary", "arbitrary"),
        name="modulation",
    )(c_all, w_mod, b_mod.reshape(depth, 1, n))


def _mod_spec(mod, tm, chunk, d):
    if mod.shape[1] == 1:
        return pl.BlockSpec((1, 1, d), lambda b, i, *_: (b, 0, chunk))
    return pl.BlockSpec((1, tm, d), lambda b, i, *_: (b, i, chunk))


def _inproj_kernel(x_ref, sh_ref, sc_ref, nw_ref, w_ref, o_ref, h_scr):
    @pl.when(pl.program_id(2) == 0)
    def _():
        h_scr[...] = _rms_mod(x_ref[0], nw_ref[...], sc_ref[0], sh_ref[0]).astype(BF16)

    o_ref[0] = _dot(h_scr[...], w_ref[...])


def _inproj(x, mod, norm_w, w):
    nb, l, d = x.shape
    n = w.shape[1]
    tm = _tile(l, 1024)
    tn = _tile(n, 1024)
    return pl.pallas_call(
        _inproj_kernel,
        grid=(nb, l // tm, n // tn),
        in_specs=[
            pl.BlockSpec((1, tm, d), lambda b, i, j: (b, i, 0)),
            _mod_spec(mod, tm, 0, d),
            _mod_spec(mod, tm, 1, d),
            pl.BlockSpec((1, d), lambda b, i, j: (0, 0)),
            pl.BlockSpec((d, tn), lambda b, i, j: (0, j)),
        ],
        out_specs=pl.BlockSpec((1, tm, tn), lambda b, i, j: (b, i, j)),
        out_shape=jax.ShapeDtypeStruct((nb, l, n), F32),
        scratch_shapes=[pltpu.VMEM((tm, d), BF16)],
        compiler_params=_cparams("arbitrary", "arbitrary", "arbitrary"),
        name="inproj",
    )(x, mod, mod, norm_w.reshape(1, d), w)


def _conv_sample_kernel(p_ref, st_ref, w_ref, y_ref, u_ref, *, nt, nbat, cw, k):
    w = w_ref[...]
    full = [st_ref[j] for j in range(k - 1)]
    gates = []
    for t in range(nt):
        p = p_ref[pl.ds(t * nbat, nbat), :]
        gates.append(p[:, :cw])
        u = p[:, cw:2 * cw] * p[:, 2 * cw:]
        u_ref[t] = u
        full.append(u)
    for t in range(nt):
        acc = w[0:1] * full[t]
        for j in range(1, k):
            acc = acc + w[j:j + 1] * full[t + j]
        y_ref[pl.ds(t * nbat, nbat), :] = gates[t] * acc


def _whole(shape):
    return pl.BlockSpec(shape, lambda i: (0,) * len(shape))


def _conv_sample(proj_s, col_block, state_tm, w, nt, nbat, cw):
    k = w.shape[0]
    rows = nt * nbat
    kern = functools.partial(_conv_sample_kernel, nt=nt, nbat=nbat, cw=cw, k=k)
    return pl.pallas_call(
        kern,
        grid=(1,),
        in_specs=[pl.BlockSpec((rows, 3 * cw), lambda i: (0, col_block)), _whole(state_tm.shape), _whole(w.shape)],
        out_specs=[_whole((rows, cw)), _whole((nt, nbat, cw))],
        out_shape=[
            jax.ShapeDtypeStruct((rows, cw), F32),
            jax.ShapeDtypeStruct((nt, nbat, cw), F32),
        ],
        compiler_params=_cparams("arbitrary"),
        name="conv_sample",
    )(proj_s, state_tm, w)


def _rope(x, cos, sin_lo, sin_hi, half_rot):
    return (x * cos + pltpu.roll(x, LANES - half_rot, 1) * sin_lo
            + pltpu.roll(x, half_rot, 1) * sin_hi)


def _attn_core(q, kcat, vcat, sinks_ref, valid, cos, sin_lo, sin_hi, *, n_heads, group, head_dim):
    tq = q.shape[0]
    half_rot = head_dim // 8
    heads_per_slab = LANES // head_dim
    scale = head_dim ** -0.5
    lane = lax.broadcasted_iota(jnp.int32, (tq, LANES), 1)
    k_bf = [kcat.astype(BF16), pltpu.roll(kcat, head_dim, 1).astype(BF16)]
    v_bf = [vcat.astype(BF16), pltpu.roll(vcat, head_dim, 1).astype(BF16)]
    slabs = []
    for s in range(n_heads // heads_per_slab):
        qs = _rope(q[:, s * LANES:(s + 1) * LANES], cos, sin_lo, sin_hi, half_rot)
        out = jnp.zeros((tq, LANES), F32)
        for half in range(heads_per_slab):
            h = s * heads_per_slab + half
            g = h // group
            in_head = (lane >= half * head_dim) & (lane < (half + 1) * head_dim)
            qm = jnp.where(in_head, qs, 0.0).astype(BF16)
            swap = 0 if (g % heads_per_slab) == half else 1
            sc = _dot_nt(qm, k_bf[swap]) * scale
            sc = jnp.where(valid, sc, -1e30)
            sink = sinks_ref[h]
            m = jnp.maximum(jnp.max(sc, axis=-1, keepdims=True), sink)
            p = jnp.exp(sc - m)
            p = p / (jnp.sum(p, axis=-1, keepdims=True) + jnp.exp(sink - m))
            o = _dot(p.astype(BF16), v_bf[swap])
            out = jnp.where(in_head, o, out)
        slabs.append(out)
    return slabs


def _attn_prompt_kernel(sinks_ref, q_ref, kv_ref, cos_ref, slo_ref, shi_ref, y_ref, last_ref, kprev, vprev,
                        *, n_heads, group, head_dim):
    j = pl.program_id(1)
    tq = kprev.shape[0]

    @pl.when(j == 0)
    def _():
        kprev[...] = jnp.zeros_like(kprev)
        vprev[...] = jnp.zeros_like(vprev)

    r = lax.broadcasted_iota(jnp.int32, (tq, 2 * tq), 0)
    c = lax.broadcasted_iota(jnp.int32, (tq, 2 * tq), 1)
    band = (c >= r) & (c <= r + tq)
    k_prev, v_prev = kprev[...], vprev[...]
    for blk in range(q_ref.shape[1] // tq):
        rows = pl.ds(blk * tq, tq)
        cos, slo, shi = cos_ref[rows, :], slo_ref[rows, :], shi_ref[rows, :]
        kv = kv_ref[0, rows, :]
        k_rot = _rope(kv[:, :LANES], cos, slo, shi, head_dim // 8)
        v = kv[:, LANES:]
        kcat = jnp.concatenate([k_prev, k_rot], axis=0)
        vcat = jnp.concatenate([v_prev, v], axis=0)
        if blk == 0:
            valid = band & (c >= jnp.where(j > 0, 0, tq))
        else:
            valid = band
        slabs = _attn_core(q_ref[0, rows, :], kcat, vcat, sinks_ref, valid, cos, slo, shi,
                           n_heads=n_heads, group=group, head_dim=head_dim)
        for s, o in enumerate(slabs):
            y_ref[0, rows, s * LANES:(s + 1) * LANES] = o
        k_prev, v_prev = k_rot, v
    kprev[...] = k_prev
    vprev[...] = v_prev
    last_ref[0, :, :LANES] = k_prev
    last_ref[0, :, LANES:] = v_prev


def _attn_prompt(proj, sinks, tables, q_block, kv_block, n_heads, n_kv, head_dim):
    nb, l, _ = proj.shape
    tq = ATTN_BLOCK
    qw = n_heads * head_dim
    kvw = 2 * n_kv * head_dim
    assert n_kv * head_dim == LANES
    kern = functools.partial(_attn_prompt_kernel, n_heads=n_heads, group=n_heads // n_kv, head_dim=head_dim)
    ts = _tile(l, ATTN_BLOCKS_PER_STEP * tq)
    tab_spec = pl.BlockSpec((ts, LANES), lambda b, j: (j, 0))
    return pl.pallas_call(
        kern,
        grid=(nb, l // ts),
        in_specs=[
            pl.BlockSpec(memory_space=pltpu.SMEM),
            pl.BlockSpec((1, ts, qw), lambda b, j: (b, j, q_block)),
            pl.BlockSpec((1, ts, kvw), lambda b, j: (b, j, kv_block)),
            tab_spec, tab_spec, tab_spec,
        ],
        out_specs=[
            pl.BlockSpec((1, ts, qw), lambda b, j: (b, j, 0)),
            pl.BlockSpec((1, tq, kvw), lambda b, j: (b, 0, 0)),
        ],
        out_shape=[
            jax.ShapeDtypeStruct((nb, l, qw), F32),
            jax.ShapeDtypeStruct((nb, tq, kvw), F32),
        ],
        scratch_shapes=[pltpu.VMEM((tq, LANES), F32), pltpu.VMEM((tq, LANES), F32)],
        compiler_params=_cparams("arbitrary", "arbitrary"),
        name="attn_prompt",
    )(sinks, proj, proj, *tables)


def _attn_sample_kernel(sinks_ref, q_ref, kv_ref, ck_ref, cv_ref, cos_ref, slo_ref, shi_ref, y_ref, knew_ref,
                        qh, kc, vc, ob, *, n_heads, group, head_dim, nt):
    gb, window = ck_ref.shape[0], ck_ref.shape[1]
    half_rot = head_dim // 8
    heads_per_slab = LANES // head_dim
    nq = n_heads * SUBLANES
    nk = kc.shape[1]
    lane = lax.broadcasted_iota(jnp.int32, (gb, LANES), 1)

    @pl.when(pl.program_id(0) == 0)
    def _():
        qh[...] = jnp.zeros_like(qh)
        kc[...] = jnp.zeros_like(kc)
        vc[...] = jnp.zeros_like(vc)

    kc[:, 0:window, :] = ck_ref[...]
    vc[:, 0:window, :] = cv_ref[...]
    for t in range(nt):
        cos, slo, shi = cos_ref[t:t + 1, :], slo_ref[t:t + 1, :], shi_ref[t:t + 1, :]
        kv = kv_ref[t]
        k_rot = _rope(kv[:, :LANES], cos, slo, shi, half_rot)
        knew_ref[t] = k_rot
        kc[:, window + t, :] = k_rot
        vc[:, window + t, :] = kv[:, LANES:]
        for s in range(n_heads // heads_per_slab):
            qs = _rope(q_ref[t][:, s * LANES:(s + 1) * LANES], cos, slo, shi, half_rot)
            qs_swapped = pltpu.roll(qs, head_dim, 1)
            for half in range(heads_per_slab):
                h = s * heads_per_slab + half
                g = (h // group) % heads_per_slab
                in_kv_half = (lane >= g * head_dim) & (lane < (g + 1) * head_dim)
                qh[:, h * SUBLANES + t, :] = jnp.where(in_kv_half, qs if g == half else qs_swapped, 0.0)

    sc = jnp.einsum("bqd,bkd->bqk", qh[...].astype(BF16), kc[...].astype(BF16),
                    preferred_element_type=F32) * (head_dim ** -0.5)
    r = lax.broadcasted_iota(jnp.int32, (nq, nk), 0) % SUBLANES
    c = lax.broadcasted_iota(jnp.int32, (nq, nk), 1)
    valid = (c >= r) & (c <= r + window)
    sc = jnp.where(valid[None], sc, -1e30)
    row_head = lax.broadcasted_iota(jnp.int32, (nq, 1), 0) // SUBLANES
    sink = jnp.zeros((nq, 1), F32)
    for h in range(n_heads):
        sink = jnp.where(row_head == h, sinks_ref[h], sink)
    m = jnp.maximum(jnp.max(sc, axis=-1, keepdims=True), sink[None])
    p = jnp.exp(sc - m)
    p = p / (jnp.sum(p, axis=-1, keepdims=True) + jnp.exp(sink[None] - m))
    ob[...] = jnp.einsum("bqk,bkd->bqd", p.astype(BF16), vc[...].astype(BF16), preferred_element_type=F32)

    for t in range(nt):
        for s in range(n_heads // heads_per_slab):
            out = jnp.zeros((gb, LANES), F32)
            for half in range(heads_per_slab):
                h = s * heads_per_slab + half
                g = (h // group) % heads_per_slab
                o = ob[:, h * SUBLANES + t, :]
                if g != half:
                    o = pltpu.roll(o, head_dim, 1)
                out = jnp.where((lane >= half * head_dim) & (lane < (half + 1) * head_dim), o, out)
            y_ref[t, :, s * LANES:(s + 1) * LANES] = out


def _attn_sample(proj_tm, ck, cv, sinks, tables, q_block, kv_block, n_heads, n_kv, head_dim):
    nt, nbat, _ = proj_tm.shape
    window = ck.shape[1]
    qw = n_heads * head_dim
    assert n_kv * head_dim == LANES and nt <= SUBLANES
    gb = _tile(nbat, 16)
    nk = window + 2 * SUBLANES
    kern = functools.partial(_attn_sample_kernel, n_heads=n_heads, group=n_heads // n_kv, head_dim=head_dim, nt=nt)
    tab_spec = pl.BlockSpec((SUBLANES, LANES), lambda b: (0, 0))
    return pl.pallas_call(
        kern,
        grid=(nbat // gb,),
        in_specs=[
            pl.BlockSpec(memory_space=pltpu.SMEM),
            pl.BlockSpec((nt, gb, qw), lambda b: (0, b, q_block)),
            pl.BlockSpec((nt, gb, 2 * LANES), lambda b: (0, b, kv_block)),
            pl.BlockSpec((gb, window, LANES), lambda b: (b, 0, 0)),
            pl.BlockSpec((gb, window, LANES), lambda b: (b, 0, 0)),
            tab_spec, tab_spec, tab_spec,
        ],
        out_specs=[
            pl.BlockSpec((nt, gb, qw), lambda b: (0, b, 0)),
            pl.BlockSpec((nt, gb, LANES), lambda b: (0, b, 0)),
        ],
        out_shape=[
            jax.ShapeDtypeStruct((nt, nbat, qw), F32),
            jax.ShapeDtypeStruct((nt, nbat, LANES), F32),
        ],
        scratch_shapes=[
            pltpu.VMEM((gb, n_heads * SUBLANES, LANES), F32),
            pltpu.VMEM((gb, nk, LANES), F32), pltpu.VMEM((gb, nk, LANES), F32),
            pltpu.VMEM((gb, n_heads * SUBLANES, LANES), F32),
        ],
        compiler_params=_cparams("arbitrary"),
        name="attn_sample",
    )(sinks, proj_tm, proj_tm, ck, cv, *tables)


def _rope_tables(pos, head_dim):
    rot = head_dim // 4
    half = rot // 2
    inv = jnp.exp(-(2.0 * jnp.arange(half, dtype=F32) / rot) * math.log(ROPE_THETA))
    ang = pos.astype(F32)[:, None] * inv[None, :]
    cos, sin = jnp.cos(ang), jnp.sin(ang)
    n = pos.shape[0]
    pad = jnp.zeros((n, head_dim - rot), F32)
    zeros = jnp.zeros((n, half), F32)
    cos_h = jnp.concatenate([cos, cos, pad + 1.0], axis=1)
    lo_h = jnp.concatenate([-sin, zeros, pad], axis=1)
    hi_h = jnp.concatenate([zeros, sin, pad], axis=1)
    reps = LANES // head_dim
    return tuple(jnp.tile(t, (1, reps)) for t in (cos_h, lo_h, hi_h))


def _gated_group_norm(y, z, norm_w, groups):
    y = y * _silu(z)
    gw = y.shape[1] // groups
    parts = []
    for g in range(groups):
        yg = y[:, g * gw:(g + 1) * gw]
        parts.append(yg * lax.rsqrt(jnp.mean(yg * yg, axis=-1, keepdims=True) + EPS))
    return jnp.concatenate(parts, axis=1) * norm_w


def _ssd_prompt_kernel(xbc_ref, z_ref, dt_ref, cw_ref, cb_ref, dtb_ref, alog_ref, dsk_ref, nw_ref, e_ref,
                       y_ref, hfin_ref, ext, ht, *, q, width, dstate, hdim, groups):
    j = pl.program_id(1)
    kconv = cw_ref.shape[0]

    @pl.when(j == 0)
    def _():
        ext[0:SUBLANES, :] = jnp.zeros((SUBLANES, ext.shape[1]), F32)
        ht[...] = jnp.zeros_like(ht)

    xbc = xbc_ref[0]
    ext[SUBLANES:, :] = xbc
    cw = cw_ref[...]
    conv = cw[kconv - 1:kconv] * xbc + cb_ref[...]
    for t in range(kconv - 1):
        conv = conv + cw[t:t + 1] * ext[pl.ds(SUBLANES - (kconv - 1) + t, q), :]
    ext[0:SUBLANES, :] = xbc[q - SUBLANES:, :]
    act = _silu(conv)
    xs = act[:, :width]
    bm = act[:, width:width + groups * dstate]
    cm = act[:, width + groups * dstate:]

    expand = e_ref[...]
    nh = expand.shape[0]
    dt_h = _softplus(dt_ref[0] + dtb_ref[...])
    a_h = dt_h * (-jnp.exp(alog_ref[...]))
    row = lax.broadcasted_iota(jnp.int32, (q, q), 0)
    col = lax.broadcasted_iota(jnp.int32, (q, q), 1)
    causal = col <= row
    tri = jnp.where(causal, 1.0, 0.0).astype(BF16)
    acum_h = _dot_exact_lhs01(tri, a_h)
    acum_ht = acum_h.T
    dt_x = _dot_exact_rhs01(dt_h[:, :nh], expand)
    acum_x = _dot_exact_rhs01(acum_h[:, :nh], expand)
    xdt = xs * dt_x
    acum_last = acum_x[q - 1:q, :]
    xd = xdt * jnp.exp(acum_last - acum_x)
    chunk_decay = jnp.exp(acum_last)
    exp_acum = jnp.exp(acum_x)

    lane = lax.broadcasted_iota(jnp.int32, (q, LANES), 1)
    gw = width // groups
    heads_per_group = gw // hdim
    pair = LANES // hdim
    y_parts = []
    for g in range(groups):
        b_g = bm[:, g * dstate:(g + 1) * dstate].astype(BF16)
        c_g = cm[:, g * dstate:(g + 1) * dstate].astype(BF16)
        cbm = _dot_nt(c_g, b_g)
        h_g = ht[:, g * gw:(g + 1) * gw]
        y_off = _dot(c_g, h_g.astype(BF16)) * exp_acum[:, g * gw:(g + 1) * gw]
        diag_parts = []
        for jp in range(heads_per_group // pair):
            l0 = g * gw + jp * LANES
            x_pair = xdt[:, l0:l0 + LANES].astype(BF16)
            out = jnp.zeros((q, LANES), F32)
            for half in range(pair):
                hd = g * heads_per_group + jp * pair + half
                decay = jnp.exp(jnp.where(causal, acum_h[:, hd:hd + 1] - acum_ht[hd:hd + 1, :], -jnp.inf))
                res = _dot((cbm * decay).astype(BF16), x_pair)
                in_head = (lane >= half * hdim) & (lane < (half + 1) * hdim)
                out = jnp.where(in_head, res, out)
            diag_parts.append(out)
        y_parts.append(jnp.concatenate(diag_parts, axis=1) + y_off)
        s_t = _dot_tn(b_g, xd[:, g * gw:(g + 1) * gw].astype(BF16))
        ht[:, g * gw:(g + 1) * gw] = h_g * chunk_decay[:, g * gw:(g + 1) * gw] + s_t
    y = jnp.concatenate(y_parts, axis=1) + xs * dsk_ref[...]
    y_ref[0] = _gated_group_norm(y, z_ref[0], nw_ref[...], groups)

    @pl.when(j == pl.num_programs(1) - 1)
    def _():
        hfin_ref[0] = ht[...].T


def _ssd_prompt(proj, blocks, params, dims):
    nb, l, _ = proj.shape
    q = SSD_CHUNK
    width, dstate, hdim, groups, xbcw = dims
    xbc_block, z_block, dt_block = blocks
    cw, cb, dtb, alog, dsk, nw, expand = params
    kern = functools.partial(_ssd_prompt_kernel, q=q, width=width, dstate=dstate, hdim=hdim, groups=groups)
    full = lambda a: pl.BlockSpec(a.shape, lambda b, j: (0,) * a.ndim)
    return pl.pallas_call(
        kern,
        grid=(nb, l // q),
        in_specs=[
            pl.BlockSpec((1, q, xbcw), lambda b, j: (b, j, xbc_block)),
            pl.BlockSpec((1, q, width), lambda b, j: (b, j, z_block)),
            pl.BlockSpec((1, q, LANES), lambda b, j: (b, j, dt_block)),
            full(cw), full(cb), full(dtb), full(alog), full(dsk), full(nw), full(expand),
        ],
        out_specs=[
            pl.BlockSpec((1, q, width), lambda b, j: (b, j, 0)),
            pl.BlockSpec((1, width, dstate), lambda b, j: (b, 0, 0)),
        ],
        out_shape=[
            jax.ShapeDtypeStruct((nb, l, width), F32),
            jax.ShapeDtypeStruct((nb, width, dstate), F32),
        ],
        scratch_shapes=[pltpu.VMEM((q + SUBLANES, xbcw), F32), pltpu.VMEM((dstate, width), F32)],
        compiler_params=_cparams("arbitrary", "arbitrary"),
        name="ssd_prompt",
    )(proj, proj, proj, cw, cb, dtb, alog, dsk, nw, expand)


def _ssd_sample_pre_kernel(xbc_ref, dt_ref, st_ref, cw_ref, cb_ref, dtb_ref, alog_ref, dsk_ref, e_ref,
                           ypart_ref, expa_ref, xd_ref, b_ref, c_ref, cd_ref,
                           *, nt, nbat, width, dstate, groups):
    kconv = cw_ref.shape[0]
    cw = cw_ref[...]
    expand = e_ref[...]
    nh = expand.shape[0]
    neg_a = -jnp.exp(alog_ref[...])
    full = [st_ref[t] for t in range(kconv - 1)]
    for t in range(nt):
        full.append(xbc_ref[pl.ds(t * nbat, nbat), :])
    xs, bm, cm, dt_x, acum_x, xdt = [], [], [], [], [], []
    acum_h = None
    for t in range(nt):
        conv = cb_ref[...] + cw[0:1] * full[t]
        for jj in range(1, kconv):
            conv = conv + cw[jj:jj + 1] * full[t + jj]
        act = _silu(conv)
        xs.append(act[:, :width])
        bm.append(act[:, width:width + groups * dstate])
        cm.append(act[:, width + groups * dstate:])
        dt_h = _softplus(dt_ref[pl.ds(t * nbat, nbat), :] + dtb_ref[...])
        a_h = dt_h * neg_a
        acum_h = a_h if acum_h is None else acum_h + a_h
        dt_x.append(_dot_exact_rhs01(dt_h[:, :nh], expand))
        acum_x.append(_dot_exact_rhs01(acum_h[:, :nh], expand))
        xdt.append(xs[t] * dt_x[t])
    cd_ref[...] = jnp.exp(acum_h)
    gw = width // groups
    for t in range(nt):
        y = xs[t] * dsk_ref[...]
        for s in range(t + 1):
            cb_parts = []
            for g in range(groups):
                prod = cm[t][:, g * dstate:(g + 1) * dstate] * bm[s][:, g * dstate:(g + 1) * dstate]
                cb_parts.append(jnp.broadcast_to(jnp.sum(prod, axis=-1, keepdims=True), (nbat, gw)))
            cb_x = jnp.concatenate(cb_parts, axis=1)
            y = y + cb_x * jnp.exp(acum_x[t] - acum_x[s]) * xdt[s]
        ypart_ref[t] = y
        expa_ref[t] = jnp.exp(acum_x[t])
    xd_ref[...] = jnp.zeros_like(xd_ref)
    b_ref[...] = jnp.zeros_like(b_ref)
    c_ref[...] = jnp.zeros_like(c_ref)
    for t in range(nt):
        xd_ref[:, t, :] = xdt[t] * jnp.exp(acum_x[nt - 1] - acum_x[t])
        b_ref[:, t, :] = bm[t]
        c_ref[:, t, :] = cm[t]


def _ssd_sample_pre(proj_s, blocks, state_tm, params, nt, nbat, dims, srows):
    width, dstate, hdim, groups, xbcw = dims
    xbc_block, dt_block = blocks
    cw, cb, dtb, alog, dsk, expand = params
    rows = nt * nbat
    kern = functools.partial(_ssd_sample_pre_kernel, nt=nt, nbat=nbat, width=width, dstate=dstate, groups=groups)
    sd = jax.ShapeDtypeStruct
    out_shapes = [(nt, nbat, width), (nt, nbat, width), (nbat, srows, width),
                  (nbat, srows, groups * dstate), (nbat, srows, groups * dstate), (nbat, LANES)]
    return pl.pallas_call(
        kern,
        grid=(1,),
        in_specs=[pl.BlockSpec((rows, xbcw), lambda i: (0, xbc_block)),
                  pl.BlockSpec((rows, LANES), lambda i: (0, dt_block)),
                  _whole(state_tm.shape)] + [_whole(a.shape) for a in params],
        out_specs=[_whole(s) for s in out_shapes],
        out_shape=[sd(s, F32) for s in out_shapes],
        compiler_params=_cparams("arbitrary"),
        name="ssd_sample_pre",
    )(proj_s, proj_s, state_tm, cw, cb, dtb, alog, dsk, expand)


def _ssd_sample_state_kernel(cd_ref, c_ref, b_ref, xd_ref, h0_ref, *rest, heads, hdim, dstate, groups):
    yoff_ref, hnew_ref = rest[-2:]
    gb = h0_ref.shape[1]
    hpg = heads // groups
    gw = hpg * hdim
    for i in range(gb):
        b = pl.program_id(0) * gb + i
        for g in range(groups):
            hm = h0_ref[0, i, g * hpg:(g + 1) * hpg].reshape(gw, dstate)
            c_g = c_ref[i, :, g * dstate:(g + 1) * dstate].astype(BF16)
            b_g = b_ref[i, :, g * dstate:(g + 1) * dstate].astype(BF16)
            yoff_ref[i, :, g * gw:(g + 1) * gw] = _dot_nt(c_g, hm.astype(BF16))
            upd = _dot_tn(xd_ref[i, :, g * gw:(g + 1) * gw].astype(BF16), b_g)
            for hh in range(hpg):
                hd = g * hpg + hh
                hnew_ref[0, i, hd] = (h0_ref[0, i, hd] * cd_ref[b * heads + hd]
                                      + upd[hh * hdim:(hh + 1) * hdim, :])
    for later in range(1, hnew_ref.shape[0]):
        hnew_ref[later] = jnp.zeros(hnew_ref.shape[1:], F32)


def _ssd_sample_state(cd_flat, c_bm, b_bm, xd_bm, state_all, layer, stacked, dims):
    width, dstate, hdim, groups, _ = dims
    depth, nbat, heads = state_all.shape[:3]
    rows = c_bm.shape[1]
    gb = _tile(nbat, 4)
    kern = functools.partial(_ssd_sample_state_kernel, heads=heads, hdim=hdim, dstate=dstate, groups=groups)
    in_specs = [
        pl.BlockSpec(memory_space=pltpu.SMEM),
        pl.BlockSpec((gb, rows, groups * dstate), lambda b: (b, 0, 0)),
        pl.BlockSpec((gb, rows, groups * dstate), lambda b: (b, 0, 0)),
        pl.BlockSpec((gb, rows, width), lambda b: (b, 0, 0)),
        pl.BlockSpec((1, gb, heads, hdim, dstate), lambda b: (layer, b, 0, 0, 0)),
    ]
    args = [cd_flat, c_bm, b_bm, xd_bm, state_all]
    if layer == 0:
        assert stacked is None
        state_spec = pl.BlockSpec((depth, gb, heads, hdim, dstate), lambda b: (0, b, 0, 0, 0))
        aliases = {}
    else:
        in_specs.append(pl.BlockSpec(memory_space=pl.ANY))
        args.append(stacked)
        state_spec = pl.BlockSpec((1, gb, heads, hdim, dstate), lambda b: (layer, b, 0, 0, 0))
        aliases = {len(args) - 1: 1}
    return pl.pallas_call(
        kern,
        grid=(nbat // gb,),
        in_specs=in_specs,
        out_specs=[pl.BlockSpec((gb, rows, width), lambda b: (b, 0, 0)), state_spec],
        out_shape=[
            jax.ShapeDtypeStruct((nbat, rows, width), F32),
            jax.ShapeDtypeStruct(state_all.shape, F32),
        ],
        input_output_aliases=aliases,
        compiler_params=_cparams("arbitrary"),
        name="ssd_sample_state",
    )(*args)


def _ssd_sample_post_kernel(ypart_ref, yoff_ref, expa_ref, z_ref, nw_ref, y_ref, *, groups, nt, nbat):
    for t in range(nt):
        rows = pl.ds(t * nbat, nbat)
        y = ypart_ref[t] + yoff_ref[:, t, :] * expa_ref[t]
        y_ref[rows, :] = _gated_group_norm(y, z_ref[rows, :], nw_ref[...], groups)


def _ssd_sample_post(ypart, yoff_bm, expa, proj_s, z_block, nw, groups):
    nt, nbat, width = ypart.shape
    rows = nt * nbat
    kern = functools.partial(_ssd_sample_post_kernel, groups=groups, nt=nt, nbat=nbat)
    return pl.pallas_call(
        kern,
        grid=(1,),
        in_specs=[_whole(ypart.shape), _whole(yoff_bm.shape), _whole(expa.shape),
                  pl.BlockSpec((rows, width), lambda i: (0, z_block)), _whole(nw.shape)],
        out_specs=_whole((rows, width)),
        out_shape=jax.ShapeDtypeStruct((rows, width), F32),
        compiler_params=_cparams("arbitrary"),
        name="ssd_sample_post",
    )(ypart, yoff_bm, expa, proj_s, nw)


def _merge_math(x, gates, y_conv, y_attn, y_ssm, gate1, wc_ref, wa_ref, ws_ref, wo_ref, d):
    merged = (_sigmoid(gates[:, :d]) * _dot(y_conv.astype(BF16), wc_ref[...])
              + _sigmoid(gates[:, d:2 * d]) * _dot(y_attn.astype(BF16), wa_ref[...])
              + _sigmoid(gates[:, 2 * d:]) * _dot(y_ssm.astype(BF16), ws_ref[...]))
    return x + gate1 * _dot(merged.astype(BF16), wo_ref[...])


def _merge_kernel(x_ref, g_ref, yc_ref, ya_ref, ys_ref, g1_ref, wc_ref, wa_ref, ws_ref, wo_ref, o_ref, *, d):
    o_ref[0] = _merge_math(x_ref[0], g_ref[0], yc_ref[0], ya_ref[0], ys_ref[0], g1_ref[0],
                           wc_ref, wa_ref, ws_ref, wo_ref, d)


def _merge_conv_kernel(x_ref, g_ref, p_ref, cw_ref, ya_ref, ys_ref, g1_ref, wc_ref, wa_ref, ws_ref, wo_ref,
                       o_ref, st_ref, ext, *, d, tm, cw):
    @pl.when(pl.program_id(1) == 0)
    def _():
        ext[0:SUBLANES, :] = jnp.zeros((SUBLANES, cw), F32)

    p = p_ref[0]
    u = p[:, cw:2 * cw] * p[:, 2 * cw:]
    ext[SUBLANES:, :] = u
    w = cw_ref[...]
    conv = w[0:1] * ext[pl.ds(SUBLANES - 2, tm), :] + w[1:2] * ext[pl.ds(SUBLANES - 1, tm), :] + w[2:3] * u
    tail = u[tm - SUBLANES:, :]
    ext[0:SUBLANES, :] = tail
    st_ref[0] = tail
    o_ref[0] = _merge_math(x_ref[0], g_ref[0], p[:, :cw] * conv, ya_ref[0], ys_ref[0], g1_ref[0],
                           wc_ref, wa_ref, ws_ref, wo_ref, d)


def _merge_conv(x, proj, gate_block, conv_block, w_conv, ya, ys, mod, wc, wa, ws, wo):
    nb, l, d = x.shape
    cw = w_conv.shape[1]
    assert w_conv.shape[0] == 3
    tm = _tile(l, 512)
    kern = functools.partial(_merge_conv_kernel, d=d, tm=tm, cw=cw)
    tok = lambda w: pl.BlockSpec((1, tm, w), lambda b, i: (b, i, 0))
    full = lambda a: pl.BlockSpec(a.shape, lambda b, i: (0, 0))
    return pl.pallas_call(
        kern,
        grid=(nb, l // tm),
        in_specs=[
            tok(d),
            pl.BlockSpec((1, tm, 3 * d), lambda b, i: (b, i, gate_block)),
            pl.BlockSpec((1, tm, 3 * cw), lambda b, i: (b, i, conv_block)),
            full(w_conv),
            tok(ya.shape[2]), tok(ys.shape[2]),
            _mod_spec(mod, tm, 2, d),
            full(wc), full(wa), full(ws), full(wo),
        ],
        out_specs=[tok(d), pl.BlockSpec((1, SUBLANES, cw), lambda b, i: (b, 0, 0))],
        out_shape=[jax.ShapeDtypeStruct((nb, l, d), F32), jax.ShapeDtypeStruct((nb, SUBLANES, cw), F32)],
        scratch_shapes=[pltpu.VMEM((tm + SUBLANES, cw), F32)],
        compiler_params=_cparams("arbitrary", "arbitrary"),
        name="merge_conv",
    )(x, proj, proj, w_conv, ya, ys, mod, wc, wa, ws, wo)


def _merge(x, proj, gate_block, yc, ya, ys, mod, wc, wa, ws, wo):
    nb, l, d = x.shape
    tm = _tile(l, 512)
    kern = functools.partial(_merge_kernel, d=d)
    tok = lambda w: pl.BlockSpec((1, tm, w), lambda b, i: (b, i, 0))
    full = lambda a: pl.BlockSpec(a.shape, lambda b, i: (0, 0))
    return pl.pallas_call(
        kern,
        grid=(nb, l // tm),
        in_specs=[
            tok(d),
            pl.BlockSpec((1, tm, 3 * d), lambda b, i: (b, i, gate_block)),
            tok(yc.shape[2]), tok(ya.shape[2]), tok(ys.shape[2]),
            _mod_spec(mod, tm, 2, d),
            full(wc), full(wa), full(ws), full(wo),
        ],
        out_specs=tok(d),
        out_shape=jax.ShapeDtypeStruct((nb, l, d), F32),
        compiler_params=_cparams("arbitrary", "arbitrary"),
        name="merge",
    )(x, proj, yc, ya, ys, mod, wc, wa, ws, wo)


def _finish(x, gate, f, nf_ref, final_norm):
    out = x + gate * f
    if final_norm:
        out = out * lax.rsqrt(jnp.mean(out * out, axis=-1, keepdims=True) + EPS) * nf_ref[...]
    return out


def _ffn_kernel(x_ref, sh_ref, sc_ref, g2_ref, nw_ref, nf_ref, wg_ref, wu_ref, wd_ref, o_ref, h_scr, acc,
                *, final_norm):
    f = pl.program_id(2)

    @pl.when(f == 0)
    def _():
        h_scr[...] = _rms_mod(x_ref[0], nw_ref[...], sc_ref[0], sh_ref[0]).astype(BF16)
        acc[...] = jnp.zeros_like(acc)

    h = h_scr[...]
    a = _silu(_dot(h, wg_ref[...].astype(BF16))) * _dot(h, wu_ref[...].astype(BF16))
    acc[...] += _dot(a.astype(BF16), wd_ref[...].astype(BF16))

    @pl.when(f == pl.num_programs(2) - 1)
    def _():
        o_ref[0] = _finish(x_ref[0], g2_ref[0], acc[...], nf_ref, final_norm)


def _ffn(x, mod, norm_w, norm_final, wg, wu, wd, final_norm):
    nb, l, d = x.shape
    ff = wg.shape[1]
    tm = _tile(l, 1024)
    tf = _tile(ff, 512)
    kern = functools.partial(_ffn_kernel, final_norm=final_norm)
    vec = pl.BlockSpec((1, d), lambda b, i, f: (0, 0))
    return pl.pallas_call(
        kern,
        grid=(nb, l // tm, ff // tf),
        in_specs=[
            pl.BlockSpec((1, tm, d), lambda b, i, f: (b, i, 0)),
            _mod_spec(mod, tm, 3, d), _mod_spec(mod, tm, 4, d), _mod_spec(mod, tm, 5, d),
            vec, vec,
            pl.BlockSpec((d, tf), lambda b, i, f: (0, f)),
            pl.BlockSpec((d, tf), lambda b, i, f: (0, f)),
            pl.BlockSpec((tf, d), lambda b, i, f: (f, 0)),
        ],
        out_specs=pl.BlockSpec((1, tm, d), lambda b, i, f: (b, i, 0)),
        out_shape=jax.ShapeDtypeStruct((nb, l, d), F32),
        scratch_shapes=[pltpu.VMEM((tm, d), BF16), pltpu.VMEM((tm, d), F32)],
        compiler_params=_cparams("arbitrary", "arbitrary", "arbitrary"),
        name="ffn",
    )(x, mod, mod, mod, norm_w.reshape(1, d), norm_final.reshape(1, d), wg, wu, wd)


def _route_kernel(x_ref, sh_ref, sc_ref, nw_ref, wr_ref, w_ref, e_ref, *, n_experts):
    h = _rms_mod(x_ref[0], nw_ref[...], sc_ref[0], sh_ref[0])
    h_hi = h.astype(BF16)
    h_lo = (h - h_hi.astype(F32)).astype(BF16)
    wr = wr_ref[...]
    r_hi = wr.astype(BF16)
    r_lo = (wr - r_hi.astype(F32)).astype(BF16)
    logits = _dot(h_hi, r_hi) + _dot(h_lo, r_hi) + _dot(h_hi, r_lo)
    lane = lax.broadcasted_iota(jnp.int32, logits.shape, 1).astype(F32)
    neg = -jnp.inf
    lg = jnp.where(lane < n_experts, logits, neg)
    m1 = jnp.max(lg, axis=-1, keepdims=True)
    i1 = jnp.min(jnp.where(lg == m1, lane, float(LANES)), axis=-1, keepdims=True)
    rest = jnp.where(lane == i1, neg, lg)
    m2 = jnp.max(rest, axis=-1, keepdims=True)
    i2 = jnp.min(jnp.where(rest == m2, lane, float(LANES)), axis=-1, keepdims=True)
    e2 = jnp.exp(m2 - m1)
    w1 = 1.0 / (1.0 + e2)
    w2 = e2 / (1.0 + e2)
    w_ref[...] = jnp.where(lane == 0.0, w1, jnp.where(lane == 1.0, w2, 0.0))
    chosen = jnp.where(lane == 0.0, i1, jnp.where(lane == 1.0, i2, 0.0))
    e_ref[...] = chosen.T[:SUBLANES, :]


def _route(x, mod, norm_w, w_router_pad, n_experts):
    nb, l, d = x.shape
    tm = _tile(l, ROUTE_TILE)
    nt = l // tm
    kern = functools.partial(_route_kernel, n_experts=n_experts)
    return pl.pallas_call(
        kern,
        grid=(nb, nt),
        in_specs=[
            pl.BlockSpec((1, tm, d), lambda b, i: (b, i, 0)),
            _mod_spec(mod, tm, 3, d), _mod_spec(mod, tm, 4, d),
            pl.BlockSpec((1, d), lambda b, i: (0, 0)),
            pl.BlockSpec((d, LANES), lambda b, i: (0, 0)),
        ],
        out_specs=[
            pl.BlockSpec((tm, LANES), lambda b, i: (b * nt + i, 0)),
            pl.BlockSpec((SUBLANES, tm), lambda b, i: (0, b * nt + i)),
        ],
        out_shape=[
            jax.ShapeDtypeStruct((nb * l, LANES), F32),
            jax.ShapeDtypeStruct((SUBLANES, nb * l), F32),
        ],
        compiler_params=_cparams("arbitrary", "arbitrary"),
        name="moe_route",
    )(x, mod, mod, norm_w.reshape(1, d), w_router_pad)


def _plan_kernel(e_ref, pos_ref, te_ref, *, n_experts, tile, row_tile):
    steps = e_ref.shape[1] // tile
    sub = lax.broadcasted_iota(jnp.int32, (SUBLANES, tile), 0).astype(F32)
    sub_col = lax.broadcasted_iota(jnp.int32, (SUBLANES, 1), 0)

    def member(i):
        blk = e_ref[:, pl.ds(pl.multiple_of(i * tile, tile), tile)]
        e1, e2 = blk[0:1, :], blk[1:2, :]
        return e1, e2, jnp.where((sub == e1) | (sub == e2), 1.0, 0.0)

    def count_body(i, cnt):
        return cnt + jnp.sum(member(i)[2], axis=1, keepdims=True)

    cnt = lax.fori_loop(0, steps, count_body, jnp.zeros((SUBLANES, 1), F32))
    padded = jnp.floor((cnt + (row_tile - 1)) * (1.0 / row_tile)) * row_tile
    off = jnp.zeros((SUBLANES, 1), F32)
    run = jnp.zeros((1, 1), F32)
    for e in range(n_experts):
        off = jnp.where(sub_col == e, run, off)
        run = run + padded[e:e + 1, :]
    seg_end = off + padded

    r = lax.broadcasted_iota(jnp.int32, (tile, tile), 0)
    c = lax.broadcasted_iota(jnp.int32, (tile, tile), 1)
    before = jnp.where(r < c, 1.0, 0.0).astype(BF16)

    def pos_body(i, carry):
        e1, e2, m = member(i)
        val = off + carry + _dot(m.astype(BF16), before)
        p1 = jnp.sum(jnp.where(sub == e1, val, 0.0), axis=0, keepdims=True)
        p2 = jnp.sum(jnp.where(sub == e2, val, 0.0), axis=0, keepdims=True)
        rows = jnp.where(sub == 0.0, p1, jnp.where(sub == 1.0, p2, 0.0))
        pos_ref[:, pl.ds(pl.multiple_of(i * tile, tile), tile)] = rows.astype(jnp.int32)
        return carry + jnp.sum(m, axis=1, keepdims=True)

    lax.fori_loop(0, steps, pos_body, jnp.zeros((SUBLANES, 1), F32))

    sub_l = lax.broadcasted_iota(jnp.int32, (SUBLANES, LANES), 0)
    start = lax.broadcasted_iota(jnp.int32, (SUBLANES, LANES), 1).astype(F32) * row_tile
    owner = jnp.sum(jnp.where((seg_end <= start) & (sub_l < n_experts), 1.0, 0.0), axis=0, keepdims=True)
    owner = jnp.minimum(owner, n_experts - 1.0)
    used = run * (1.0 / row_tile)
    te_ref[...] = jnp.where(sub_l == 0, owner, jnp.where(sub_l == 1, used, 0.0)).astype(jnp.int32)


def _plan(e_all, n_experts, row_tile):
    t = e_all.shape[1]
    assert t % LANES == 0 and n_experts <= SUBLANES
    tile = _tile(t, ROUTE_TILE)
    kern = functools.partial(_plan_kernel, n_experts=n_experts, tile=tile, row_tile=row_tile)
    return pl.pallas_call(
        kern,
        out_shape=[
            jax.ShapeDtypeStruct((SUBLANES, t), jnp.int32),
            jax.ShapeDtypeStruct((SUBLANES, LANES), jnp.int32),
        ],
        compiler_params=pltpu.CompilerParams(vmem_limit_bytes=VMEM_LIMIT_BYTES),
        name="moe_plan",
    )(e_all)


def _scatter_kernel(p1_ref, p2_ref, x_ref, sh_ref, sc_ref, nw_ref, xs_in_ref, xs_ref, h_scr, sem, *, tm):
    del xs_in_ref
    h_scr[...] = _rms_mod(x_ref[0], nw_ref[...], sc_ref[0], sh_ref[0])

    def row_copy(t, p):
        return pltpu.make_async_copy(h_scr.at[pl.ds(t, 1), :], xs_ref.at[pl.ds(p, 1), :], sem)

    def issue(i, carry):
        for k in range(DMA_UNROLL):
            t = i * DMA_UNROLL + k
            row_copy(t, p1_ref[t]).start(priority=0)
            row_copy(t, p2_ref[t]).start(priority=1)
        return carry

    def drain(t, carry):
        row_copy(0, 0).wait()
        row_copy(0, 0).wait()
        return carry

    lax.fori_loop(0, tm // DMA_UNROLL, issue, 0)
    lax.fori_loop(0, tm, drain, 0, unroll=DMA_UNROLL)


def _scatter(x, mod, norm_w, p1, p2, xs_sorted):
    nb, l, d = x.shape
    tm = _tile(l, ROUTE_TILE)
    nt = l // tm
    kern = functools.partial(_scatter_kernel, tm=tm)
    idx = pl.BlockSpec((tm,), lambda b, i: (b * nt + i,), memory_space=pltpu.SMEM)
    return pl.pallas_call(
        kern,
        grid=(nb, nt),
        in_specs=[
            idx, idx,
            pl.BlockSpec((1, tm, d), lambda b, i: (b, i, 0)),
            _mod_spec(mod, tm, 3, d), _mod_spec(mod, tm, 4, d),
            pl.BlockSpec((1, d), lambda b, i: (0, 0)),
            pl.BlockSpec(memory_space=pl.ANY),
        ],
        out_specs=pl.BlockSpec(memory_space=pl.ANY),
        out_shape=jax.ShapeDtypeStruct(xs_sorted.shape, F32),
        scratch_shapes=[pltpu.VMEM((tm, d), F32), pltpu.SemaphoreType.DMA],
        input_output_aliases={6: 0},
        compiler_params=_cparams("arbitrary", "arbitrary"),
        name="moe_scatter",
    )(p1, p2, x, mod, mod, norm_w.reshape(1, d), xs_sorted)


def _group_ffn_kernel(te_ref, xs_ref, wg_ref, wu_ref, wd_ref, y_ref, h_scr, acc):
    j = pl.program_id(0)
    f = pl.program_id(1)

    @pl.when(j < te_ref[LANES])
    def _():
        @pl.when(f == 0)
        def _():
            h_scr[...] = xs_ref[...].astype(BF16)
            acc[...] = jnp.zeros_like(acc)

        h = h_scr[...]
        a = _silu(_dot(h, wg_ref[0].astype(BF16))) * _dot(h, wu_ref[0].astype(BF16))
        acc[...] += _dot(a.astype(BF16), wd_ref[0].astype(BF16))

        @pl.when(f == pl.num_programs(1) - 1)
        def _():
            y_ref[...] = acc[...]

    @pl.when((j >= te_ref[LANES]) & (f == 0))
    def _():
        y_ref[...] = jnp.zeros_like(y_ref)


def _group_ffn(te_flat, xs_sorted, wg, wu, wd, row_tile):
    rows, d = xs_sorted.shape
    ff = wg.shape[2]
    tf = _tile(ff, 512)
    nf = ff // tf

    def tile_of(j, te):
        return jnp.minimum(j, te[LANES] - 1)

    def f_of(j, f, te):
        return jnp.where(j < te[LANES], f, nf - 1)

    grid_spec = pltpu.PrefetchScalarGridSpec(
        num_scalar_prefetch=1,
        grid=(rows // row_tile, nf),
        in_specs=[
            pl.BlockSpec((row_tile, d), lambda j, f, te: (tile_of(j, te), 0)),
            pl.BlockSpec((1, d, tf), lambda j, f, te: (te[tile_of(j, te)], 0, f_of(j, f, te))),
            pl.BlockSpec((1, d, tf), lambda j, f, te: (te[tile_of(j, te)], 0, f_of(j, f, te))),
            pl.BlockSpec((1, tf, d), lambda j, f, te: (te[tile_of(j, te)], f_of(j, f, te), 0)),
        ],
        out_specs=pl.BlockSpec((row_tile, d), lambda j, f, te: (j, 0)),
        scratch_shapes=[pltpu.VMEM((row_tile, d), BF16), pltpu.VMEM((row_tile, d), F32)],
    )
    return pl.pallas_call(
        _group_ffn_kernel,
        grid_spec=grid_spec,
        out_shape=jax.ShapeDtypeStruct((rows, d), F32),
        compiler_params=_cparams("arbitrary", "arbitrary"),
        name="moe_group_ffn",
    )(te_flat, xs_sorted, wg, wu, wd)


def _combine_kernel(p1_ref, p2_ref, x_ref, g2_ref, w_ref, nf_ref, y_hbm, o_ref, buf, sem, *, tm, final_norm):
    def row_copy(k, t, p):
        return pltpu.make_async_copy(y_hbm.at[pl.ds(p, 1), :], buf.at[k, pl.ds(t, 1), :], sem)

    def issue(i, carry):
        for k in range(DMA_UNROLL):
            t = i * DMA_UNROLL + k
            row_copy(0, t, p1_ref[t]).start(priority=0)
            row_copy(1, t, p2_ref[t]).start(priority=1)
        return carry

    def drain(t, carry):
        row_copy(0, 0, 0).wait()
        row_copy(0, 0, 0).wait()
        return carry

    lax.fori_loop(0, tm // DMA_UNROLL, issue, 0)
    lax.fori_loop(0, tm, drain, 0, unroll=DMA_UNROLL)
    w = w_ref[...]
    f = w[:, 0:1] * buf[0] + w[:, 1:2] * buf[1]
    o_ref[0] = _finish(x_ref[0], g2_ref[0], f, nf_ref, final_norm)


def _combine(x, mod, w_cols, p1, p2, y_sorted, norm_final, final_norm):
    nb, l, d = x.shape
    tm = _tile(l, ROUTE_TILE)
    nt = l // tm
    kern = functools.partial(_combine_kernel, tm=tm, final_norm=final_norm)
    idx = pl.BlockSpec((tm,), lambda b, i: (b * nt + i,), memory_space=pltpu.SMEM)
    return pl.pallas_call(
        kern,
        grid=(nb, nt),
        in_specs=[
            idx, idx,
            pl.BlockSpec((1, tm, d), lambda b, i: (b, i, 0)),
            _mod_spec(mod, tm, 5, d),
            pl.BlockSpec((tm, LANES), lambda b, i: (b * nt + i, 0)),
            pl.BlockSpec((1, d), lambda b, i: (0, 0)),
            pl.BlockSpec(memory_space=pl.ANY),
        ],
        out_specs=pl.BlockSpec((1, tm, d), lambda b, i: (b, i, 0)),
        out_shape=jax.ShapeDtypeStruct((nb, l, d), F32),
        scratch_shapes=[pltpu.VMEM((2, tm, d), F32), pltpu.SemaphoreType.DMA],
        compiler_params=_cparams("arbitrary", "arbitrary"),
        name="moe_combine",
    )(p1, p2, x, mod, w_cols, norm_final.reshape(1, d), y_sorted)


def _moe(groups, norm_w, norm_final, w_router_pad, wg, wu, wd, final_norm):
    n_experts = wg.shape[0]
    d = groups[0][0].shape[2]
    routed = [_route(x, mod, norm_w, w_router_pad, n_experts) for x, mod in groups]
    e_all = jnp.concatenate([e for _, e in routed], axis=1)
    t = e_all.shape[1]
    pos, te = _plan(e_all, n_experts, MOE_ROW_TILE)
    te_flat = te.reshape(SUBLANES * LANES)
    n_tiles = -(-TOP_K * t // MOE_ROW_TILE) + n_experts
    assert n_tiles <= LANES
    xs_sorted = jnp.zeros((n_tiles * MOE_ROW_TILE, d), F32)
    spans, start = [], 0
    for x, _ in groups:
        n = x.shape[0] * x.shape[1]
        spans.append((start, start + n))
        start += n
    for (x, mod), (lo, hi) in zip(groups, spans):
        xs_sorted = _scatter(x, mod, norm_w, pos[0, lo:hi], pos[1, lo:hi], xs_sorted)
    y_sorted = _group_ffn(te_flat, xs_sorted, wg, wu, wd, MOE_ROW_TILE)
    return [_combine(x, mod, w_cols, pos[0, lo:hi], pos[1, lo:hi], y_sorted, norm_final, final_norm)
            for (x, mod), (w_cols, _), (lo, hi) in zip(groups, routed, spans)]


def kernel(x_prompt, x_sample, c_prompt, c_sample, cache_k, cache_v, state_conv, state_ssm_conv, state_ssm,
           w_mod, b_mod, norm_mix, norm_ffn, norm_final, w_in, w_sconv, sinks, ssm_conv_w, ssm_conv_b,
           dt_bias, a_log, d_skip, ssm_norm, w_br_conv, w_br_attn, w_br_ssm, w_o,
           ffn_w_gate, ffn_w_up, ffn_w_down, router, moe_w_gate, moe_w_up, moe_w_down):
    nbp, seq, d = x_prompt.shape
    nbat, nt, _ = x_sample.shape
    depth = w_mod.shape[0]
    cwid = w_sconv.shape[2]
    n_heads = sinks.shape[1]
    window, n_kv, head_dim = cache_k.shape[2:]
    heads, hdim, dstate = state_ssm.shape[2:]
    width = heads * hdim
    xbcw = ssm_conv_w.shape[2]
    groups = SSM_GROUPS
    n_experts = router.shape[2]
    aw = n_heads * head_dim
    kvw = n_kv * head_dim
    dims = (width, dstate, hdim, groups, xbcw)
    assert window == ATTN_BLOCK and seq % ATTN_BLOCK == 0 and xbcw == width + 2 * groups * dstate

    o_cv, o_q, o_k, o_z, o_xbc = 0, 3 * cwid, 3 * cwid + aw, 3 * cwid + aw + 2 * kvw, 3 * cwid + aw + 2 * kvw + width
    o_dt = o_xbc + xbcw
    o_g = o_dt + heads
    n_in = w_in.shape[2]
    dt_pad = 2 * LANES - heads

    def regroup(w):
        return jnp.concatenate(
            [w[:, o_g:n_in], w[:, o_cv:o_q], w[:, o_xbc:o_dt], w[:, o_z:o_xbc], w[:, o_q:o_k], w[:, o_k:o_z],
             w[:, o_dt:o_g], jnp.zeros((d, dt_pad), w.dtype)], axis=1).astype(BF16)

    p_gate, p_conv, p_xbc, p_z = 0, 3 * d, 3 * d + 3 * cwid, 3 * d + 3 * cwid + xbcw
    p_q = p_z + width
    p_kv = p_q + aw
    p_dt = p_kv + 2 * kvw
    blk = lambda off, w: off // w
    assert all(off % w == 0 for off, w in ((p_conv, 3 * cwid), (p_xbc, xbcw), (p_z, width), (p_q, aw),
                                           (p_kv, 2 * kvw), (p_dt, LANES)))

    n_c = nbp + nbat
    c_rows = -(-n_c // SUBLANES) * SUBLANES
    c_all = jnp.pad(jnp.concatenate([c_prompt, c_sample], axis=0), ((0, c_rows - n_c), (0, 0)))
    mod_all = _modulation(c_all, w_mod, b_mod)

    xs_tm = x_sample.transpose(1, 0, 2).reshape(1, nt * nbat, d)
    srows = 2 * SUBLANES

    pos_p = jnp.arange(seq, dtype=jnp.int32)
    pos_s = PAST_LEN + jnp.arange(SUBLANES, dtype=jnp.int32)
    tab_p = _rope_tables(pos_p, head_dim)
    tab_s = _rope_tables(pos_s, head_dim)

    expand = jnp.repeat(jnp.eye(heads, dtype=F32), hdim, axis=1).astype(BF16)
    pad_h = lambda v: jnp.pad(v, (0, LANES - heads)).reshape(1, LANES)

    xp, xs = x_prompt, xs_tm
    outs = {k: [] for k in ("kp", "vp", "cp", "scp", "sp", "ks", "vs", "cs", "scs")}
    new_state_s = None
    for i in range(depth):
        w_in_i = regroup(w_in[i])
        wc, wa, ws, wo = (w[i].astype(BF16) for w in (w_br_conv, w_br_attn, w_br_ssm, w_o))
        mod_p = mod_all[i, :nbp].reshape(nbp, 1, 6 * d)
        mod_s = jnp.tile(mod_all[i, nbp:n_c], (nt, 1)).reshape(1, nt * nbat, 6 * d)
        ssm_params = (ssm_conv_w[i], ssm_conv_b[i].reshape(1, xbcw), pad_h(dt_bias[i]), pad_h(a_log[i]),
                      jnp.repeat(d_skip[i], hdim).reshape(1, width))
        nw_ssm = ssm_norm[i].reshape(1, width)

        proj = _inproj(xp, mod_p, norm_mix[i], w_in_i)
        y_attn, kv_last = _attn_prompt(proj, sinks[i], tab_p, blk(p_q, aw), blk(p_kv, 2 * kvw),
                                       n_heads, n_kv, head_dim)
        y_ssm, h_fin = _ssd_prompt(proj, (blk(p_xbc, xbcw), blk(p_z, width), blk(p_dt, LANES)),
                                   ssm_params + (nw_ssm, expand), dims)
        xp, conv_tail = _merge_conv(xp, proj, blk(p_gate, 3 * d), blk(p_conv, 3 * cwid), w_sconv[i],
                                    y_attn, y_ssm, mod_p, wc, wa, ws, wo)
        outs["kp"].append(kv_last[:, :, :kvw].reshape(nbp, window, n_kv, head_dim))
        outs["vp"].append(kv_last[:, :, kvw:].reshape(nbp, window, n_kv, head_dim))
        outs["cp"].append(conv_tail[:, SUBLANES - (w_sconv.shape[1] - 1):])
        outs["scp"].append(proj[:, seq - (ssm_conv_w.shape[1] - 1):, p_xbc:p_xbc + xbcw])
        outs["sp"].append(h_fin.reshape(nbp, heads, hdim, dstate))

        proj_s = _inproj(xs, mod_s, norm_mix[i], w_in_i)[0]
        proj_tm = proj_s.reshape(nt, nbat, proj_s.shape[1])
        y_conv_s, u_s = _conv_sample(proj_s, blk(p_conv, 3 * cwid), state_conv[i].transpose(1, 0, 2),
                                     w_sconv[i], nt, nbat, cwid)
        y_attn_tm, k_new = _attn_sample(proj_tm, cache_k[i].reshape(nbat, window, kvw),
                                        cache_v[i].reshape(nbat, window, kvw), sinks[i], tab_s,
                                        blk(p_q, aw), blk(p_kv, 2 * kvw), n_heads, n_kv, head_dim)
        ypart, expa, xd_bm, b_bm, c_bm, cd = _ssd_sample_pre(
            proj_s, (blk(p_xbc, xbcw), blk(p_dt, LANES)), state_ssm_conv[i].transpose(1, 0, 2),
            ssm_params + (expand,), nt, nbat, dims, srows)
        yoff_bm, new_state_s = _ssd_sample_state(cd[:, :heads].reshape(nbat * heads), c_bm, b_bm, xd_bm,
                                                 state_ssm, i, new_state_s, dims)
        y_ssm_s = _ssd_sample_post(ypart, yoff_bm, expa, proj_s, blk(p_z, width), nw_ssm, groups)
        xs = _merge(xs, proj_s[None], blk(p_gate, 3 * d), y_conv_s[None], y_attn_tm.reshape(1, nt * nbat, aw),
                    y_ssm_s[None], mod_s, wc, wa, ws, wo)
        k_rows = k_new.transpose(1, 0, 2).reshape(nbat, nt, n_kv, head_dim)
        v_rows = proj_tm[:, :, p_kv + kvw:p_kv + 2 * kvw].transpose(1, 0, 2).reshape(nbat, nt, n_kv, head_dim)
        outs["ks"].append(jnp.concatenate([cache_k[i][:, nt:], k_rows], axis=1))
        outs["vs"].append(jnp.concatenate([cache_v[i][:, nt:], v_rows], axis=1))
        outs["cs"].append(u_s[nt - (w_sconv.shape[1] - 1):].transpose(1, 0, 2))
        kc = ssm_conv_w.shape[1] - 1
        outs["scs"].append(proj_tm[nt - kc:, :, p_xbc:p_xbc + xbcw].transpose(1, 0, 2))

        last = i == depth - 1
        jj = i // 2
        if i % 2 == 0:
            wg, wu, wd = ffn_w_gate[jj], ffn_w_up[jj], ffn_w_down[jj]
            xp = _ffn(xp, mod_p, norm_ffn[i], norm_final, wg, wu, wd, last)
            xs = _ffn(xs, mod_s, norm_ffn[i], norm_final, wg, wu, wd, last)
        else:
            wg, wu, wd = moe_w_gate[jj], moe_w_up[jj], moe_w_down[jj]
            wr = jnp.pad(router[jj], ((0, 0), (0, LANES - n_experts)))
            xp, xs = _moe([(xp, mod_p), (xs, mod_s)], norm_ffn[i], norm_final, wr, wg, wu, wd, last)

    y_sample = xs.reshape(nt, nbat, d).transpose(1, 0, 2)
    st = lambda k: jnp.stack(outs[k])
    return (xp, y_sample, st("kp"), st("vp"), st("cp"), st("scp"), st("sp"),
            st("ks"), st("vs"), st("cs"), st("scs"), new_state_s)
```

```python
import functools
import math

import jax
import jax.numpy as jnp
from jax import lax
from jax.experimental import pallas as pl
from jax.experimental.pallas import tpu as pltpu

F32 = jnp.float32
BF16 = jnp.bfloat16

PAST_LEN = 8192
ROPE_THETA = 500000.0
EPS = 1e-6
TOP_K = 2
SSM_GROUPS = 2
ATTN_BLOCK = 128
ATTN_BLOCKS_PER_STEP = 1
SSD_CHUNK = 128
MOE_ROW_TILE = 1024
ROUTE_TILE = 512
DMA_UNROLL = 8

LANES = 128
SUBLANES = 8
VMEM_LIMIT_BYTES = 56 * 1024 * 1024


def _cparams(*semantics):
    return pltpu.CompilerParams(dimension_semantics=semantics, vmem_limit_bytes=VMEM_LIMIT_BYTES)


def _tile(n, pref):
    if n <= pref:
        return n
    t = pref
    while n % t:
        t //= 2
    return t


def _silu(x):
    return x / (1.0 + jnp.exp(-x))


def _sigmoid(x):
    return 1.0 / (1.0 + jnp.exp(-x))


def _softplus(x):
    return jnp.maximum(x, 0.0) + jnp.log1p(jnp.exp(-jnp.abs(x)))


def _dot(a, b):
    return jnp.dot(a, b, preferred_element_type=F32)


def _dot_nt(a, b):
    return lax.dot_general(a, b, (((1,), (1,)), ((), ())), preferred_element_type=F32)


def _dot_tn(a, b):
    return lax.dot_general(a, b, (((0,), (0,)), ((), ())), preferred_element_type=F32)


def _split3(x):
    hi = x.astype(BF16)
    r1 = x - hi.astype(F32)
    mid = r1.astype(BF16)
    lo = (r1 - mid.astype(F32)).astype(BF16)
    return hi, mid, lo


def _dot_exact_rhs01(x, m01):
    hi, mid, lo = _split3(x)
    return _dot(hi, m01) + _dot(mid, m01) + _dot(lo, m01)


def _dot_exact_lhs01(m01, x):
    hi, mid, lo = _split3(x)
    return _dot(m01, hi) + _dot(m01, mid) + _dot(m01, lo)


def _rms_mod(x, norm_w, scale, shift):
    xn = x * lax.rsqrt(jnp.mean(x * x, axis=-1, keepdims=True) + EPS)
    return (xn * norm_w) * (1.0 + scale) + shift


def _mod_kernel(c_ref, w_ref, b_ref, o_ref):
    a = _silu(c_ref[...]).astype(BF16)
    o_ref[0] = _dot(a, w_ref[0].astype(BF16)) + b_ref[0]


def _modulation(c_all, w_mod, b_mod):
    depth, d, n = w_mod.shape
    rows = c_all.shape[0]
    tn = _tile(n, 1024)
    return pl.pallas_call(
        _mod_kernel,
        grid=(depth, n // tn),
        in_specs=[
            pl.BlockSpec((rows, d), lambda i, j: (0, 0)),
            pl.BlockSpec((1, d, tn), lambda i, j: (i, 0, j)),
            pl.BlockSpec((1, 1, tn), lambda i, j: (i, 0, j)),
        ],
        out_specs=pl.BlockSpec((1, rows, tn), lambda i, j: (i, 0, j)),
        out_shape=jax.ShapeDtypeStruct((depth, rows, n), F32),
        compiler_params=_cparams("arbitrary", "arbitrary"),
        name="modulation",
    )(c_all, w_mod, b_mod.reshape(depth, 1, n))


def _mod_spec(mod, tm, chunk, d):
    if mod.shape[1] == 1:
        return pl.BlockSpec((1, 1, d), lambda b, i, *_: (b, 0, chunk))
    return pl.BlockSpec((1, tm, d), lambda b, i, *_: (b, i, chunk))


def _inproj_kernel(x_ref, sh_ref, sc_ref, nw_ref, w_ref, o_ref, h_scr):
    @pl.when(pl.program_id(2) == 0)
    def _():
        h_scr[...] = _rms_mod(x_ref[0], nw_ref[...], sc_ref[0], sh_ref[0]).astype(BF16)

    o_ref[0] = _dot(h_scr[...], w_ref[...])


def _inproj(x, mod, norm_w, w):
    nb, l, d = x.shape
    n = w.shape[1]
    tm = _tile(l, 1024)
    tn = _tile(n, 2048)
    return pl.pallas_call(
        _inproj_kernel,
        grid=(nb, l // tm, n // tn),
        in_specs=[
            pl.BlockSpec((1, tm, d), lambda b, i, j: (b, i, 0)),
            _mod_spec(mod, tm, 0, d),
            _mod_spec(mod, tm, 1, d),
            pl.BlockSpec((1, d), lambda b, i, j: (0, 0)),
            pl.BlockSpec((d, tn), lambda b, i, j: (0, j)),
        ],
        out_specs=pl.BlockSpec((1, tm, tn), lambda b, i, j: (b, i, j)),
        out_shape=jax.ShapeDtypeStruct((nb, l, n), F32),
        scratch_shapes=[pltpu.VMEM((tm, d), BF16)],
        compiler_params=_cparams("arbitrary", "arbitrary", "arbitrary"),
        name="inproj",
    )(x, mod, mod, norm_w.reshape(1, d), w)


def _conv_sample_kernel(p_ref, st_ref, w_ref, y_ref, u_ref, *, nt, nbat, cw, k):
    w = w_ref[...]
    full = [st_ref[j] for j in range(k - 1)]
    gates = []
    for t in range(nt):
        p = p_ref[pl.ds(t * nbat, nbat), :]
        gates.append(p[:, :cw])
        u = p[:, cw:2 * cw] * p[:, 2 * cw:]
        u_ref[t] = u
        full.append(u)
    for t in range(nt):
        acc = w[0:1] * full[t]
        for j in range(1, k):
            acc = acc + w[j:j + 1] * full[t + j]
        y_ref[pl.ds(t * nbat, nbat), :] = gates[t] * acc


def _whole(shape):
    return pl.BlockSpec(shape, lambda i: (0,) * len(shape))


def _conv_sample(proj_s, col_block, state_tm, w, nt, nbat, cw):
    k = w.shape[0]
    rows = nt * nbat
    kern = functools.partial(_conv_sample_kernel, nt=nt, nbat=nbat, cw=cw, k=k)
    return pl.pallas_call(
        kern,
        grid=(1,),
        in_specs=[pl.BlockSpec((rows, 3 * cw), lambda i: (0, col_block)), _whole(state_tm.shape), _whole(w.shape)],
        out_specs=[_whole((rows, cw)), _whole((nt, nbat, cw))],
        out_shape=[
            jax.ShapeDtypeStruct((rows, cw), F32),
            jax.ShapeDtypeStruct((nt, nbat, cw), F32),
        ],
        compiler_params=_cparams("arbitrary"),
        name="conv_sample",
    )(proj_s, state_tm, w)


def _rope(x, cos, sin_lo, sin_hi, half_rot):
    return (x * cos + pltpu.roll(x, LANES - half_rot, 1) * sin_lo
            + pltpu.roll(x, half_rot, 1) * sin_hi)


def _attn_core(q, kcat, vcat, sinks_ref, valid, cos, sin_lo, sin_hi, *, n_heads, group, head_dim):
    tq = q.shape[0]
    half_rot = head_dim // 8
    heads_per_slab = LANES // head_dim
    scale = head_dim ** -0.5
    lane = lax.broadcasted_iota(jnp.int32, (tq, LANES), 1)
    k_bf = [kcat.astype(BF16), pltpu.roll(kcat, head_dim, 1).astype(BF16)]
    v_bf = [vcat.astype(BF16), pltpu.roll(vcat, head_dim, 1).astype(BF16)]
    slabs = []
    for s in range(n_heads // heads_per_slab):
        qs = _rope(q[:, s * LANES:(s + 1) * LANES], cos, sin_lo, sin_hi, half_rot)
        out = jnp.zeros((tq, LANES), F32)
        for half in range(heads_per_slab):
            h = s * heads_per_slab + half
            g = h // group
            in_head = (lane >= half * head_dim) & (lane < (half + 1) * head_dim)
            qm = jnp.where(in_head, qs, 0.0).astype(BF16)
            swap = 0 if (g % heads_per_slab) == half else 1
            sc = _dot_nt(qm, k_bf[swap]) * scale
            sc = jnp.where(valid, sc, -1e30)
            sink = sinks_ref[h]
            m = jnp.maximum(jnp.max(sc, axis=-1, keepdims=True), sink)
            p = jnp.exp(sc - m)
            p = p / (jnp.sum(p, axis=-1, keepdims=True) + jnp.exp(sink - m))
            o = _dot(p.astype(BF16), v_bf[swap])
            out = jnp.where(in_head, o, out)
        slabs.append(out)
    return slabs


def _attn_prompt_kernel(sinks_ref, q_ref, kv_ref, cos_ref, slo_ref, shi_ref, y_ref, last_ref, kprev, vprev,
                        *, n_heads, group, head_dim):
    j = pl.program_id(1)
    tq = kprev.shape[0]

    @pl.when(j == 0)
    def _():
        kprev[...] = jnp.zeros_like(kprev)
        vprev[...] = jnp.zeros_like(vprev)

    r = lax.broadcasted_iota(jnp.int32, (tq, 2 * tq), 0)
    c = lax.broadcasted_iota(jnp.int32, (tq, 2 * tq), 1)
    band = (c >= r) & (c <= r + tq)
    k_prev, v_prev = kprev[...], vprev[...]
    for blk in range(q_ref.shape[1] // tq):
        rows = pl.ds(blk * tq, tq)
        cos, slo, shi = cos_ref[rows, :], slo_ref[rows, :], shi_ref[rows, :]
        kv = kv_ref[0, rows, :]
        k_rot = _rope(kv[:, :LANES], cos, slo, shi, head_dim // 8)
        v = kv[:, LANES:]
        kcat = jnp.concatenate([k_prev, k_rot], axis=0)
        vcat = jnp.concatenate([v_prev, v], axis=0)
        if blk == 0:
            valid = band & (c >= jnp.where(j > 0, 0, tq))
        else:
            valid = band
        slabs = _attn_core(q_ref[0, rows, :], kcat, vcat, sinks_ref, valid, cos, slo, shi,
                           n_heads=n_heads, group=group, head_dim=head_dim)
        for s, o in enumerate(slabs):
            y_ref[0, rows, s * LANES:(s + 1) * LANES] = o
        k_prev, v_prev = k_rot, v
    kprev[...] = k_prev
    vprev[...] = v_prev
    last_ref[0, :, :LANES] = k_prev
    last_ref[0, :, LANES:] = v_prev


def _attn_prompt(proj, sinks, tables, q_block, kv_block, n_heads, n_kv, head_dim):
    nb, l, _ = proj.shape
    tq = ATTN_BLOCK
    qw = n_heads * head_dim
    kvw = 2 * n_kv * head_dim
    assert n_kv * head_dim == LANES
    kern = functools.partial(_attn_prompt_kernel, n_heads=n_heads, group=n_heads // n_kv, head_dim=head_dim)
    ts = _tile(l, ATTN_BLOCKS_PER_STEP * tq)
    tab_spec = pl.BlockSpec((ts, LANES), lambda b, j: (j, 0))
    return pl.pallas_call(
        kern,
        grid=(nb, l // ts),
        in_specs=[
            pl.BlockSpec(memory_space=pltpu.SMEM),
            pl.BlockSpec((1, ts, qw), lambda b, j: (b, j, q_block)),
            pl.BlockSpec((1, ts, kvw), lambda b, j: (b, j, kv_block)),
            tab_spec, tab_spec, tab_spec,
        ],
        out_specs=[
            pl.BlockSpec((1, ts, qw), lambda b, j: (b, j, 0)),
            pl.BlockSpec((1, tq, kvw), lambda b, j: (b, 0, 0)),
        ],
        out_shape=[
            jax.ShapeDtypeStruct((nb, l, qw), F32),
            jax.ShapeDtypeStruct((nb, tq, kvw), F32),
        ],
        scratch_shapes=[pltpu.VMEM((tq, LANES), F32), pltpu.VMEM((tq, LANES), F32)],
        compiler_params=_cparams("arbitrary", "arbitrary"),
        name="attn_prompt",
    )(sinks, proj, proj, *tables)


def _attn_sample_kernel(sinks_ref, q_ref, kv_ref, ck_ref, cv_ref, cos_ref, slo_ref, shi_ref, y_ref, knew_ref,
                        qh, kc, vc, ob, *, n_heads, group, head_dim, nt):
    gb, window = ck_ref.shape[0], ck_ref.shape[1]
    half_rot = head_dim // 8
    heads_per_slab = LANES // head_dim
    nq = n_heads * SUBLANES
    nk = kc.shape[1]
    lane = lax.broadcasted_iota(jnp.int32, (gb, LANES), 1)

    @pl.when(pl.program_id(0) == 0)
    def _():
        qh[...] = jnp.zeros_like(qh)
        kc[...] = jnp.zeros_like(kc)
        vc[...] = jnp.zeros_like(vc)

    kc[:, 0:window, :] = ck_ref[...]
    vc[:, 0:window, :] = cv_ref[...]
    for t in range(nt):
        cos, slo, shi = cos_ref[t:t + 1, :], slo_ref[t:t + 1, :], shi_ref[t:t + 1, :]
        kv = kv_ref[t]
        k_rot = _rope(kv[:, :LANES], cos, slo, shi, half_rot)
        knew_ref[t] = k_rot
        kc[:, window + t, :] = k_rot
        vc[:, window + t, :] = kv[:, LANES:]
        for s in range(n_heads // heads_per_slab):
            qs = _rope(q_ref[t][:, s * LANES:(s + 1) * LANES], cos, slo, shi, half_rot)
            qs_swapped = pltpu.roll(qs, head_dim, 1)
            for half in range(heads_per_slab):
                h = s * heads_per_slab + half
                g = (h // group) % heads_per_slab
                in_kv_half = (lane >= g * head_dim) & (lane < (g + 1) * head_dim)
                qh[:, h * SUBLANES + t, :] = jnp.where(in_kv_half, qs if g == half else qs_swapped, 0.0)

    sc = jnp.einsum("bqd,bkd->bqk", qh[...].astype(BF16), kc[...].astype(BF16),
                    preferred_element_type=F32) * (head_dim ** -0.5)
    r = lax.broadcasted_iota(jnp.int32, (nq, nk), 0) % SUBLANES
    c = lax.broadcasted_iota(jnp.int32, (nq, nk), 1)
    valid = (c >= r) & (c <= r + window)
    sc = jnp.where(valid[None], sc, -1e30)
    row_head = lax.broadcasted_iota(jnp.int32, (nq, 1), 0) // SUBLANES
    sink = jnp.zeros((nq, 1), F32)
    for h in range(n_heads):
        sink = jnp.where(row_head == h, sinks_ref[h], sink)
    m = jnp.maximum(jnp.max(sc, axis=-1, keepdims=True), sink[None])
    p = jnp.exp(sc - m)
    p = p / (jnp.sum(p, axis=-1, keepdims=True) + jnp.exp(sink[None] - m))
    ob[...] = jnp.einsum("bqk,bkd->bqd", p.astype(BF16), vc[...].astype(BF16), preferred_element_type=F32)

    for t in range(nt):
        for s in range(n_heads // heads_per_slab):
            out = jnp.zeros((gb, LANES), F32)
            for half in range(heads_per_slab):
                h = s * heads_per_slab + half
                g = (h // group) % heads_per_slab
                o = ob[:, h * SUBLANES + t, :]
                if g != half:
                    o = pltpu.roll(o, head_dim, 1)
                out = jnp.where((lane >= half * head_dim) & (lane < (half + 1) * head_dim), o, out)
            y_ref[t, :, s * LANES:(s + 1) * LANES] = out


def _attn_sample(proj_tm, ck, cv, sinks, tables, q_block, kv_block, n_heads, n_kv, head_dim):
    nt, nbat, _ = proj_tm.shape
    window = ck.shape[1]
    qw = n_heads * head_dim
    assert n_kv * head_dim == LANES and nt <= SUBLANES
    gb = _tile(nbat, 16)
    nk = window + 2 * SUBLANES
    kern = functools.partial(_attn_sample_kernel, n_heads=n_heads, group=n_heads // n_kv, head_dim=head_dim, nt=nt)
    tab_spec = pl.BlockSpec((SUBLANES, LANES), lambda b: (0, 0))
    return pl.pallas_call(
        kern,
        grid=(nbat // gb,),
        in_specs=[
            pl.BlockSpec(memory_space=pltpu.SMEM),
            pl.BlockSpec((nt, gb, qw), lambda b: (0, b, q_block)),
            pl.BlockSpec((nt, gb, 2 * LANES), lambda b: (0, b, kv_block)),
            pl.BlockSpec((gb, window, LANES), lambda b: (b, 0, 0)),
            pl.BlockSpec((gb, window, LANES), lambda b: (b, 0, 0)),
            tab_spec, tab_spec, tab_spec,
        ],
        out_specs=[
            pl.BlockSpec((nt, gb, qw), lambda b: (0, b, 0)),
            pl.BlockSpec((nt, gb, LANES), lambda b: (0, b, 0)),
        ],
        out_shape=[
            jax.ShapeDtypeStruct((nt, nbat, qw), F32),
            jax.ShapeDtypeStruct((nt, nbat, LANES), F32),
        ],
        scratch_shapes=[
            pltpu.VMEM((gb, n_heads * SUBLANES, LANES), F32),
            pltpu.VMEM((gb, nk, LANES), F32), pltpu.VMEM((gb, nk, LANES), F32),
            pltpu.VMEM((gb, n_heads * SUBLANES, LANES), F32),
        ],
        compiler_params=_cparams("arbitrary"),
        name="attn_sample",
    )(sinks, proj_tm, proj_tm, ck, cv, *tables)


def _rope_tables(pos, head_dim):
    rot = head_dim // 4
    half = rot // 2
    inv = jnp.exp(-(2.0 * jnp.arange(half, dtype=F32) / rot) * math.log(ROPE_THETA))
    ang = pos.astype(F32)[:, None] * inv[None, :]
    cos, sin = jnp.cos(ang), jnp.sin(ang)
    n = pos.shape[0]
    pad = jnp.zeros((n, head_dim - rot), F32)
    zeros = jnp.zeros((n, half), F32)
    cos_h = jnp.concatenate([cos, cos, pad + 1.0], axis=1)
    lo_h = jnp.concatenate([-sin, zeros, pad], axis=1)
    hi_h = jnp.concatenate([zeros, sin, pad], axis=1)
    reps = LANES // head_dim
    return tuple(jnp.tile(t, (1, reps)) for t in (cos_h, lo_h, hi_h))


def _gated_group_norm(y, z, norm_w, groups):
    y = y * _silu(z)
    gw = y.shape[1] // groups
    parts = []
    for g in range(groups):
        yg = y[:, g * gw:(g + 1) * gw]
        parts.append(yg * lax.rsqrt(jnp.mean(yg * yg, axis=-1, keepdims=True) + EPS))
    return jnp.concatenate(parts, axis=1) * norm_w


def _ssd_prompt_kernel(xbc_ref, z_ref, dt_ref, cw_ref, cb_ref, dtb_ref, alog_ref, dsk_ref, nw_ref, e_ref,
                       y_ref, hfin_ref, ext, ht, *, q, width, dstate, hdim, groups):
    j = pl.program_id(1)
    kconv = cw_ref.shape[0]

    @pl.when(j == 0)
    def _():
        ext[0:SUBLANES, :] = jnp.zeros((SUBLANES, ext.shape[1]), F32)
        ht[...] = jnp.zeros_like(ht)

    xbc = xbc_ref[0]
    ext[SUBLANES:, :] = xbc
    cw = cw_ref[...]
    conv = cw[kconv - 1:kconv] * xbc + cb_ref[...]
    for t in range(kconv - 1):
        conv = conv + cw[t:t + 1] * ext[pl.ds(SUBLANES - (kconv - 1) + t, q), :]
    ext[0:SUBLANES, :] = xbc[q - SUBLANES:, :]
    act = _silu(conv)
    xs = act[:, :width]
    bm = act[:, width:width + groups * dstate]
    cm = act[:, width + groups * dstate:]

    expand = e_ref[...]
    nh = expand.shape[0]
    dt_h = _softplus(dt_ref[0] + dtb_ref[...])
    a_h = dt_h * (-jnp.exp(alog_ref[...]))
    row = lax.broadcasted_iota(jnp.int32, (q, q), 0)
    col = lax.broadcasted_iota(jnp.int32, (q, q), 1)
    causal = col <= row
    tri = jnp.where(causal, 1.0, 0.0).astype(BF16)
    acum_h = _dot_exact_lhs01(tri, a_h)
    acum_ht = acum_h.T
    dt_x = _dot_exact_rhs01(dt_h[:, :nh], expand)
    acum_x = _dot_exact_rhs01(acum_h[:, :nh], expand)
    xdt = xs * dt_x
    acum_last = acum_x[q - 1:q, :]
    xd = xdt * jnp.exp(acum_last - acum_x)
    chunk_decay = jnp.exp(acum_last)
    exp_acum = jnp.exp(acum_x)

    lane = lax.broadcasted_iota(jnp.int32, (q, LANES), 1)
    gw = width // groups
    heads_per_group = gw // hdim
    pair = LANES // hdim
    y_parts = []
    for g in range(groups):
        b_g = bm[:, g * dstate:(g + 1) * dstate].astype(BF16)
        c_g = cm[:, g * dstate:(g + 1) * dstate].astype(BF16)
        cbm = _dot_nt(c_g, b_g)
        h_g = ht[:, g * gw:(g + 1) * gw]
        y_off = _dot(c_g, h_g.astype(BF16)) * exp_acum[:, g * gw:(g + 1) * gw]
        diag_parts = []
        for jp in range(heads_per_group // pair):
            l0 = g * gw + jp * LANES
            x_pair = xdt[:, l0:l0 + LANES].astype(BF16)
            out = jnp.zeros((q, LANES), F32)
            for half in range(pair):
                hd = g * heads_per_group + jp * pair + half
                decay = jnp.exp(jnp.where(causal, acum_h[:, hd:hd + 1] - acum_ht[hd:hd + 1, :], -jnp.inf))
                res = _dot((cbm * decay).astype(BF16), x_pair)
                in_head = (lane >= half * hdim) & (lane < (half + 1) * hdim)
                out = jnp.where(in_head, res, out)
            diag_parts.append(out)
        y_parts.append(jnp.concatenate(diag_parts, axis=1) + y_off)
        s_t = _dot_tn(b_g, xd[:, g * gw:(g + 1) * gw].astype(BF16))
        ht[:, g * gw:(g + 1) * gw] = h_g * chunk_decay[:, g * gw:(g + 1) * gw] + s_t
    y = jnp.concatenate(y_parts, axis=1) + xs * dsk_ref[...]
    y_ref[0] = _gated_group_norm(y, z_ref[0], nw_ref[...], groups)

    @pl.when(j == pl.num_programs(1) - 1)
    def _():
        hfin_ref[0] = ht[...].T


def _ssd_prompt(proj, blocks, params, dims):
    nb, l, _ = proj.shape
    q = SSD_CHUNK
    width, dstate, hdim, groups, xbcw = dims
    xbc_block, z_block, dt_block = blocks
    cw, cb, dtb, alog, dsk, nw, expand = params
    kern = functools.partial(_ssd_prompt_kernel, q=q, width=width, dstate=dstate, hdim=hdim, groups=groups)
    full = lambda a: pl.BlockSpec(a.shape, lambda b, j: (0,) * a.ndim)
    return pl.pallas_call(
        kern,
        grid=(nb, l // q),
        in_specs=[
            pl.BlockSpec((1, q, xbcw), lambda b, j: (b, j, xbc_block)),
            pl.BlockSpec((1, q, width), lambda b, j: (b, j, z_block)),
            pl.BlockSpec((1, q, LANES), lambda b, j: (b, j, dt_block)),
            full(cw), full(cb), full(dtb), full(alog), full(dsk), full(nw), full(expand),
        ],
        out_specs=[
            pl.BlockSpec((1, q, width), lambda b, j: (b, j, 0)),
            pl.BlockSpec((1, width, dstate), lambda b, j: (b, 0, 0)),
        ],
        out_shape=[
            jax.ShapeDtypeStruct((nb, l, width), F32),
            jax.ShapeDtypeStruct((nb, width, dstate), F32),
        ],
        scratch_shapes=[pltpu.VMEM((q + SUBLANES, xbcw), F32), pltpu.VMEM((dstate, width), F32)],
        compiler_params=_cparams("arbitrary", "arbitrary"),
        name="ssd_prompt",
    )(proj, proj, proj, cw, cb, dtb, alog, dsk, nw, expand)


def _ssd_sample_pre_kernel(xbc_ref, dt_ref, st_ref, cw_ref, cb_ref, dtb_ref, alog_ref, dsk_ref, e_ref,
                           ypart_ref, expa_ref, xd_ref, b_ref, c_ref, cd_ref,
                           *, nt, nbat, width, dstate, groups):
    kconv = cw_ref.shape[0]
    cw = cw_ref[...]
    expand = e_ref[...]
    nh = expand.shape[0]
    neg_a = -jnp.exp(alog_ref[...])
    full = [st_ref[t] for t in range(kconv - 1)]
    for t in range(nt):
        full.append(xbc_ref[pl.ds(t * nbat, nbat), :])
    xs, bm, cm, dt_x, acum_x, xdt = [], [], [], [], [], []
    acum_h = None
    for t in range(nt):
        conv = cb_ref[...] + cw[0:1] * full[t]
        for jj in range(1, kconv):
            conv = conv + cw[jj:jj + 1] * full[t + jj]
        act = _silu(conv)
        xs.append(act[:, :width])
        bm.append(act[:, width:width + groups * dstate])
        cm.append(act[:, width + groups * dstate:])
        dt_h = _softplus(dt_ref[pl.ds(t * nbat, nbat), :] + dtb_ref[...])
        a_h = dt_h * neg_a
        acum_h = a_h if acum_h is None else acum_h + a_h
        dt_x.append(_dot_exact_rhs01(dt_h[:, :nh], expand))
        acum_x.append(_dot_exact_rhs01(acum_h[:, :nh], expand))
        xdt.append(xs[t] * dt_x[t])
    cd_ref[...] = jnp.exp(acum_h)
    gw = width // groups
    for t in range(nt):
        y = xs[t] * dsk_ref[...]
        for s in range(t + 1):
            cb_parts = []
            for g in range(groups):
                prod = cm[t][:, g * dstate:(g + 1) * dstate] * bm[s][:, g * dstate:(g + 1) * dstate]
                cb_parts.append(jnp.broadcast_to(jnp.sum(prod, axis=-1, keepdims=True), (nbat, gw)))
            cb_x = jnp.concatenate(cb_parts, axis=1)
            y = y + cb_x * jnp.exp(acum_x[t] - acum_x[s]) * xdt[s]
        ypart_ref[t] = y
        expa_ref[t] = jnp.exp(acum_x[t])
    xd_ref[...] = jnp.zeros_like(xd_ref)
    b_ref[...] = jnp.zeros_like(b_ref)
    c_ref[...] = jnp.zeros_like(c_ref)
    for t in range(nt):
        xd_ref[:, t, :] = xdt[t] * jnp.exp(acum_x[nt - 1] - acum_x[t])
        b_ref[:, t, :] = bm[t]
        c_ref[:, t, :] = cm[t]


def _ssd_sample_pre(proj_s, blocks, state_tm, params, nt, nbat, dims, srows):
    width, dstate, hdim, groups, xbcw = dims
    xbc_block, dt_block = blocks
    cw, cb, dtb, alog, dsk, expand = params
    rows = nt * nbat
    kern = functools.partial(_ssd_sample_pre_kernel, nt=nt, nbat=nbat, width=width, dstate=dstate, groups=groups)
    sd = jax.ShapeDtypeStruct
    out_shapes = [(nt, nbat, width), (nt, nbat, width), (nbat, srows, width),
                  (nbat, srows, groups * dstate), (nbat, srows, groups * dstate), (nbat, LANES)]
    return pl.pallas_call(
        kern,
        grid=(1,),
        in_specs=[pl.BlockSpec((rows, xbcw), lambda i: (0, xbc_block)),
                  pl.BlockSpec((rows, LANES), lambda i: (0, dt_block)),
                  _whole(state_tm.shape)] + [_whole(a.shape) for a in params],
        out_specs=[_whole(s) for s in out_shapes],
        out_shape=[sd(s, F32) for s in out_shapes],
        compiler_params=_cparams("arbitrary"),
        name="ssd_sample_pre",
    )(proj_s, proj_s, state_tm, cw, cb, dtb, alog, dsk, expand)


def _ssd_sample_state_kernel(cd_ref, c_ref, b_ref, xd_ref, h0_ref, *rest, heads, hdim, dstate, groups):
    yoff_ref, hnew_ref = rest[-2:]
    gb = h0_ref.shape[1]
    hpg = heads // groups
    gw = hpg * hdim
    for i in range(gb):
        b = pl.program_id(0) * gb + i
        for g in range(groups):
            hm = h0_ref[0, i, g * hpg:(g + 1) * hpg].reshape(gw, dstate)
            c_g = c_ref[i, :, g * dstate:(g + 1) * dstate].astype(BF16)
            b_g = b_ref[i, :, g * dstate:(g + 1) * dstate].astype(BF16)
            yoff_ref[i, :, g * gw:(g + 1) * gw] = _dot_nt(c_g, hm.astype(BF16))
            upd = _dot_tn(xd_ref[i, :, g * gw:(g + 1) * gw].astype(BF16), b_g)
            for hh in range(hpg):
                hd = g * hpg + hh
                hnew_ref[0, i, hd] = (h0_ref[0, i, hd] * cd_ref[b * heads + hd]
                                      + upd[hh * hdim:(hh + 1) * hdim, :])
    for later in range(1, hnew_ref.shape[0]):
        hnew_ref[later] = jnp.zeros(hnew_ref.shape[1:], F32)


def _ssd_sample_state(cd_flat, c_bm, b_bm, xd_bm, state_all, layer, stacked, dims):
    width, dstate, hdim, groups, _ = dims
    depth, nbat, heads = state_all.shape[:3]
    rows = c_bm.shape[1]
    gb = _tile(nbat, 4)
    kern = functools.partial(_ssd_sample_state_kernel, heads=heads, hdim=hdim, dstate=dstate, groups=groups)
    in_specs = [
        pl.BlockSpec(memory_space=pltpu.SMEM),
        pl.BlockSpec((gb, rows, groups * dstate), lambda b: (b, 0, 0)),
        pl.BlockSpec((gb, rows, groups * dstate), lambda b: (b, 0, 0)),
        pl.BlockSpec((gb, rows, width), lambda b: (b, 0, 0)),
        pl.BlockSpec((1, gb, heads, hdim, dstate), lambda b: (layer, b, 0, 0, 0)),
    ]
    args = [cd_flat, c_bm, b_bm, xd_bm, state_all]
    if layer == 0:
        assert stacked is None
        state_spec = pl.BlockSpec((depth, gb, heads, hdim, dstate), lambda b: (0, b, 0, 0, 0))
        aliases = {}
    else:
        in_specs.append(pl.BlockSpec(memory_space=pl.ANY))
        args.append(stacked)
        state_spec = pl.BlockSpec((1, gb, heads, hdim, dstate), lambda b: (layer, b, 0, 0, 0))
        aliases = {len(args) - 1: 1}
    return pl.pallas_call(
        kern,
        grid=(nbat // gb,),
        in_specs=in_specs,
        out_specs=[pl.BlockSpec((gb, rows, width), lambda b: (b, 0, 0)), state_spec],
        out_shape=[
            jax.ShapeDtypeStruct((nbat, rows, width), F32),
            jax.ShapeDtypeStruct(state_all.shape, F32),
        ],
        input_output_aliases=aliases,
        compiler_params=_cparams("arbitrary"),
        name="ssd_sample_state",
    )(*args)


def _ssd_sample_post_kernel(ypart_ref, yoff_ref, expa_ref, z_ref, nw_ref, y_ref, *, groups, nt, nbat):
    for t in range(nt):
        rows = pl.ds(t * nbat, nbat)
        y = ypart_ref[t] + yoff_ref[:, t, :] * expa_ref[t]
        y_ref[rows, :] = _gated_group_norm(y, z_ref[rows, :], nw_ref[...], groups)


def _ssd_sample_post(ypart, yoff_bm, expa, proj_s, z_block, nw, groups):
    nt, nbat, width = ypart.shape
    rows = nt * nbat
    kern = functools.partial(_ssd_sample_post_kernel, groups=groups, nt=nt, nbat=nbat)
    return pl.pallas_call(
        kern,
        grid=(1,),
        in_specs=[_whole(ypart.shape), _whole(yoff_bm.shape), _whole(expa.shape),
                  pl.BlockSpec((rows, width), lambda i: (0, z_block)), _whole(nw.shape)],
        out_specs=_whole((rows, width)),
        out_shape=jax.ShapeDtypeStruct((rows, width), F32),
        compiler_params=_cparams("arbitrary"),
        name="ssd_sample_post",
    )(ypart, yoff_bm, expa, proj_s, nw)


def _merge_math(x, gates, y_conv, y_attn, y_ssm, gate1, wc_ref, wa_ref, ws_ref, wo_ref, d):
    merged = (_sigmoid(gates[:, :d]) * _dot(y_conv.astype(BF16), wc_ref[...])
              + _sigmoid(gates[:, d:2 * d]) * _dot(y_attn.astype(BF16), wa_ref[...])
              + _sigmoid(gates[:, 2 * d:]) * _dot(y_ssm.astype(BF16), ws_ref[...]))
    return x + gate1 * _dot(merged.astype(BF16), wo_ref[...])


def _merge_kernel(*refs, d, tm, cw, conv):
    refs = list(refs)
    x_ref, g_ref = refs[:2]
    del refs[:2]
    if conv:
        p_ref, cw_ref = refs[:2]
        del refs[:2]
    else:
        yc_ref = refs.pop(0)
    ya_ref, ys_ref, g1_ref, wc_ref, wa_ref, ws_ref, wo_ref = refs[:7]
    del refs[:7]
    o_ref = refs.pop(0)

    if conv:
        st_ref, ext = refs

        @pl.when(pl.program_id(1) == 0)
        def _():
            ext[0:SUBLANES, :] = jnp.zeros((SUBLANES, cw), F32)

        p = p_ref[0]
        u = p[:, cw:2 * cw] * p[:, 2 * cw:]
        ext[SUBLANES:, :] = u
        w = cw_ref[...]
        acc = w[0:1] * ext[pl.ds(SUBLANES - 2, tm), :] + w[1:2] * ext[pl.ds(SUBLANES - 1, tm), :] + w[2:3] * u
        tail = u[tm - SUBLANES:, :]
        ext[0:SUBLANES, :] = tail
        st_ref[0] = tail
        y_conv = p[:, :cw] * acc
    else:
        y_conv = yc_ref[0]

    o_ref[0] = _merge_math(x_ref[0], g_ref[0], y_conv, ya_ref[0], ys_ref[0], g1_ref[0],
                           wc_ref, wa_ref, ws_ref, wo_ref, d)


def _merge(x, proj, gate_block, conv_src, ya, ys, mod, weights):
    nb, l, d = x.shape
    tm = _tile(l, ROUTE_TILE)
    nt = l // tm
    conv = isinstance(conv_src, tuple)
    tok = lambda w: pl.BlockSpec((1, tm, w), lambda b, i: (b, i, 0))
    full = lambda a: pl.BlockSpec(a.shape, lambda b, i: (0, 0))
    in_specs = [tok(d), pl.BlockSpec((1, tm, 3 * d), lambda b, i: (b, i, gate_block))]
    args = [x, proj]
    out_specs = [tok(d)]
    out_shape = [jax.ShapeDtypeStruct((nb, l, d), F32)]
    scratch = []
    if conv:
        conv_block, w_conv = conv_src
        cw = w_conv.shape[1]
        assert w_conv.shape[0] == 3
        in_specs += [pl.BlockSpec((1, tm, 3 * cw), lambda b, i: (b, i, conv_block)), full(w_conv)]
        args += [proj, w_conv]
        out_specs.append(pl.BlockSpec((1, SUBLANES, cw), lambda b, i: (b, 0, 0)))
        out_shape.append(jax.ShapeDtypeStruct((nb, SUBLANES, cw), F32))
        scratch.append(pltpu.VMEM((tm + SUBLANES, cw), F32))
    else:
        cw = conv_src.shape[2]
        in_specs.append(tok(cw))
        args.append(conv_src)
    in_specs += [tok(ya.shape[2]), tok(ys.shape[2]), _mod_spec(mod, tm, 2, d)] + [full(w) for w in weights]
    args += [ya, ys, mod] + list(weights)
    kern = functools.partial(_merge_kernel, d=d, tm=tm, cw=cw, conv=conv)
    return pl.pallas_call(
        kern,
        grid=(nb, nt),
        in_specs=in_specs,
        out_specs=out_specs,
        out_shape=out_shape,
        scratch_shapes=scratch,
        compiler_params=_cparams("arbitrary", "arbitrary"),
        name="merge",
    )(*args)


def _finish(x, gate, f, nf_ref, final_norm):
    out = x + gate * f
    if final_norm:
        out = out * lax.rsqrt(jnp.mean(out * out, axis=-1, keepdims=True) + EPS) * nf_ref[...]
    return out


def _ffn_kernel(x_ref, sh_ref, sc_ref, g2_ref, nw_ref, nf_ref, wg_ref, wu_ref, wd_ref, o_ref, h_scr, acc,
                *, final_norm):
    f = pl.program_id(2)

    @pl.when(f == 0)
    def _():
        h_scr[...] = _rms_mod(x_ref[0], nw_ref[...], sc_ref[0], sh_ref[0]).astype(BF16)
        acc[...] = jnp.zeros_like(acc)

    h = h_scr[...]
    a = _silu(_dot(h, wg_ref[...].astype(BF16))) * _dot(h, wu_ref[...].astype(BF16))
    acc[...] += _dot(a.astype(BF16), wd_ref[...].astype(BF16))

    @pl.when(f == pl.num_programs(2) - 1)
    def _():
        o_ref[0] = _finish(x_ref[0], g2_ref[0], acc[...], nf_ref, final_norm)


def _ffn(x, mod, norm_w, norm_final, wg, wu, wd, final_norm):
    nb, l, d = x.shape
    ff = wg.shape[1]
    tm = _tile(l, 1024)
    tf = _tile(ff, 512)
    kern = functools.partial(_ffn_kernel, final_norm=final_norm)
    vec = pl.BlockSpec((1, d), lambda b, i, f: (0, 0))
    return pl.pallas_call(
        kern,
        grid=(nb, l // tm, ff // tf),
        in_specs=[
            pl.BlockSpec((1, tm, d), lambda b, i, f: (b, i, 0)),
            _mod_spec(mod, tm, 3, d), _mod_spec(mod, tm, 4, d), _mod_spec(mod, tm, 5, d),
            vec, vec,
            pl.BlockSpec((d, tf), lambda b, i, f: (0, f)),
            pl.BlockSpec((d, tf), lambda b, i, f: (0, f)),
            pl.BlockSpec((tf, d), lambda b, i, f: (f, 0)),
        ],
        out_specs=pl.BlockSpec((1, tm, d), lambda b, i, f: (b, i, 0)),
        out_shape=jax.ShapeDtypeStruct((nb, l, d), F32),
        scratch_shapes=[pltpu.VMEM((tm, d), BF16), pltpu.VMEM((tm, d), F32)],
        compiler_params=_cparams("arbitrary", "arbitrary", "arbitrary"),
        name="ffn",
    )(x, mod, mod, mod, norm_w.reshape(1, d), norm_final.reshape(1, d), wg, wu, wd)


def _route_math(x, norm_w, scale, shift, wr, n_experts):
    h = _rms_mod(x, norm_w, scale, shift)
    h_hi = h.astype(BF16)
    h_lo = (h - h_hi.astype(F32)).astype(BF16)
    r_hi = wr.astype(BF16)
    r_lo = (wr - r_hi.astype(F32)).astype(BF16)
    logits = _dot(h_hi, r_hi) + _dot(h_lo, r_hi) + _dot(h_hi, r_lo)
    lane = lax.broadcasted_iota(jnp.int32, logits.shape, 1).astype(F32)
    neg = -jnp.inf
    lg = jnp.where(lane < n_experts, logits, neg)
    m1 = jnp.max(lg, axis=-1, keepdims=True)
    i1 = jnp.min(jnp.where(lg == m1, lane, float(LANES)), axis=-1, keepdims=True)
    rest = jnp.where(lane == i1, neg, lg)
    m2 = jnp.max(rest, axis=-1, keepdims=True)
    i2 = jnp.min(jnp.where(rest == m2, lane, float(LANES)), axis=-1, keepdims=True)
    e2 = jnp.exp(m2 - m1)
    w1 = 1.0 / (1.0 + e2)
    w2 = e2 / (1.0 + e2)
    w_cols = jnp.where(lane == 0.0, w1, jnp.where(lane == 1.0, w2, 0.0))
    chosen = jnp.where(lane == 0.0, i1, jnp.where(lane == 1.0, i2, 0.0))
    return w_cols, chosen.T[:SUBLANES, :]


def _route_kernel(x_ref, sh_ref, sc_ref, nw_ref, wr_ref, w_ref, e_ref, *, n_experts):
    w_cols, e_rows = _route_math(x_ref[0], nw_ref[...], sc_ref[0], sh_ref[0], wr_ref[...], n_experts)
    w_ref[...] = w_cols
    e_ref[...] = e_rows


def _route(x, mod, norm_w, w_router_pad, n_experts):
    nb, l, d = x.shape
    tm = _tile(l, ROUTE_TILE)
    nt = l // tm
    kern = functools.partial(_route_kernel, n_experts=n_experts)
    return pl.pallas_call(
        kern,
        grid=(nb, nt),
        in_specs=[
            pl.BlockSpec((1, tm, d), lambda b, i: (b, i, 0)),
            _mod_spec(mod, tm, 3, d), _mod_spec(mod, tm, 4, d),
            pl.BlockSpec((1, d), lambda b, i: (0, 0)),
            pl.BlockSpec((d, LANES), lambda b, i: (0, 0)),
        ],
        out_specs=[
            pl.BlockSpec((tm, LANES), lambda b, i: (b * nt + i, 0)),
            pl.BlockSpec((SUBLANES, tm), lambda b, i: (0, b * nt + i)),
        ],
        out_shape=[
            jax.ShapeDtypeStruct((nb * l, LANES), F32),
            jax.ShapeDtypeStruct((SUBLANES, nb * l), F32),
        ],
        compiler_params=_cparams("arbitrary", "arbitrary"),
        name="moe_route",
    )(x, mod, mod, norm_w.reshape(1, d), w_router_pad)


def _plan_kernel(e_ref, pos_ref, te_ref, *, n_experts, tile, row_tile):
    steps = e_ref.shape[1] // tile
    sub = lax.broadcasted_iota(jnp.int32, (SUBLANES, tile), 0).astype(F32)
    sub_col = lax.broadcasted_iota(jnp.int32, (SUBLANES, 1), 0)

    def member(i):
        blk = e_ref[:, pl.ds(pl.multiple_of(i * tile, tile), tile)]
        e1, e2 = blk[0:1, :], blk[1:2, :]
        return e1, e2, jnp.where((sub == e1) | (sub == e2), 1.0, 0.0)

    def count_body(i, cnt):
        return cnt + jnp.sum(member(i)[2], axis=1, keepdims=True)

    cnt = lax.fori_loop(0, steps, count_body, jnp.zeros((SUBLANES, 1), F32))
    padded = jnp.floor((cnt + (row_tile - 1)) * (1.0 / row_tile)) * row_tile
    off = jnp.zeros((SUBLANES, 1), F32)
    run = jnp.zeros((1, 1), F32)
    for e in range(n_experts):
        off = jnp.where(sub_col == e, run, off)
        run = run + padded[e:e + 1, :]
    seg_end = off + padded

    r = lax.broadcasted_iota(jnp.int32, (tile, tile), 0)
    c = lax.broadcasted_iota(jnp.int32, (tile, tile), 1)
    before = jnp.where(r < c, 1.0, 0.0).astype(BF16)

    def pos_body(i, carry):
        e1, e2, m = member(i)
        val = off + carry + _dot(m.astype(BF16), before)
        p1 = jnp.sum(jnp.where(sub == e1, val, 0.0), axis=0, keepdims=True)
        p2 = jnp.sum(jnp.where(sub == e2, val, 0.0), axis=0, keepdims=True)
        rows = jnp.where(sub == 0.0, p1, jnp.where(sub == 1.0, p2, 0.0))
        pos_ref[:, pl.ds(pl.multiple_of(i * tile, tile), tile)] = rows.astype(jnp.int32)
        return carry + jnp.sum(m, axis=1, keepdims=True)

    lax.fori_loop(0, steps, pos_body, jnp.zeros((SUBLANES, 1), F32))

    sub_l = lax.broadcasted_iota(jnp.int32, (SUBLANES, LANES), 0)
    start = lax.broadcasted_iota(jnp.int32, (SUBLANES, LANES), 1).astype(F32) * row_tile
    owner = jnp.sum(jnp.where((seg_end <= start) & (sub_l < n_experts), 1.0, 0.0), axis=0, keepdims=True)
    owner = jnp.minimum(owner, n_experts - 1.0)
    used = run * (1.0 / row_tile)
    te_ref[...] = jnp.where(sub_l == 0, owner, jnp.where(sub_l == 1, used, 0.0)).astype(jnp.int32)


def _plan(e_all, n_experts, row_tile):
    t = e_all.shape[1]
    assert t % LANES == 0 and n_experts <= SUBLANES
    tile = _tile(t, ROUTE_TILE)
    kern = functools.partial(_plan_kernel, n_experts=n_experts, tile=tile, row_tile=row_tile)
    return pl.pallas_call(
        kern,
        out_shape=[
            jax.ShapeDtypeStruct((SUBLANES, t), jnp.int32),
            jax.ShapeDtypeStruct((SUBLANES, LANES), jnp.int32),
        ],
        compiler_params=pltpu.CompilerParams(vmem_limit_bytes=VMEM_LIMIT_BYTES),
        name="moe_plan",
    )(e_all)


def _scatter_kernel(p1_ref, p2_ref, x_ref, sh_ref, sc_ref, nw_ref, xs_in_ref, xs_ref, h_scr, sem, *, tm):
    del xs_in_ref
    h_scr[...] = _rms_mod(x_ref[0], nw_ref[...], sc_ref[0], sh_ref[0])

    def row_copy(t, p):
        return pltpu.make_async_copy(h_scr.at[pl.ds(t, 1), :], xs_ref.at[pl.ds(p, 1), :], sem)

    def issue(i, carry):
        for k in range(DMA_UNROLL):
            t = i * DMA_UNROLL + k
            row_copy(t, p1_ref[t]).start(priority=0)
            row_copy(t, p2_ref[t]).start(priority=1)
        return carry

    def drain(t, carry):
        row_copy(0, 0).wait()
        row_copy(0, 0).wait()
        return carry

    lax.fori_loop(0, tm // DMA_UNROLL, issue, 0)
    lax.fori_loop(0, tm, drain, 0, unroll=DMA_UNROLL)


def _scatter(x, mod, norm_w, p1, p2, xs_sorted):
    nb, l, d = x.shape
    tm = _tile(l, ROUTE_TILE)
    nt = l // tm
    kern = functools.partial(_scatter_kernel, tm=tm)
    idx = pl.BlockSpec((tm,), lambda b, i: (b * nt + i,), memory_space=pltpu.SMEM)
    return pl.pallas_call(
        kern,
        grid=(nb, nt),
        in_specs=[
            idx, idx,
            pl.BlockSpec((1, tm, d), lambda b, i: (b, i, 0)),
            _mod_spec(mod, tm, 3, d), _mod_spec(mod, tm, 4, d),
            pl.BlockSpec((1, d), lambda b, i: (0, 0)),
            pl.BlockSpec(memory_space=pl.ANY),
        ],
        out_specs=pl.BlockSpec(memory_space=pl.ANY),
        out_shape=jax.ShapeDtypeStruct(xs_sorted.shape, F32),
        scratch_shapes=[pltpu.VMEM((tm, d), F32), pltpu.SemaphoreType.DMA],
        input_output_aliases={6: 0},
        compiler_params=_cparams("arbitrary", "arbitrary"),
        name="moe_scatter",
    )(p1, p2, x, mod, mod, norm_w.reshape(1, d), xs_sorted)


def _group_ffn_kernel(te_ref, xs_ref, wg_ref, wu_ref, wd_ref, y_ref, h_scr, acc):
    j = pl.program_id(0)
    f = pl.program_id(1)

    @pl.when(j < te_ref[LANES])
    def _():
        @pl.when(f == 0)
        def _():
            h_scr[...] = xs_ref[...].astype(BF16)
            acc[...] = jnp.zeros_like(acc)

        h = h_scr[...]
        a = _silu(_dot(h, wg_ref[0].astype(BF16))) * _dot(h, wu_ref[0].astype(BF16))
        acc[...] += _dot(a.astype(BF16), wd_ref[0].astype(BF16))

        @pl.when(f == pl.num_programs(1) - 1)
        def _():
            y_ref[...] = acc[...]

    @pl.when((j >= te_ref[LANES]) & (f == 0))
    def _():
        y_ref[...] = jnp.zeros_like(y_ref)


def _group_ffn(te_flat, xs_sorted, wg, wu, wd, row_tile):
    rows, d = xs_sorted.shape
    ff = wg.shape[2]
    tf = _tile(ff, 512)
    nf = ff // tf

    def tile_of(j, te):
        return jnp.minimum(j, te[LANES] - 1)

    def f_of(j, f, te):
        return jnp.where(j < te[LANES], f, nf - 1)

    grid_spec = pltpu.PrefetchScalarGridSpec(
        num_scalar_prefetch=1,
        grid=(rows // row_tile, nf),
        in_specs=[
            pl.BlockSpec((row_tile, d), lambda j, f, te: (tile_of(j, te), 0)),
            pl.BlockSpec((1, d, tf), lambda j, f, te: (te[tile_of(j, te)], 0, f_of(j, f, te))),
            pl.BlockSpec((1, d, tf), lambda j, f, te: (te[tile_of(j, te)], 0, f_of(j, f, te))),
            pl.BlockSpec((1, tf, d), lambda j, f, te: (te[tile_of(j, te)], f_of(j, f, te), 0)),
        ],
        out_specs=pl.BlockSpec((row_tile, d), lambda j, f, te: (j, 0)),
        scratch_shapes=[pltpu.VMEM((row_tile, d), BF16), pltpu.VMEM((row_tile, d), F32)],
    )
    return pl.pallas_call(
        _group_ffn_kernel,
        grid_spec=grid_spec,
        out_shape=jax.ShapeDtypeStruct((rows, d), F32),
        compiler_params=_cparams("arbitrary", "arbitrary"),
        name="moe_group_ffn",
    )(te_flat, xs_sorted, wg, wu, wd)


def _combine_kernel(p1_ref, p2_ref, x_ref, g2_ref, w_ref, nf_ref, y_hbm, o_ref, buf, sems,
                    *, tm, nt, n_steps, final_norm):
    step = pl.program_id(0) * nt + pl.program_id(1)

    def row_copy(slot, k, t, p):
        return pltpu.make_async_copy(y_hbm.at[pl.ds(p, 1), :], buf.at[slot, k, pl.ds(t, 1), :], sems.at[slot])

    def start_tile(tile, slot):
        def issue(i, carry):
            for k in range(DMA_UNROLL):
                t = i * DMA_UNROLL + k
                row_copy(slot, 0, t, p1_ref[tile * tm + t]).start(priority=0)
                row_copy(slot, 1, t, p2_ref[tile * tm + t]).start(priority=1)
            return carry
        lax.fori_loop(0, tm // DMA_UNROLL, issue, 0)

    def wait_tile(slot):
        def drain(t, carry):
            row_copy(slot, 0, 0, 0).wait()
            row_copy(slot, 0, 0, 0).wait()
            return carry
        lax.fori_loop(0, tm, drain, 0, unroll=DMA_UNROLL)

    @pl.when(step == 0)
    def _():
        start_tile(0, 0)

    for slot in range(2):
        @pl.when(step % 2 == slot)
        def _(slot=slot):
            @pl.when(step + 1 < n_steps)
            def _():
                start_tile(step + 1, 1 - slot)

            wait_tile(slot)
            w = w_ref[...]
            f = w[:, 0:1] * buf[slot, 0] + w[:, 1:2] * buf[slot, 1]
            o_ref[0] = _finish(x_ref[0], g2_ref[0], f, nf_ref, final_norm)


def _combine(x, mod, w_cols, p1, p2, y_sorted, norm_final, final_norm):
    nb, l, d = x.shape
    tm = _tile(l, ROUTE_TILE)
    nt = l // tm
    kern = functools.partial(_combine_kernel, tm=tm, nt=nt, n_steps=nb * nt, final_norm=final_norm)
    grid_spec = pltpu.PrefetchScalarGridSpec(
        num_scalar_prefetch=2,
        grid=(nb, nt),
        in_specs=[
            pl.BlockSpec((1, tm, d), lambda b, i, *_: (b, i, 0)),
            _mod_spec(mod, tm, 5, d),
            pl.BlockSpec((tm, LANES), lambda b, i, *_: (b * nt + i, 0)),
            pl.BlockSpec((1, d), lambda b, i, *_: (0, 0)),
            pl.BlockSpec(memory_space=pl.ANY),
        ],
        out_specs=pl.BlockSpec((1, tm, d), lambda b, i, *_: (b, i, 0)),
        scratch_shapes=[pltpu.VMEM((2, 2, tm, d), F32), pltpu.SemaphoreType.DMA((2,))],
    )
    return pl.pallas_call(
        kern,
        grid_spec=grid_spec,
        out_shape=jax.ShapeDtypeStruct((nb, l, d), F32),
        compiler_params=_cparams("arbitrary", "arbitrary"),
        name="moe_combine",
    )(p1, p2, x, mod, w_cols, norm_final.reshape(1, d), y_sorted)


def _moe(groups, norm_w, norm_final, w_router_pad, wg, wu, wd, final_norm):
    n_experts = wg.shape[0]
    d = groups[0][0].shape[2]
    routed = [_route(x, mod, norm_w, w_router_pad, n_experts) for x, mod in groups]
    e_all = jnp.concatenate([e for _, e in routed], axis=1)
    t = e_all.shape[1]
    pos, te = _plan(e_all, n_experts, MOE_ROW_TILE)
    te_flat = te.reshape(SUBLANES * LANES)
    n_tiles = -(-TOP_K * t // MOE_ROW_TILE) + n_experts
    assert n_tiles <= LANES
    xs_sorted = jnp.zeros((n_tiles * MOE_ROW_TILE, d), F32)
    spans, start = [], 0
    for x, _ in groups:
        n = x.shape[0] * x.shape[1]
        spans.append((start, start + n))
        start += n
    for (x, mod), (lo, hi) in zip(groups, spans):
        xs_sorted = _scatter(x, mod, norm_w, pos[0, lo:hi], pos[1, lo:hi], xs_sorted)
    y_sorted = _group_ffn(te_flat, xs_sorted, wg, wu, wd, MOE_ROW_TILE)
    return [_combine(x, mod, w_cols, pos[0, lo:hi], pos[1, lo:hi], y_sorted, norm_final, final_norm)
            for (x, mod), (w_cols, _), (lo, hi) in zip(groups, routed, spans)]


def kernel(x_prompt, x_sample, c_prompt, c_sample, cache_k, cache_v, state_conv, state_ssm_conv, state_ssm,
           w_mod, b_mod, norm_mix, norm_ffn, norm_final, w_in, w_sconv, sinks, ssm_conv_w, ssm_conv_b,
           dt_bias, a_log, d_skip, ssm_norm, w_br_conv, w_br_attn, w_br_ssm, w_o,
           ffn_w_gate, ffn_w_up, ffn_w_down, router, moe_w_gate, moe_w_up, moe_w_down):
    nbp, seq, d = x_prompt.shape
    nbat, nt, _ = x_sample.shape
    depth = w_mod.shape[0]
    cwid = w_sconv.shape[2]
    n_heads = sinks.shape[1]
    window, n_kv, head_dim = cache_k.shape[2:]
    heads, hdim, dstate = state_ssm.shape[2:]
    width = heads * hdim
    xbcw = ssm_conv_w.shape[2]
    groups = SSM_GROUPS
    n_experts = router.shape[2]
    aw = n_heads * head_dim
    kvw = n_kv * head_dim
    dims = (width, dstate, hdim, groups, xbcw)
    assert window == ATTN_BLOCK and seq % ATTN_BLOCK == 0 and xbcw == width + 2 * groups * dstate

    o_cv, o_q, o_k, o_z, o_xbc = 0, 3 * cwid, 3 * cwid + aw, 3 * cwid + aw + 2 * kvw, 3 * cwid + aw + 2 * kvw + width
    o_dt = o_xbc + xbcw
    o_g = o_dt + heads
    n_in = w_in.shape[2]
    dt_pad = 2 * LANES - heads

    def regroup(w):
        return jnp.concatenate(
            [w[:, o_g:n_in], w[:, o_cv:o_q], w[:, o_xbc:o_dt], w[:, o_z:o_xbc], w[:, o_q:o_k], w[:, o_k:o_z],
             w[:, o_dt:o_g], jnp.zeros((d, dt_pad), w.dtype)], axis=1).astype(BF16)

    p_gate, p_conv, p_xbc, p_z = 0, 3 * d, 3 * d + 3 * cwid, 3 * d + 3 * cwid + xbcw
    p_q = p_z + width
    p_kv = p_q + aw
    p_dt = p_kv + 2 * kvw
    blk = lambda off, w: off // w
    assert all(off % w == 0 for off, w in ((p_conv, 3 * cwid), (p_xbc, xbcw), (p_z, width), (p_q, aw),
                                           (p_kv, 2 * kvw), (p_dt, LANES)))

    n_c = nbp + nbat
    c_rows = -(-n_c // SUBLANES) * SUBLANES
    c_all = jnp.pad(jnp.concatenate([c_prompt, c_sample], axis=0), ((0, c_rows - n_c), (0, 0)))
    mod_all = _modulation(c_all, w_mod, b_mod)

    xs_tm = x_sample.transpose(1, 0, 2).reshape(1, nt * nbat, d)
    srows = 2 * SUBLANES

    pos_p = jnp.arange(seq, dtype=jnp.int32)
    pos_s = PAST_LEN + jnp.arange(SUBLANES, dtype=jnp.int32)
    tab_p = _rope_tables(pos_p, head_dim)
    tab_s = _rope_tables(pos_s, head_dim)

    expand = jnp.repeat(jnp.eye(heads, dtype=F32), hdim, axis=1).astype(BF16)
    pad_h = lambda v: jnp.pad(v, (0, LANES - heads)).reshape(1, LANES)

    xp, xs = x_prompt, xs_tm
    outs = {k: [] for k in ("kp", "vp", "cp", "scp", "sp", "ks", "vs", "cs", "scs")}
    new_state_s = None
    for i in range(depth):
        w_in_i = regroup(w_in[i])
        wc, wa, ws, wo = (w[i].astype(BF16) for w in (w_br_conv, w_br_attn, w_br_ssm, w_o))
        mod_p = mod_all[i, :nbp].reshape(nbp, 1, 6 * d)
        mod_s = jnp.tile(mod_all[i, nbp:n_c], (nt, 1)).reshape(1, nt * nbat, 6 * d)
        ssm_params = (ssm_conv_w[i], ssm_conv_b[i].reshape(1, xbcw), pad_h(dt_bias[i]), pad_h(a_log[i]),
                      jnp.repeat(d_skip[i], hdim).reshape(1, width))
        nw_ssm = ssm_norm[i].reshape(1, width)

        proj = _inproj(xp, mod_p, norm_mix[i], w_in_i)
        y_attn, kv_last = _attn_prompt(proj, sinks[i], tab_p, blk(p_q, aw), blk(p_kv, 2 * kvw),
                                       n_heads, n_kv, head_dim)
        y_ssm, h_fin = _ssd_prompt(proj, (blk(p_xbc, xbcw), blk(p_z, width), blk(p_dt, LANES)),
                                   ssm_params + (nw_ssm, expand), dims)
        xp, conv_tail = _merge(xp, proj, blk(p_gate, 3 * d), (blk(p_conv, 3 * cwid), w_sconv[i]),
                               y_attn, y_ssm, mod_p, (wc, wa, ws, wo))
        outs["kp"].append(kv_last[:, :, :kvw].reshape(nbp, window, n_kv, head_dim))
        outs["vp"].append(kv_last[:, :, kvw:].reshape(nbp, window, n_kv, head_dim))
        outs["cp"].append(conv_tail[:, SUBLANES - (w_sconv.shape[1] - 1):])
        outs["scp"].append(proj[:, seq - (ssm_conv_w.shape[1] - 1):, p_xbc:p_xbc + xbcw])
        outs["sp"].append(h_fin.reshape(nbp, heads, hdim, dstate))

        proj_s = _inproj(xs, mod_s, norm_mix[i], w_in_i)[0]
        proj_tm = proj_s.reshape(nt, nbat, proj_s.shape[1])
        y_conv_s, u_s = _conv_sample(proj_s, blk(p_conv, 3 * cwid), state_conv[i].transpose(1, 0, 2),
                                     w_sconv[i], nt, nbat, cwid)
        y_attn_tm, k_new = _attn_sample(proj_tm, cache_k[i].reshape(nbat, window, kvw),
                                        cache_v[i].reshape(nbat, window, kvw), sinks[i], tab_s,
                                        blk(p_q, aw), blk(p_kv, 2 * kvw), n_heads, n_kv, head_dim)
        ypart, expa, xd_bm, b_bm, c_bm, cd = _ssd_sample_pre(
            proj_s, (blk(p_xbc, xbcw), blk(p_dt, LANES)), state_ssm_conv[i].transpose(1, 0, 2),
            ssm_params + (expand,), nt, nbat, dims, srows)
        yoff_bm, new_state_s = _ssd_sample_state(cd[:, :heads].reshape(nbat * heads), c_bm, b_bm, xd_bm,
                                                 state_ssm, i, new_state_s, dims)
        y_ssm_s = _ssd_sample_post(ypart, yoff_bm, expa, proj_s, blk(p_z, width), nw_ssm, groups)
        (xs,) = _merge(xs, proj_s[None], blk(p_gate, 3 * d), y_conv_s[None],
                       y_attn_tm.reshape(1, nt * nbat, aw), y_ssm_s[None], mod_s, (wc, wa, ws, wo))
        k_rows = k_new.transpose(1, 0, 2).reshape(nbat, nt, n_kv, head_dim)
        v_rows = proj_tm[:, :, p_kv + kvw:p_kv + 2 * kvw].transpose(1, 0, 2).reshape(nbat, nt, n_kv, head_dim)
        outs["ks"].append(jnp.concatenate([cache_k[i][:, nt:], k_rows], axis=1))
        outs["vs"].append(jnp.concatenate([cache_v[i][:, nt:], v_rows], axis=1))
        outs["cs"].append(u_s[nt - (w_sconv.shape[1] - 1):].transpose(1, 0, 2))
        kc = ssm_conv_w.shape[1] - 1
        outs["scs"].append(proj_tm[nt - kc:, :, p_xbc:p_xbc + xbcw].transpose(1, 0, 2))

        last = i == depth - 1
        jj = i // 2
        if i % 2 == 0:
            wg, wu, wd = ffn_w_gate[jj], ffn_w_up[jj], ffn_w_down[jj]
            xp = _ffn(xp, mod_p, norm_ffn[i], norm_final, wg, wu, wd, last)
            xs = _ffn(xs, mod_s, norm_ffn[i], norm_final, wg, wu, wd, last)
        else:
            wg, wu, wd = moe_w_gate[jj], moe_w_up[jj], moe_w_down[jj]
            wr = jnp.pad(router[jj], ((0, 0), (0, LANES - n_experts)))
            xp, xs = _moe([(xp, mod_p), (xs, mod_s)], norm_ffn[i], norm_final, wr, wg, wu, wd, last)

    y_sample = xs.reshape(nt, nbat, d).transpose(1, 0, 2)
    st = lambda k: jnp.stack(outs[k])
    return (xp, y_sample, st("kp"), st("vp"), st("cp"), st("scp"), st("sp"),
            st("ks"), st("vs"), st("cs"), st("scs"), new_state_s)
```

```python
import functools
import math

import jax
import jax.numpy as jnp
from jax import lax
from jax.experimental import pallas as pl
from jax.experimental.pallas import tpu as pltpu

F32 = jnp.float32
BF16 = jnp.bfloat16

PAST_LEN = 8192
ROPE_THETA = 500000.0
EPS = 1e-6
TOP_K = 2
SSM_GROUPS = 2
ATTN_BLOCK = 128
ATTN_BLOCKS_PER_STEP = 1
SSD_CHUNK = 128
MOE_ROW_TILE = 1024
ROUTE_TILE = 512

LANES = 128
SUBLANES = 8
VMEM_LIMIT_BYTES = 56 * 1024 * 1024


def _cparams(*semantics):
    return pltpu.CompilerParams(dimension_semantics=semantics, vmem_limit_bytes=VMEM_LIMIT_BYTES)


def _tile(n, pref):
    if n <= pref:
        return n
    t = pref
    while n % t:
        t //= 2
    return t


def _silu(x):
    return x / (1.0 + jnp.exp(-x))


def _sigmoid(x):
    return 1.0 / (1.0 + jnp.exp(-x))


def _softplus(x):
    return jnp.maximum(x, 0.0) + jnp.log1p(jnp.exp(-jnp.abs(x)))


def _dot(a, b):
    return jnp.dot(a, b, preferred_element_type=F32)


def _dot_nt(a, b):
    return lax.dot_general(a, b, (((1,), (1,)), ((), ())), preferred_element_type=F32)


def _dot_tn(a, b):
    return lax.dot_general(a, b, (((0,), (0,)), ((), ())), preferred_element_type=F32)


def _split3(x):
    hi = x.astype(BF16)
    r1 = x - hi.astype(F32)
    mid = r1.astype(BF16)
    lo = (r1 - mid.astype(F32)).astype(BF16)
    return hi, mid, lo


def _dot_exact_rhs01(x, m01):
    hi, mid, lo = _split3(x)
    return _dot(hi, m01) + _dot(mid, m01) + _dot(lo, m01)


def _dot_exact_lhs01(m01, x):
    hi, mid, lo = _split3(x)
    return _dot(m01, hi) + _dot(m01, mid) + _dot(m01, lo)


def _rms_mod(x, norm_w, scale, shift):
    xn = x * lax.rsqrt(jnp.mean(x * x, axis=-1, keepdims=True) + EPS)
    return (xn * norm_w) * (1.0 + scale) + shift


def _mod_kernel(c_ref, w_ref, b_ref, o_ref):
    a = _silu(c_ref[...]).astype(BF16)
    o_ref[0] = _dot(a, w_ref[0].astype(BF16)) + b_ref[0]


def _modulation(c_all, w_mod, b_mod):
    depth, d, n = w_mod.shape
    rows = c_all.shape[0]
    tn = _tile(n, 1024)
    return pl.pallas_call(
        _mod_kernel,
        grid=(depth, n // tn),
        in_specs=[
            pl.BlockSpec((rows, d), lambda i, j: (0, 0)),
            pl.BlockSpec((1, d, tn), lambda i, j: (i, 0, j)),
            pl.BlockSpec((1, 1, tn), lambda i, j: (i, 0, j)),
        ],
        out_specs=pl.BlockSpec((1, rows, tn), lambda i, j: (i, 0, j)),
        out_shape=jax.ShapeDtypeStruct((depth, rows, n), F32),
        compiler_params=_cparams("arbitrary", "arbitrary"),
        name="modulation",
    )(c_all, w_mod, b_mod.reshape(depth, 1, n))


def _mod_spec(mod, tm, chunk, d):
    if mod.shape[1] == 1:
        return pl.BlockSpec((1, 1, d), lambda b, i, *_: (b, 0, chunk))
    return pl.BlockSpec((1, tm, d), lambda b, i, *_: (b, i, chunk))


def _regroup_kernel(w_ref, o_ref, *, pieces, pad):
    dst = 0
    for lo, hi in pieces:
        o_ref[0, :, dst:dst + hi - lo] = w_ref[0, :, lo:hi].astype(BF16)
        dst += hi - lo
    o_ref[0, :, dst:dst + pad] = jnp.zeros((o_ref.shape[1], pad), BF16)


def _regroup_weights(w, pieces, pad):
    depth, d, n = w.shape
    assert sum(hi - lo for lo, hi in pieces) == n
    tr = _tile(d, 256)
    kern = functools.partial(_regroup_kernel, pieces=pieces, pad=pad)
    return pl.pallas_call(
        kern,
        grid=(depth, d // tr),
        in_specs=[pl.BlockSpec((1, tr, n), lambda i, r: (i, r, 0))],
        out_specs=pl.BlockSpec((1, tr, n + pad), lambda i, r: (i, r, 0)),
        out_shape=jax.ShapeDtypeStruct((depth, d, n + pad), BF16),
        compiler_params=_cparams("arbitrary", "arbitrary"),
        name="regroup_w_in",
    )(w)


def _inproj_kernel(x_ref, sh_ref, sc_ref, nw_ref, w_ref, o_ref, h_scr):
    @pl.when(pl.program_id(2) == 0)
    def _():
        h_scr[...] = _rms_mod(x_ref[0], nw_ref[...], sc_ref[0], sh_ref[0]).astype(BF16)

    o_ref[0] = _dot(h_scr[...], w_ref[...])


def _inproj(x, mod, norm_w, w):
    nb, l, d = x.shape
    n = w.shape[1]
    tm = _tile(l, 1024)
    tn = _tile(n, 2048)
    return pl.pallas_call(
        _inproj_kernel,
        grid=(nb, l // tm, n // tn),
        in_specs=[
            pl.BlockSpec((1, tm, d), lambda b, i, j: (b, i, 0)),
            _mod_spec(mod, tm, 0, d),
            _mod_spec(mod, tm, 1, d),
            pl.BlockSpec((1, d), lambda b, i, j: (0, 0)),
            pl.BlockSpec((d, tn), lambda b, i, j: (0, j)),
        ],
        out_specs=pl.BlockSpec((1, tm, tn), lambda b, i, j: (b, i, j)),
        out_shape=jax.ShapeDtypeStruct((nb, l, n), F32),
        scratch_shapes=[pltpu.VMEM((tm, d), BF16)],
        compiler_params=_cparams("arbitrary", "arbitrary", "arbitrary"),
        name="inproj",
    )(x, mod, mod, norm_w.reshape(1, d), w)


def _conv_sample_kernel(p_ref, st_ref, w_ref, y_ref, u_ref, *, nt, nbat, cw, k):
    w = w_ref[...]
    full = [st_ref[j] for j in range(k - 1)]
    gates = []
    for t in range(nt):
        p = p_ref[pl.ds(t * nbat, nbat), :]
        gates.append(p[:, :cw])
        u = p[:, cw:2 * cw] * p[:, 2 * cw:]
        u_ref[t] = u
        full.append(u)
    for t in range(nt):
        acc = w[0:1] * full[t]
        for j in range(1, k):
            acc = acc + w[j:j + 1] * full[t + j]
        y_ref[pl.ds(t * nbat, nbat), :] = gates[t] * acc


def _whole(shape):
    return pl.BlockSpec(shape, lambda i: (0,) * len(shape))


def _conv_sample(proj_s, col_block, state_tm, w, nt, nbat, cw):
    k = w.shape[0]
    rows = nt * nbat
    kern = functools.partial(_conv_sample_kernel, nt=nt, nbat=nbat, cw=cw, k=k)
    return pl.pallas_call(
        kern,
        grid=(1,),
        in_specs=[pl.BlockSpec((rows, 3 * cw), lambda i: (0, col_block)), _whole(state_tm.shape), _whole(w.shape)],
        out_specs=[_whole((rows, cw)), _whole((nt, nbat, cw))],
        out_shape=[
            jax.ShapeDtypeStruct((rows, cw), F32),
            jax.ShapeDtypeStruct((nt, nbat, cw), F32),
        ],
        compiler_params=_cparams("arbitrary"),
        name="conv_sample",
    )(proj_s, state_tm, w)


def _rope(x, cos, sin_lo, sin_hi, half_rot):
    return (x * cos + pltpu.roll(x, LANES - half_rot, 1) * sin_lo
            + pltpu.roll(x, half_rot, 1) * sin_hi)


def _attn_core(q, kcat, vcat, sinks_ref, valid, cos, sin_lo, sin_hi, *, n_heads, group, head_dim):
    tq = q.shape[0]
    half_rot = head_dim // 8
    heads_per_slab = LANES // head_dim
    scale = head_dim ** -0.5
    lane = lax.broadcasted_iota(jnp.int32, (tq, LANES), 1)
    k_bf = [kcat.astype(BF16), pltpu.roll(kcat, head_dim, 1).astype(BF16)]
    v_bf = [vcat.astype(BF16), pltpu.roll(vcat, head_dim, 1).astype(BF16)]
    slabs = []
    for s in range(n_heads // heads_per_slab):
        qs = _rope(q[:, s * LANES:(s + 1) * LANES], cos, sin_lo, sin_hi, half_rot)
        out = jnp.zeros((tq, LANES), F32)
        for half in range(heads_per_slab):
            h = s * heads_per_slab + half
            g = h // group
            in_head = (lane >= half * head_dim) & (lane < (half + 1) * head_dim)
            qm = jnp.where(in_head, qs, 0.0).astype(BF16)
            swap = 0 if (g % heads_per_slab) == half else 1
            sc = _dot_nt(qm, k_bf[swap]) * scale
            sc = jnp.where(valid, sc, -1e30)
            sink = sinks_ref[h]
            m = jnp.maximum(jnp.max(sc, axis=-1, keepdims=True), sink)
            p = jnp.exp(sc - m)
            p = p / (jnp.sum(p, axis=-1, keepdims=True) + jnp.exp(sink - m))
            o = _dot(p.astype(BF16), v_bf[swap])
            out = jnp.where(in_head, o, out)
        slabs.append(out)
    return slabs


def _attn_prompt_kernel(sinks_ref, q_ref, kv_ref, cos_ref, slo_ref, shi_ref, y_ref, last_ref, kprev, vprev,
                        *, n_heads, group, head_dim):
    j = pl.program_id(1)
    tq = kprev.shape[0]

    @pl.when(j == 0)
    def _():
        kprev[...] = jnp.zeros_like(kprev)
        vprev[...] = jnp.zeros_like(vprev)

    r = lax.broadcasted_iota(jnp.int32, (tq, 2 * tq), 0)
    c = lax.broadcasted_iota(jnp.int32, (tq, 2 * tq), 1)
    band = (c >= r) & (c <= r + tq)
    k_prev, v_prev = kprev[...], vprev[...]
    for blk in range(q_ref.shape[1] // tq):
        rows = pl.ds(blk * tq, tq)
        cos, slo, shi = cos_ref[rows, :], slo_ref[rows, :], shi_ref[rows, :]
        kv = kv_ref[0, rows, :]
        k_rot = _rope(kv[:, :LANES], cos, slo, shi, head_dim // 8)
        v = kv[:, LANES:]
        kcat = jnp.concatenate([k_prev, k_rot], axis=0)
        vcat = jnp.concatenate([v_prev, v], axis=0)
        if blk == 0:
            valid = band & (c >= jnp.where(j > 0, 0, tq))
        else:
            valid = band
        slabs = _attn_core(q_ref[0, rows, :], kcat, vcat, sinks_ref, valid, cos, slo, shi,
                           n_heads=n_heads, group=group, head_dim=head_dim)
        for s, o in enumerate(slabs):
            y_ref[0, rows, s * LANES:(s + 1) * LANES] = o
        k_prev, v_prev = k_rot, v
    kprev[...] = k_prev
    vprev[...] = v_prev
    last_ref[0, :, :LANES] = k_prev
    last_ref[0, :, LANES:] = v_prev


def _attn_prompt(proj, sinks, tables, q_block, kv_block, n_heads, n_kv, head_dim):
    nb, l, _ = proj.shape
    tq = ATTN_BLOCK
    qw = n_heads * head_dim
    kvw = 2 * n_kv * head_dim
    assert n_kv * head_dim == LANES
    kern = functools.partial(_attn_prompt_kernel, n_heads=n_heads, group=n_heads // n_kv, head_dim=head_dim)
    ts = _tile(l, ATTN_BLOCKS_PER_STEP * tq)
    tab_spec = pl.BlockSpec((ts, LANES), lambda b, j: (j, 0))
    return pl.pallas_call(
        kern,
        grid=(nb, l // ts),
        in_specs=[
            pl.BlockSpec(memory_space=pltpu.SMEM),
            pl.BlockSpec((1, ts, qw), lambda b, j: (b, j, q_block)),
            pl.BlockSpec((1, ts, kvw), lambda b, j: (b, j, kv_block)),
            tab_spec, tab_spec, tab_spec,
        ],
        out_specs=[
            pl.BlockSpec((1, ts, qw), lambda b, j: (b, j, 0)),
            pl.BlockSpec((1, tq, kvw), lambda b, j: (b, 0, 0)),
        ],
        out_shape=[
            jax.ShapeDtypeStruct((nb, l, qw), F32),
            jax.ShapeDtypeStruct((nb, tq, kvw), F32),
        ],
        scratch_shapes=[pltpu.VMEM((tq, LANES), F32), pltpu.VMEM((tq, LANES), F32)],
        compiler_params=_cparams("arbitrary", "arbitrary"),
        name="attn_prompt",
    )(sinks, proj, proj, *tables)


def _attn_sample_kernel(sinks_ref, q_ref, kv_ref, ck_ref, cv_ref, cos_ref, slo_ref, shi_ref, y_ref, knew_ref,
                        qh, kc, vc, ob, *, n_heads, group, head_dim, nt):
    gb, window = ck_ref.shape[0], ck_ref.shape[1]
    half_rot = head_dim // 8
    heads_per_slab = LANES // head_dim
    nq = n_heads * SUBLANES
    nk = kc.shape[1]
    lane = lax.broadcasted_iota(jnp.int32, (gb, LANES), 1)

    @pl.when(pl.program_id(0) == 0)
    def _():
        qh[...] = jnp.zeros_like(qh)
        kc[...] = jnp.zeros_like(kc)
        vc[...] = jnp.zeros_like(vc)

    kc[:, 0:window, :] = ck_ref[...]
    vc[:, 0:window, :] = cv_ref[...]
    for t in range(nt):
        cos, slo, shi = cos_ref[t:t + 1, :], slo_ref[t:t + 1, :], shi_ref[t:t + 1, :]
        kv = kv_ref[t]
        k_rot = _rope(kv[:, :LANES], cos, slo, shi, half_rot)
        knew_ref[t] = k_rot
        kc[:, window + t, :] = k_rot
        vc[:, window + t, :] = kv[:, LANES:]
        for s in range(n_heads // heads_per_slab):
            qs = _rope(q_ref[t][:, s * LANES:(s + 1) * LANES], cos, slo, shi, half_rot)
            qs_swapped = pltpu.roll(qs, head_dim, 1)
            for half in range(heads_per_slab):
                h = s * heads_per_slab + half
                g = (h // group) % heads_per_slab
                in_kv_half = (lane >= g * head_dim) & (lane < (g + 1) * head_dim)
                qh[:, h * SUBLANES + t, :] = jnp.where(in_kv_half, qs if g == half else qs_swapped, 0.0)

    sc = jnp.einsum("bqd,bkd->bqk", qh[...].astype(BF16), kc[...].astype(BF16),
                    preferred_element_type=F32) * (head_dim ** -0.5)
    r = lax.broadcasted_iota(jnp.int32, (nq, nk), 0) % SUBLANES
    c = lax.broadcasted_iota(jnp.int32, (nq, nk), 1)
    valid = (c >= r) & (c <= r + window)
    sc = jnp.where(valid[None], sc, -1e30)
    row_head = lax.broadcasted_iota(jnp.int32, (nq, 1), 0) // SUBLANES
    sink = jnp.zeros((nq, 1), F32)
    for h in range(n_heads):
        sink = jnp.where(row_head == h, sinks_ref[h], sink)
    m = jnp.maximum(jnp.max(sc, axis=-1, keepdims=True), sink[None])
    p = jnp.exp(sc - m)
    p = p / (jnp.sum(p, axis=-1, keepdims=True) + jnp.exp(sink[None] - m))
    ob[...] = jnp.einsum("bqk,bkd->bqd", p.astype(BF16), vc[...].astype(BF16), preferred_element_type=F32)

    for t in range(nt):
        for s in range(n_heads // heads_per_slab):
            out = jnp.zeros((gb, LANES), F32)
            for half in range(heads_per_slab):
                h = s * heads_per_slab + half
                g = (h // group) % heads_per_slab
                o = ob[:, h * SUBLANES + t, :]
                if g != half:
                    o = pltpu.roll(o, head_dim, 1)
                out = jnp.where((lane >= half * head_dim) & (lane < (half + 1) * head_dim), o, out)
            y_ref[t, :, s * LANES:(s + 1) * LANES] = out


def _attn_sample(proj_tm, ck, cv, sinks, tables, q_block, kv_block, n_heads, n_kv, head_dim):
    nt, nbat, _ = proj_tm.shape
    window = ck.shape[1]
    qw = n_heads * head_dim
    assert n_kv * head_dim == LANES and nt <= SUBLANES
    gb = _tile(nbat, 16)
    nk = window + 2 * SUBLANES
    kern = functools.partial(_attn_sample_kernel, n_heads=n_heads, group=n_heads // n_kv, head_dim=head_dim, nt=nt)
    tab_spec = pl.BlockSpec((SUBLANES, LANES), lambda b: (0, 0))
    return pl.pallas_call(
        kern,
        grid=(nbat // gb,),
        in_specs=[
            pl.BlockSpec(memory_space=pltpu.SMEM),
            pl.BlockSpec((nt, gb, qw), lambda b: (0, b, q_block)),
            pl.BlockSpec((nt, gb, 2 * LANES), lambda b: (0, b, kv_block)),
            pl.BlockSpec((gb, window, LANES), lambda b: (b, 0, 0)),
            pl.BlockSpec((gb, window, LANES), lambda b: (b, 0, 0)),
            tab_spec, tab_spec, tab_spec,
        ],
        out_specs=[
            pl.BlockSpec((nt, gb, qw), lambda b: (0, b, 0)),
            pl.BlockSpec((nt, gb, LANES), lambda b: (0, b, 0)),
        ],
        out_shape=[
            jax.ShapeDtypeStruct((nt, nbat, qw), F32),
            jax.ShapeDtypeStruct((nt, nbat, LANES), F32),
        ],
        scratch_shapes=[
            pltpu.VMEM((gb, n_heads * SUBLANES, LANES), F32),
            pltpu.VMEM((gb, nk, LANES), F32), pltpu.VMEM((gb, nk, LANES), F32),
            pltpu.VMEM((gb, n_heads * SUBLANES, LANES), F32),
        ],
        compiler_params=_cparams("arbitrary"),
        name="attn_sample",
    )(sinks, proj_tm, proj_tm, ck, cv, *tables)


def _rope_tables(pos, head_dim):
    rot = head_dim // 4
    half = rot // 2
    inv = jnp.exp(-(2.0 * jnp.arange(half, dtype=F32) / rot) * math.log(ROPE_THETA))
    ang = pos.astype(F32)[:, None] * inv[None, :]
    cos, sin = jnp.cos(ang), jnp.sin(ang)
    n = pos.shape[0]
    pad = jnp.zeros((n, head_dim - rot), F32)
    zeros = jnp.zeros((n, half), F32)
    cos_h = jnp.concatenate([cos, cos, pad + 1.0], axis=1)
    lo_h = jnp.concatenate([-sin, zeros, pad], axis=1)
    hi_h = jnp.concatenate([zeros, sin, pad], axis=1)
    reps = LANES // head_dim
    return tuple(jnp.tile(t, (1, reps)) for t in (cos_h, lo_h, hi_h))


def _gated_group_norm(y, z, norm_w, groups):
    y = y * _silu(z)
    gw = y.shape[1] // groups
    parts = []
    for g in range(groups):
        yg = y[:, g * gw:(g + 1) * gw]
        parts.append(yg * lax.rsqrt(jnp.mean(yg * yg, axis=-1, keepdims=True) + EPS))
    return jnp.concatenate(parts, axis=1) * norm_w


def _ssd_prompt_kernel(xbc_ref, z_ref, dt_ref, cw_ref, cb_ref, dtb_ref, alog_ref, dsk_ref, nw_ref, e_ref,
                       y_ref, hfin_ref, ext, ht, *, q, width, dstate, hdim, groups):
    j = pl.program_id(1)
    kconv = cw_ref.shape[0]

    @pl.when(j == 0)
    def _():
        ext[0:SUBLANES, :] = jnp.zeros((SUBLANES, ext.shape[1]), F32)
        ht[...] = jnp.zeros_like(ht)

    xbc = xbc_ref[0]
    ext[SUBLANES:, :] = xbc
    cw = cw_ref[...]
    conv = cw[kconv - 1:kconv] * xbc + cb_ref[...]
    for t in range(kconv - 1):
        conv = conv + cw[t:t + 1] * ext[pl.ds(SUBLANES - (kconv - 1) + t, q), :]
    ext[0:SUBLANES, :] = xbc[q - SUBLANES:, :]
    act = _silu(conv)
    xs = act[:, :width]
    bm = act[:, width:width + groups * dstate]
    cm = act[:, width + groups * dstate:]

    expand = e_ref[...]
    nh = expand.shape[0]
    dt_h = _softplus(dt_ref[0] + dtb_ref[...])
    a_h = dt_h * (-jnp.exp(alog_ref[...]))
    row = lax.broadcasted_iota(jnp.int32, (q, q), 0)
    col = lax.broadcasted_iota(jnp.int32, (q, q), 1)
    causal = col <= row
    tri = jnp.where(causal, 1.0, 0.0).astype(BF16)
    acum_h = _dot_exact_lhs01(tri, a_h)
    acum_ht = acum_h.T
    dt_x = _dot_exact_rhs01(dt_h[:, :nh], expand)
    acum_x = _dot_exact_rhs01(acum_h[:, :nh], expand)
    xdt = xs * dt_x
    acum_last = acum_x[q - 1:q, :]
    xd = xdt * jnp.exp(acum_last - acum_x)
    chunk_decay = jnp.exp(acum_last)
    exp_acum = jnp.exp(acum_x)

    lane = lax.broadcasted_iota(jnp.int32, (q, LANES), 1)
    gw = width // groups
    heads_per_group = gw // hdim
    pair = LANES // hdim
    y_parts = []
    for g in range(groups):
        b_g = bm[:, g * dstate:(g + 1) * dstate].astype(BF16)
        c_g = cm[:, g * dstate:(g + 1) * dstate].astype(BF16)
        cbm = _dot_nt(c_g, b_g)
        h_g = ht[:, g * gw:(g + 1) * gw]
        y_off = _dot(c_g, h_g.astype(BF16)) * exp_acum[:, g * gw:(g + 1) * gw]
        diag_parts = []
        for jp in range(heads_per_group // pair):
            l0 = g * gw + jp * LANES
            x_pair = xdt[:, l0:l0 + LANES].astype(BF16)
            out = jnp.zeros((q, LANES), F32)
            for half in range(pair):
                hd = g * heads_per_group + jp * pair + half
                decay = jnp.exp(jnp.where(causal, acum_h[:, hd:hd + 1] - acum_ht[hd:hd + 1, :], -jnp.inf))
                res = _dot((cbm * decay).astype(BF16), x_pair)
                in_head = (lane >= half * hdim) & (lane < (half + 1) * hdim)
                out = jnp.where(in_head, res, out)
            diag_parts.append(out)
        y_parts.append(jnp.concatenate(diag_parts, axis=1) + y_off)
        s_t = _dot_tn(b_g, xd[:, g * gw:(g + 1) * gw].astype(BF16))
        ht[:, g * gw:(g + 1) * gw] = h_g * chunk_decay[:, g * gw:(g + 1) * gw] + s_t
    y = jnp.concatenate(y_parts, axis=1) + xs * dsk_ref[...]
    y_ref[0] = _gated_group_norm(y, z_ref[0], nw_ref[...], groups)

    @pl.when(j == pl.num_programs(1) - 1)
    def _():
        hfin_ref[0] = ht[...].T


def _ssd_prompt(proj, blocks, params, dims):
    nb, l, _ = proj.shape
    q = SSD_CHUNK
    width, dstate, hdim, groups, xbcw = dims
    xbc_block, z_block, dt_block = blocks
    cw, cb, dtb, alog, dsk, nw, expand = params
    kern = functools.partial(_ssd_prompt_kernel, q=q, width=width, dstate=dstate, hdim=hdim, groups=groups)
    full = lambda a: pl.BlockSpec(a.shape, lambda b, j: (0,) * a.ndim)
    return pl.pallas_call(
        kern,
        grid=(nb, l // q),
        in_specs=[
            pl.BlockSpec((1, q, xbcw), lambda b, j: (b, j, xbc_block)),
            pl.BlockSpec((1, q, width), lambda b, j: (b, j, z_block)),
            pl.BlockSpec((1, q, LANES), lambda b, j: (b, j, dt_block)),
            full(cw), full(cb), full(dtb), full(alog), full(dsk), full(nw), full(expand),
        ],
        out_specs=[
            pl.BlockSpec((1, q, width), lambda b, j: (b, j, 0)),
            pl.BlockSpec((1, width, dstate), lambda b, j: (b, 0, 0)),
        ],
        out_shape=[
            jax.ShapeDtypeStruct((nb, l, width), F32),
            jax.ShapeDtypeStruct((nb, width, dstate), F32),
        ],
        scratch_shapes=[pltpu.VMEM((q + SUBLANES, xbcw), F32), pltpu.VMEM((dstate, width), F32)],
        compiler_params=_cparams("arbitrary", "arbitrary"),
        name="ssd_prompt",
    )(proj, proj, proj, cw, cb, dtb, alog, dsk, nw, expand)


def _ssd_sample_pre_kernel(xbc_ref, dt_ref, st_ref, cw_ref, cb_ref, dtb_ref, alog_ref, dsk_ref, e_ref,
                           ypart_ref, expa_ref, xd_ref, b_ref, c_ref, cd_ref,
                           *, nt, nbat, width, dstate, groups):
    kconv = cw_ref.shape[0]
    cw = cw_ref[...]
    expand = e_ref[...]
    nh = expand.shape[0]
    neg_a = -jnp.exp(alog_ref[...])
    full = [st_ref[t] for t in range(kconv - 1)]
    for t in range(nt):
        full.append(xbc_ref[pl.ds(t * nbat, nbat), :])
    xs, bm, cm, dt_x, acum_x, xdt = [], [], [], [], [], []
    acum_h = None
    for t in range(nt):
        conv = cb_ref[...] + cw[0:1] * full[t]
        for jj in range(1, kconv):
            conv = conv + cw[jj:jj + 1] * full[t + jj]
        act = _silu(conv)
        xs.append(act[:, :width])
        bm.append(act[:, width:width + groups * dstate])
        cm.append(act[:, width + groups * dstate:])
        dt_h = _softplus(dt_ref[pl.ds(t * nbat, nbat), :] + dtb_ref[...])
        a_h = dt_h * neg_a
        acum_h = a_h if acum_h is None else acum_h + a_h
        dt_x.append(_dot_exact_rhs01(dt_h[:, :nh], expand))
        acum_x.append(_dot_exact_rhs01(acum_h[:, :nh], expand))
        xdt.append(xs[t] * dt_x[t])
    cd_ref[...] = jnp.exp(acum_h)
    gw = width // groups
    for t in range(nt):
        y = xs[t] * dsk_ref[...]
        for s in range(t + 1):
            cb_parts = []
            for g in range(groups):
                prod = cm[t][:, g * dstate:(g + 1) * dstate] * bm[s][:, g * dstate:(g + 1) * dstate]
                cb_parts.append(jnp.broadcast_to(jnp.sum(prod, axis=-1, keepdims=True), (nbat, gw)))
            cb_x = jnp.concatenate(cb_parts, axis=1)
            y = y + cb_x * jnp.exp(acum_x[t] - acum_x[s]) * xdt[s]
        ypart_ref[t] = y
        expa_ref[t] = jnp.exp(acum_x[t])
    xd_ref[...] = jnp.zeros_like(xd_ref)
    b_ref[...] = jnp.zeros_like(b_ref)
    c_ref[...] = jnp.zeros_like(c_ref)
    for t in range(nt):
        xd_ref[:, t, :] = xdt[t] * jnp.exp(acum_x[nt - 1] - acum_x[t])
        b_ref[:, t, :] = bm[t]
        c_ref[:, t, :] = cm[t]


def _ssd_sample_pre(proj_s, blocks, state_tm, params, nt, nbat, dims, srows):
    width, dstate, hdim, groups, xbcw = dims
    xbc_block, dt_block = blocks
    cw, cb, dtb, alog, dsk, expand = params
    rows = nt * nbat
    kern = functools.partial(_ssd_sample_pre_kernel, nt=nt, nbat=nbat, width=width, dstate=dstate, groups=groups)
    sd = jax.ShapeDtypeStruct
    out_shapes = [(nt, nbat, width), (nt, nbat, width), (nbat, srows, width),
                  (nbat, srows, groups * dstate), (nbat, srows, groups * dstate), (nbat, LANES)]
    return pl.pallas_call(
        kern,
        grid=(1,),
        in_specs=[pl.BlockSpec((rows, xbcw), lambda i: (0, xbc_block)),
                  pl.BlockSpec((rows, LANES), lambda i: (0, dt_block)),
                  _whole(state_tm.shape)] + [_whole(a.shape) for a in params],
        out_specs=[_whole(s) for s in out_shapes],
        out_shape=[sd(s, F32) for s in out_shapes],
        compiler_params=_cparams("arbitrary"),
        name="ssd_sample_pre",
    )(proj_s, proj_s, state_tm, cw, cb, dtb, alog, dsk, expand)


def _ssd_sample_state_kernel(cd_ref, c_ref, b_ref, xd_ref, h0_ref, *rest, heads, hdim, dstate, groups):
    yoff_ref, hnew_ref = rest[-2:]
    gb = h0_ref.shape[1]
    hpg = heads // groups
    gw = hpg * hdim
    for i in range(gb):
        b = pl.program_id(0) * gb + i
        for g in range(groups):
            hm = h0_ref[0, i, g * hpg:(g + 1) * hpg].reshape(gw, dstate)
            c_g = c_ref[i, :, g * dstate:(g + 1) * dstate].astype(BF16)
            b_g = b_ref[i, :, g * dstate:(g + 1) * dstate].astype(BF16)
            yoff_ref[i, :, g * gw:(g + 1) * gw] = _dot_nt(c_g, hm.astype(BF16))
            upd = _dot_tn(xd_ref[i, :, g * gw:(g + 1) * gw].astype(BF16), b_g)
            for hh in range(hpg):
                hd = g * hpg + hh
                hnew_ref[0, i, hd] = (h0_ref[0, i, hd] * cd_ref[b * heads + hd]
                                      + upd[hh * hdim:(hh + 1) * hdim, :])
    for later in range(1, hnew_ref.shape[0]):
        hnew_ref[later] = jnp.zeros(hnew_ref.shape[1:], F32)


def _ssd_sample_state(cd_flat, c_bm, b_bm, xd_bm, state_all, layer, stacked, dims):
    width, dstate, hdim, groups, _ = dims
    depth, nbat, heads = state_all.shape[:3]
    rows = c_bm.shape[1]
    gb = _tile(nbat, 4)
    kern = functools.partial(_ssd_sample_state_kernel, heads=heads, hdim=hdim, dstate=dstate, groups=groups)
    in_specs = [
        pl.BlockSpec(memory_space=pltpu.SMEM),
        pl.BlockSpec((gb, rows, groups * dstate), lambda b: (b, 0, 0)),
        pl.BlockSpec((gb, rows, groups * dstate), lambda b: (b, 0, 0)),
        pl.BlockSpec((gb, rows, width), lambda b: (b, 0, 0)),
        pl.BlockSpec((1, gb, heads, hdim, dstate), lambda b: (layer, b, 0, 0, 0)),
    ]
    args = [cd_flat, c_bm, b_bm, xd_bm, state_all]
    if layer == 0:
        assert stacked is None
        state_spec = pl.BlockSpec((depth, gb, heads, hdim, dstate), lambda b: (0, b, 0, 0, 0))
        aliases = {}
    else:
        in_specs.append(pl.BlockSpec(memory_space=pl.ANY))
        args.append(stacked)
        state_spec = pl.BlockSpec((1, gb, heads, hdim, dstate), lambda b: (layer, b, 0, 0, 0))
        aliases = {len(args) - 1: 1}
    return pl.pallas_call(
        kern,
        grid=(nbat // gb,),
        in_specs=in_specs,
        out_specs=[pl.BlockSpec((gb, rows, width), lambda b: (b, 0, 0)), state_spec],
        out_shape=[
            jax.ShapeDtypeStruct((nbat, rows, width), F32),
            jax.ShapeDtypeStruct(state_all.shape, F32),
        ],
        input_output_aliases=aliases,
        compiler_params=_cparams("arbitrary"),
        name="ssd_sample_state",
    )(*args)


def _ssd_sample_post_kernel(ypart_ref, yoff_ref, expa_ref, z_ref, nw_ref, y_ref, *, groups, nt, nbat):
    for t in range(nt):
        rows = pl.ds(t * nbat, nbat)
        y = ypart_ref[t] + yoff_ref[:, t, :] * expa_ref[t]
        y_ref[rows, :] = _gated_group_norm(y, z_ref[rows, :], nw_ref[...], groups)


def _ssd_sample_post(ypart, yoff_bm, expa, proj_s, z_block, nw, groups):
    nt, nbat, width = ypart.shape
    rows = nt * nbat
    kern = functools.partial(_ssd_sample_post_kernel, groups=groups, nt=nt, nbat=nbat)
    return pl.pallas_call(
        kern,
        grid=(1,),
        in_specs=[_whole(ypart.shape), _whole(yoff_bm.shape), _whole(expa.shape),
                  pl.BlockSpec((rows, width), lambda i: (0, z_block)), _whole(nw.shape)],
        out_specs=_whole((rows, width)),
        out_shape=jax.ShapeDtypeStruct((rows, width), F32),
        compiler_params=_cparams("arbitrary"),
        name="ssd_sample_post",
    )(ypart, yoff_bm, expa, proj_s, nw)


def _merge_math(x, gates, y_conv, y_attn, y_ssm, gate1, wc_ref, wa_ref, ws_ref, wo_ref, d):
    merged = (_sigmoid(gates[:, :d]) * _dot(y_conv.astype(BF16), wc_ref[...])
              + _sigmoid(gates[:, d:2 * d]) * _dot(y_attn.astype(BF16), wa_ref[...])
              + _sigmoid(gates[:, 2 * d:]) * _dot(y_ssm.astype(BF16), ws_ref[...]))
    return x + gate1 * _dot(merged.astype(BF16), wo_ref[...])


def _merge_kernel(*refs, d, tm, cw, conv):
    refs = list(refs)
    x_ref, g_ref = refs[:2]
    del refs[:2]
    if conv:
        p_ref, cw_ref = refs[:2]
        del refs[:2]
    else:
        yc_ref = refs.pop(0)
    ya_ref, ys_ref, g1_ref, wc_ref, wa_ref, ws_ref, wo_ref = refs[:7]
    del refs[:7]
    o_ref = refs.pop(0)

    if conv:
        st_ref, ext = refs

        @pl.when(pl.program_id(1) == 0)
        def _():
            ext[0:SUBLANES, :] = jnp.zeros((SUBLANES, cw), F32)

        p = p_ref[0]
        u = p[:, cw:2 * cw] * p[:, 2 * cw:]
        ext[SUBLANES:, :] = u
        w = cw_ref[...]
        acc = w[0:1] * ext[pl.ds(SUBLANES - 2, tm), :] + w[1:2] * ext[pl.ds(SUBLANES - 1, tm), :] + w[2:3] * u
        tail = u[tm - SUBLANES:, :]
        ext[0:SUBLANES, :] = tail
        st_ref[0] = tail
        y_conv = p[:, :cw] * acc
    else:
        y_conv = yc_ref[0]

    o_ref[0] = _merge_math(x_ref[0], g_ref[0], y_conv, ya_ref[0], ys_ref[0], g1_ref[0],
                           wc_ref, wa_ref, ws_ref, wo_ref, d)


def _merge(x, proj, gate_block, conv_src, ya, ys, mod, weights):
    nb, l, d = x.shape
    tm = _tile(l, ROUTE_TILE)
    nt = l // tm
    conv = isinstance(conv_src, tuple)
    tok = lambda w: pl.BlockSpec((1, tm, w), lambda b, i: (b, i, 0))
    full = lambda a: pl.BlockSpec(a.shape, lambda b, i: (0, 0))
    in_specs = [tok(d), pl.BlockSpec((1, tm, 3 * d), lambda b, i: (b, i, gate_block))]
    args = [x, proj]
    out_specs = [tok(d)]
    out_shape = [jax.ShapeDtypeStruct((nb, l, d), F32)]
    scratch = []
    if conv:
        conv_block, w_conv = conv_src
        cw = w_conv.shape[1]
        assert w_conv.shape[0] == 3
        in_specs += [pl.BlockSpec((1, tm, 3 * cw), lambda b, i: (b, i, conv_block)), full(w_conv)]
        args += [proj, w_conv]
        out_specs.append(pl.BlockSpec((1, SUBLANES, cw), lambda b, i: (b, 0, 0)))
        out_shape.append(jax.ShapeDtypeStruct((nb, SUBLANES, cw), F32))
        scratch.append(pltpu.VMEM((tm + SUBLANES, cw), F32))
    else:
        cw = conv_src.shape[2]
        in_specs.append(tok(cw))
        args.append(conv_src)
    in_specs += [tok(ya.shape[2]), tok(ys.shape[2]), _mod_spec(mod, tm, 2, d)] + [full(w) for w in weights]
    args += [ya, ys, mod] + list(weights)
    kern = functools.partial(_merge_kernel, d=d, tm=tm, cw=cw, conv=conv)
    return pl.pallas_call(
        kern,
        grid=(nb, nt),
        in_specs=in_specs,
        out_specs=out_specs,
        out_shape=out_shape,
        scratch_shapes=scratch,
        compiler_params=_cparams("arbitrary", "arbitrary"),
        name="merge",
    )(*args)


def _finish(x, gate, f, nf_ref, final_norm):
    out = x + gate * f
    if final_norm:
        out = out * lax.rsqrt(jnp.mean(out * out, axis=-1, keepdims=True) + EPS) * nf_ref[...]
    return out


def _ffn_kernel(x_ref, sh_ref, sc_ref, g2_ref, nw_ref, nf_ref, wg_ref, wu_ref, wd_ref, o_ref, h_scr, acc,
                *, final_norm):
    f = pl.program_id(2)

    @pl.when(f == 0)
    def _():
        h_scr[...] = _rms_mod(x_ref[0], nw_ref[...], sc_ref[0], sh_ref[0]).astype(BF16)
        acc[...] = jnp.zeros_like(acc)

    h = h_scr[...]
    a = _silu(_dot(h, wg_ref[...].astype(BF16))) * _dot(h, wu_ref[...].astype(BF16))
    acc[...] += _dot(a.astype(BF16), wd_ref[...].astype(BF16))

    @pl.when(f == pl.num_programs(2) - 1)
    def _():
        o_ref[0] = _finish(x_ref[0], g2_ref[0], acc[...], nf_ref, final_norm)


def _ffn(x, mod, norm_w, norm_final, wg, wu, wd, final_norm):
    nb, l, d = x.shape
    ff = wg.shape[1]
    tm = _tile(l, 1024)
    tf = _tile(ff, 512)
    kern = functools.partial(_ffn_kernel, final_norm=final_norm)
    vec = pl.BlockSpec((1, d), lambda b, i, f: (0, 0))
    return pl.pallas_call(
        kern,
        grid=(nb, l // tm, ff // tf),
        in_specs=[
            pl.BlockSpec((1, tm, d), lambda b, i, f: (b, i, 0)),
            _mod_spec(mod, tm, 3, d), _mod_spec(mod, tm, 4, d), _mod_spec(mod, tm, 5, d),
            vec, vec,
            pl.BlockSpec((d, tf), lambda b, i, f: (0, f)),
            pl.BlockSpec((d, tf), lambda b, i, f: (0, f)),
            pl.BlockSpec((tf, d), lambda b, i, f: (f, 0)),
        ],
        out_specs=pl.BlockSpec((1, tm, d), lambda b, i, f: (b, i, 0)),
        out_shape=jax.ShapeDtypeStruct((nb, l, d), F32),
        scratch_shapes=[pltpu.VMEM((tm, d), BF16), pltpu.VMEM((tm, d), F32)],
        compiler_params=_cparams("arbitrary", "arbitrary", "arbitrary"),
        name="ffn",
    )(x, mod, mod, mod, norm_w.reshape(1, d), norm_final.reshape(1, d), wg, wu, wd)


def _route_math(x, norm_w, scale, shift, wr, n_experts):
    h = _rms_mod(x, norm_w, scale, shift)
    h_hi = h.astype(BF16)
    h_lo = (h - h_hi.astype(F32)).astype(BF16)
    r_hi = wr.astype(BF16)
    r_lo = (wr - r_hi.astype(F32)).astype(BF16)
    logits = _dot(h_hi, r_hi) + _dot(h_lo, r_hi) + _dot(h_hi, r_lo)
    lane = lax.broadcasted_iota(jnp.int32, logits.shape, 1).astype(F32)
    neg = -jnp.inf
    lg = jnp.where(lane < n_experts, logits, neg)
    m1 = jnp.max(lg, axis=-1, keepdims=True)
    i1 = jnp.min(jnp.where(lg == m1, lane, float(LANES)), axis=-1, keepdims=True)
    rest = jnp.where(lane == i1, neg, lg)
    m2 = jnp.max(rest, axis=-1, keepdims=True)
    i2 = jnp.min(jnp.where(rest == m2, lane, float(LANES)), axis=-1, keepdims=True)
    e2 = jnp.exp(m2 - m1)
    w1 = 1.0 / (1.0 + e2)
    w2 = e2 / (1.0 + e2)
    w_cols = jnp.where(lane == 0.0, w1, jnp.where(lane == 1.0, w2, 0.0))
    chosen = jnp.where(lane == 0.0, i1, jnp.where(lane == 1.0, i2, 0.0))
    return w_cols, chosen.T[:SUBLANES, :]


def _route_kernel(x_ref, sh_ref, sc_ref, nw_ref, wr_ref, w_ref, e_ref, *, n_experts):
    w_cols, e_rows = _route_math(x_ref[0], nw_ref[...], sc_ref[0], sh_ref[0], wr_ref[...], n_experts)
    w_ref[...] = w_cols
    e_ref[...] = e_rows


def _route(x, mod, norm_w, w_router_pad, n_experts):
    nb, l, d = x.shape
    tm = _tile(l, ROUTE_TILE)
    nt = l // tm
    kern = functools.partial(_route_kernel, n_experts=n_experts)
    return pl.pallas_call(
        kern,
        grid=(nb, nt),
        in_specs=[
            pl.BlockSpec((1, tm, d), lambda b, i: (b, i, 0)),
            _mod_spec(mod, tm, 3, d), _mod_spec(mod, tm, 4, d),
            pl.BlockSpec((1, d), lambda b, i: (0, 0)),
            pl.BlockSpec((d, LANES), lambda b, i: (0, 0)),
        ],
        out_specs=[
            pl.BlockSpec((tm, LANES), lambda b, i: (b * nt + i, 0)),
            pl.BlockSpec((SUBLANES, tm), lambda b, i: (0, b * nt + i)),
        ],
        out_shape=[
            jax.ShapeDtypeStruct((nb * l, LANES), F32),
            jax.ShapeDtypeStruct((SUBLANES, nb * l), F32),
        ],
        compiler_params=_cparams("arbitrary", "arbitrary"),
        name="moe_route",
    )(x, mod, mod, norm_w.reshape(1, d), w_router_pad)


def _plan_kernel(e_ref, pos_ref, te_ref, *, n_experts, tile, row_tile):
    steps = e_ref.shape[1] // tile
    sub = lax.broadcasted_iota(jnp.int32, (SUBLANES, tile), 0).astype(F32)
    sub_col = lax.broadcasted_iota(jnp.int32, (SUBLANES, 1), 0)

    def member(i):
        blk = e_ref[:, pl.ds(pl.multiple_of(i * tile, tile), tile)]
        e1, e2 = blk[0:1, :], blk[1:2, :]
        return e1, e2, jnp.where((sub == e1) | (sub == e2), 1.0, 0.0)

    def count_body(i, cnt):
        return cnt + jnp.sum(member(i)[2], axis=1, keepdims=True)

    cnt = lax.fori_loop(0, steps, count_body, jnp.zeros((SUBLANES, 1), F32))
    padded = jnp.floor((cnt + (row_tile - 1)) * (1.0 / row_tile)) * row_tile
    off = jnp.zeros((SUBLANES, 1), F32)
    run = jnp.zeros((1, 1), F32)
    for e in range(n_experts):
        off = jnp.where(sub_col == e, run, off)
        run = run + padded[e:e + 1, :]
    seg_end = off + padded

    r = lax.broadcasted_iota(jnp.int32, (tile, tile), 0)
    c = lax.broadcasted_iota(jnp.int32, (tile, tile), 1)
    before = jnp.where(r < c, 1.0, 0.0).astype(BF16)

    def pos_body(i, carry):
        e1, e2, m = member(i)
        val = off + carry + _dot(m.astype(BF16), before)
        p1 = jnp.sum(jnp.where(sub == e1, val, 0.0), axis=0, keepdims=True)
        p2 = jnp.sum(jnp.where(sub == e2, val, 0.0), axis=0, keepdims=True)
        rows = jnp.where(sub == 0.0, p1, jnp.where(sub == 1.0, p2, 0.0))
        pos_ref[:, pl.ds(pl.multiple_of(i * tile, tile), tile)] = rows.astype(jnp.int32)
        return carry + jnp.sum(m, axis=1, keepdims=True)

    lax.fori_loop(0, steps, pos_body, jnp.zeros((SUBLANES, 1), F32))

    sub_l = lax.broadcasted_iota(jnp.int32, (SUBLANES, LANES), 0)
    start = lax.broadcasted_iota(jnp.int32, (SUBLANES, LANES), 1).astype(F32) * row_tile
    owner = jnp.sum(jnp.where((seg_end <= start) & (sub_l < n_experts), 1.0, 0.0), axis=0, keepdims=True)
    owner = jnp.minimum(owner, n_experts - 1.0)
    used = run * (1.0 / row_tile)
    te_ref[...] = jnp.where(sub_l == 0, owner, jnp.where(sub_l == 1, used, 0.0)).astype(jnp.int32)


def _plan(e_all, n_experts, row_tile):
    t = e_all.shape[1]
    assert t % LANES == 0 and n_experts <= SUBLANES
    tile = _tile(t, ROUTE_TILE)
    kern = functools.partial(_plan_kernel, n_experts=n_experts, tile=tile, row_tile=row_tile)
    return pl.pallas_call(
        kern,
        out_shape=[
            jax.ShapeDtypeStruct((SUBLANES, t), jnp.int32),
            jax.ShapeDtypeStruct((SUBLANES, LANES), jnp.int32),
        ],
        compiler_params=pltpu.CompilerParams(vmem_limit_bytes=VMEM_LIMIT_BYTES),
        name="moe_plan",
    )(e_all)


def _scatter_kernel(p1_ref, p2_ref, x_ref, sh_ref, sc_ref, nw_ref, xs_in_ref, xs_ref, h_scr, sem, *, tm):
    del xs_in_ref
    h_scr[...] = _rms_mod(x_ref[0], nw_ref[...], sc_ref[0], sh_ref[0]).reshape(h_scr.shape)

    def row_copy(i, k, p):
        return pltpu.make_async_copy(h_scr.at[i, pl.ds(k, 1), :], xs_ref.at[pl.ds(p, 1), :], sem)

    def issue(i, carry):
        for k in range(SUBLANES):
            t = i * SUBLANES + k
            row_copy(i, k, p1_ref[t]).start(priority=0)
            row_copy(i, k, p2_ref[t]).start(priority=1)
        return carry

    def drain(t, carry):
        row_copy(0, 0, 0).wait()
        row_copy(0, 0, 0).wait()
        return carry

    lax.fori_loop(0, tm // SUBLANES, issue, 0)
    lax.fori_loop(0, tm, drain, 0, unroll=SUBLANES)


def _scatter(x, mod, norm_w, p1, p2, xs_sorted):
    nb, l, d = x.shape
    tm = _tile(l, ROUTE_TILE)
    nt = l // tm
    kern = functools.partial(_scatter_kernel, tm=tm)
    idx = pl.BlockSpec((tm,), lambda b, i: (b * nt + i,), memory_space=pltpu.SMEM)
    return pl.pallas_call(
        kern,
        grid=(nb, nt),
        in_specs=[
            idx, idx,
            pl.BlockSpec((1, tm, d), lambda b, i: (b, i, 0)),
            _mod_spec(mod, tm, 3, d), _mod_spec(mod, tm, 4, d),
            pl.BlockSpec((1, d), lambda b, i: (0, 0)),
            pl.BlockSpec(memory_space=pl.ANY),
        ],
        out_specs=pl.BlockSpec(memory_space=pl.ANY),
        out_shape=jax.ShapeDtypeStruct(xs_sorted.shape, F32),
        scratch_shapes=[pltpu.VMEM((tm // SUBLANES, SUBLANES, d), F32), pltpu.SemaphoreType.DMA],
        input_output_aliases={6: 0},
        compiler_params=_cparams("arbitrary", "arbitrary"),
        name="moe_scatter",
    )(p1, p2, x, mod, mod, norm_w.reshape(1, d), xs_sorted)


def _group_ffn_kernel(te_ref, xs_ref, wg_ref, wu_ref, wd_ref, y_ref, h_scr, acc):
    j = pl.program_id(0)
    f = pl.program_id(1)

    @pl.when(j < te_ref[LANES])
    def _():
        @pl.when(f == 0)
        def _():
            h_scr[...] = xs_ref[...].astype(BF16)
            acc[...] = jnp.zeros_like(acc)

        h = h_scr[...]
        a = _silu(_dot(h, wg_ref[0].astype(BF16))) * _dot(h, wu_ref[0].astype(BF16))
        acc[...] += _dot(a.astype(BF16), wd_ref[0].astype(BF16))

        @pl.when(f == pl.num_programs(1) - 1)
        def _():
            y_ref[...] = acc[...]

    @pl.when((j >= te_ref[LANES]) & (f == 0))
    def _():
        y_ref[...] = jnp.zeros_like(y_ref)


def _group_ffn(te_flat, xs_sorted, wg, wu, wd, row_tile):
    rows, d = xs_sorted.shape
    ff = wg.shape[2]
    tf = _tile(ff, 512)
    nf = ff // tf

    def tile_of(j, te):
        return jnp.minimum(j, te[LANES] - 1)

    def f_of(j, f, te):
        return jnp.where(j < te[LANES], f, nf - 1)

    grid_spec = pltpu.PrefetchScalarGridSpec(
        num_scalar_prefetch=1,
        grid=(rows // row_tile, nf),
        in_specs=[
            pl.BlockSpec((row_tile, d), lambda j, f, te: (tile_of(j, te), 0)),
            pl.BlockSpec((1, d, tf), lambda j, f, te: (te[tile_of(j, te)], 0, f_of(j, f, te))),
            pl.BlockSpec((1, d, tf), lambda j, f, te: (te[tile_of(j, te)], 0, f_of(j, f, te))),
            pl.BlockSpec((1, tf, d), lambda j, f, te: (te[tile_of(j, te)], f_of(j, f, te), 0)),
        ],
        out_specs=pl.BlockSpec((row_tile, d), lambda j, f, te: (j, 0)),
        scratch_shapes=[pltpu.VMEM((row_tile, d), BF16), pltpu.VMEM((row_tile, d), F32)],
    )
    return pl.pallas_call(
        _group_ffn_kernel,
        grid_spec=grid_spec,
        out_shape=jax.ShapeDtypeStruct((rows, d), F32),
        compiler_params=_cparams("arbitrary", "arbitrary"),
        name="moe_group_ffn",
    )(te_flat, xs_sorted, wg, wu, wd)


def _combine_kernel(p1_ref, p2_ref, x_ref, g2_ref, w_ref, nf_ref, y_hbm, o_ref, buf, sems,
                    *, tm, nt, n_steps, final_norm):
    step = pl.program_id(0) * nt + pl.program_id(1)

    def row_copy(slot, choice, i, k, p):
        return pltpu.make_async_copy(y_hbm.at[pl.ds(p, 1), :], buf.at[slot, choice, i, pl.ds(k, 1), :],
                                     sems.at[slot])

    def start_tile(tile, slot):
        def issue(i, carry):
            for k in range(SUBLANES):
                t = tile * tm + i * SUBLANES + k
                row_copy(slot, 0, i, k, p1_ref[t]).start(priority=0)
                row_copy(slot, 1, i, k, p2_ref[t]).start(priority=1)
            return carry
        lax.fori_loop(0, tm // SUBLANES, issue, 0)

    def wait_tile(slot):
        def drain(t, carry):
            row_copy(slot, 0, 0, 0, 0).wait()
            row_copy(slot, 0, 0, 0, 0).wait()
            return carry
        lax.fori_loop(0, tm, drain, 0, unroll=SUBLANES)

    @pl.when(step == 0)
    def _():
        start_tile(0, 0)

    for slot in range(2):
        @pl.when(step % 2 == slot)
        def _(slot=slot):
            @pl.when(step + 1 < n_steps)
            def _():
                start_tile(step + 1, 1 - slot)

            wait_tile(slot)
            w = w_ref[...]
            d = buf.shape[-1]
            f = w[:, 0:1] * buf[slot, 0].reshape(tm, d) + w[:, 1:2] * buf[slot, 1].reshape(tm, d)
            o_ref[0] = _finish(x_ref[0], g2_ref[0], f, nf_ref, final_norm)


def _combine(x, mod, w_cols, p1, p2, y_sorted, norm_final, final_norm):
    nb, l, d = x.shape
    tm = _tile(l, ROUTE_TILE)
    nt = l // tm
    kern = functools.partial(_combine_kernel, tm=tm, nt=nt, n_steps=nb * nt, final_norm=final_norm)
    grid_spec = pltpu.PrefetchScalarGridSpec(
        num_scalar_prefetch=2,
        grid=(nb, nt),
        in_specs=[
            pl.BlockSpec((1, tm, d), lambda b, i, *_: (b, i, 0)),
            _mod_spec(mod, tm, 5, d),
            pl.BlockSpec((tm, LANES), lambda b, i, *_: (b * nt + i, 0)),
            pl.BlockSpec((1, d), lambda b, i, *_: (0, 0)),
            pl.BlockSpec(memory_space=pl.ANY),
        ],
        out_specs=pl.BlockSpec((1, tm, d), lambda b, i, *_: (b, i, 0)),
        scratch_shapes=[pltpu.VMEM((2, 2, tm // SUBLANES, SUBLANES, d), F32), pltpu.SemaphoreType.DMA((2,))],
    )
    return pl.pallas_call(
        kern,
        grid_spec=grid_spec,
        out_shape=jax.ShapeDtypeStruct((nb, l, d), F32),
        compiler_params=_cparams("arbitrary", "arbitrary"),
        name="moe_combine",
    )(p1, p2, x, mod, w_cols, norm_final.reshape(1, d), y_sorted)


def _moe(groups, norm_w, norm_final, w_router_pad, wg, wu, wd, final_norm):
    n_experts = wg.shape[0]
    d = groups[0][0].shape[2]
    routed = [_route(x, mod, norm_w, w_router_pad, n_experts) for x, mod in groups]
    e_all = jnp.concatenate([e for _, e in routed], axis=1)
    t = e_all.shape[1]
    pos, te = _plan(e_all, n_experts, MOE_ROW_TILE)
    te_flat = te.reshape(SUBLANES * LANES)
    n_tiles = -(-TOP_K * t // MOE_ROW_TILE) + n_experts
    assert n_tiles <= LANES
    xs_sorted = jnp.zeros((n_tiles * MOE_ROW_TILE, d), F32)
    spans, start = [], 0
    for x, _ in groups:
        n = x.shape[0] * x.shape[1]
        spans.append((start, start + n))
        start += n
    for (x, mod), (lo, hi) in zip(groups, spans):
        xs_sorted = _scatter(x, mod, norm_w, pos[0, lo:hi], pos[1, lo:hi], xs_sorted)
    y_sorted = _group_ffn(te_flat, xs_sorted, wg, wu, wd, MOE_ROW_TILE)
    return [_combine(x, mod, w_cols, pos[0, lo:hi], pos[1, lo:hi], y_sorted, norm_final, final_norm)
            for (x, mod), (w_cols, _), (lo, hi) in zip(groups, routed, spans)]


def kernel(x_prompt, x_sample, c_prompt, c_sample, cache_k, cache_v, state_conv, state_ssm_conv, state_ssm,
           w_mod, b_mod, norm_mix, norm_ffn, norm_final, w_in, w_sconv, sinks, ssm_conv_w, ssm_conv_b,
           dt_bias, a_log, d_skip, ssm_norm, w_br_conv, w_br_attn, w_br_ssm, w_o,
           ffn_w_gate, ffn_w_up, ffn_w_down, router, moe_w_gate, moe_w_up, moe_w_down):
    nbp, seq, d = x_prompt.shape
    nbat, nt, _ = x_sample.shape
    depth = w_mod.shape[0]
    cwid = w_sconv.shape[2]
    n_heads = sinks.shape[1]
    window, n_kv, head_dim = cache_k.shape[2:]
    heads, hdim, dstate = state_ssm.shape[2:]
    width = heads * hdim
    xbcw = ssm_conv_w.shape[2]
    groups = SSM_GROUPS
    n_experts = router.shape[2]
    aw = n_heads * head_dim
    kvw = n_kv * head_dim
    dims = (width, dstate, hdim, groups, xbcw)
    assert window == ATTN_BLOCK and seq % ATTN_BLOCK == 0 and xbcw == width + 2 * groups * dstate

    o_cv, o_q, o_k, o_z, o_xbc = 0, 3 * cwid, 3 * cwid + aw, 3 * cwid + aw + 2 * kvw, 3 * cwid + aw + 2 * kvw + width
    o_dt = o_xbc + xbcw
    o_g = o_dt + heads
    n_in = w_in.shape[2]
    dt_pad = 2 * LANES - heads

    pieces = ((o_g, n_in), (o_cv, o_q), (o_xbc, o_dt), (o_z, o_xbc), (o_q, o_k), (o_k, o_z), (o_dt, o_g))
    w_in_regrouped = _regroup_weights(w_in, pieces, dt_pad)

    p_gate, p_conv, p_xbc, p_z = 0, 3 * d, 3 * d + 3 * cwid, 3 * d + 3 * cwid + xbcw
    p_q = p_z + width
    p_kv = p_q + aw
    p_dt = p_kv + 2 * kvw
    blk = lambda off, w: off // w
    assert all(off % w == 0 for off, w in ((p_conv, 3 * cwid), (p_xbc, xbcw), (p_z, width), (p_q, aw),
                                           (p_kv, 2 * kvw), (p_dt, LANES)))

    n_c = nbp + nbat
    c_rows = -(-n_c // SUBLANES) * SUBLANES
    c_all = jnp.pad(jnp.concatenate([c_prompt, c_sample], axis=0), ((0, c_rows - n_c), (0, 0)))
    mod_all = _modulation(c_all, w_mod, b_mod)

    xs_tm = x_sample.transpose(1, 0, 2).reshape(1, nt * nbat, d)
    srows = 2 * SUBLANES

    pos_p = jnp.arange(seq, dtype=jnp.int32)
    pos_s = PAST_LEN + jnp.arange(SUBLANES, dtype=jnp.int32)
    tab_p = _rope_tables(pos_p, head_dim)
    tab_s = _rope_tables(pos_s, head_dim)

    expand = jnp.repeat(jnp.eye(heads, dtype=F32), hdim, axis=1).astype(BF16)
    pad_h = lambda v: jnp.pad(v, (0, LANES - heads)).reshape(1, LANES)

    xp, xs = x_prompt, xs_tm
    outs = {k: [] for k in ("kp", "vp", "cp", "scp", "sp", "ks", "vs", "cs", "scs")}
    new_state_s = None
    for i in range(depth):
        w_in_i = w_in_regrouped[i]
        wc, wa, ws, wo = (w[i].astype(BF16) for w in (w_br_conv, w_br_attn, w_br_ssm, w_o))
        mod_p = mod_all[i, :nbp].reshape(nbp, 1, 6 * d)
        mod_s = jnp.tile(mod_all[i, nbp:n_c], (nt, 1)).reshape(1, nt * nbat, 6 * d)
        ssm_params = (ssm_conv_w[i], ssm_conv_b[i].reshape(1, xbcw), pad_h(dt_bias[i]), pad_h(a_log[i]),
                      jnp.repeat(d_skip[i], hdim).reshape(1, width))
        nw_ssm = ssm_norm[i].reshape(1, width)

        proj = _inproj(xp, mod_p, norm_mix[i], w_in_i)
        y_attn, kv_last = _attn_prompt(proj, sinks[i], tab_p, blk(p_q, aw), blk(p_kv, 2 * kvw),
                                       n_heads, n_kv, head_dim)
        y_ssm, h_fin = _ssd_prompt(proj, (blk(p_xbc, xbcw), blk(p_z, width), blk(p_dt, LANES)),
                                   ssm_params + (nw_ssm, expand), dims)
        xp, conv_tail = _merge(xp, proj, blk(p_gate, 3 * d), (blk(p_conv, 3 * cwid), w_sconv[i]),
                               y_attn, y_ssm, mod_p, (wc, wa, ws, wo))
        outs["kp"].append(kv_last[:, :, :kvw].reshape(nbp, window, n_kv, head_dim))
        outs["vp"].append(kv_last[:, :, kvw:].reshape(nbp, window, n_kv, head_dim))
        outs["cp"].append(conv_tail[:, SUBLANES - (w_sconv.shape[1] - 1):])
        outs["scp"].append(proj[:, seq - (ssm_conv_w.shape[1] - 1):, p_xbc:p_xbc + xbcw])
        outs["sp"].append(h_fin.reshape(nbp, heads, hdim, dstate))

        proj_s = _inproj(xs, mod_s, norm_mix[i], w_in_i)[0]
        proj_tm = proj_s.reshape(nt, nbat, proj_s.shape[1])
        y_conv_s, u_s = _conv_sample(proj_s, blk(p_conv, 3 * cwid), state_conv[i].transpose(1, 0, 2),
                                     w_sconv[i], nt, nbat, cwid)
        y_attn_tm, k_new = _attn_sample(proj_tm, cache_k[i].reshape(nbat, window, kvw),
                                        cache_v[i].reshape(nbat, window, kvw), sinks[i], tab_s,
                                        blk(p_q, aw), blk(p_kv, 2 * kvw), n_heads, n_kv, head_dim)
        ypart, expa, xd_bm, b_bm, c_bm, cd = _ssd_sample_pre(
            proj_s, (blk(p_xbc, xbcw), blk(p_dt, LANES)), state_ssm_conv[i].transpose(1, 0, 2),
            ssm_params + (expand,), nt, nbat, dims, srows)
        yoff_bm, new_state_s = _ssd_sample_state(cd[:, :heads].reshape(nbat * heads), c_bm, b_bm, xd_bm,
                                                 state_ssm, i, new_state_s, dims)
        y_ssm_s = _ssd_sample_post(ypart, yoff_bm, expa, proj_s, blk(p_z, width), nw_ssm, groups)
        (xs,) = _merge(xs, proj_s[None], blk(p_gate, 3 * d), y_conv_s[None],
                       y_attn_tm.reshape(1, nt * nbat, aw), y_ssm_s[None], mod_s, (wc, wa, ws, wo))
        k_rows = k_new.transpose(1, 0, 2).reshape(nbat, nt, n_kv, head_dim)
        v_rows = proj_tm[:, :, p_kv + kvw:p_kv + 2 * kvw].transpose(1, 0, 2).reshape(nbat, nt, n_kv, head_dim)
        outs["ks"].append(jnp.concatenate([cache_k[i][:, nt:], k_rows], axis=1))
        outs["vs"].append(jnp.concatenate([cache_v[i][:, nt:], v_rows], axis=1))
        outs["cs"].append(u_s[nt - (w_sconv.shape[1] - 1):].transpose(1, 0, 2))
        kc = ssm_conv_w.shape[1] - 1
        outs["scs"].append(proj_tm[nt - kc:, :, p_xbc:p_xbc + xbcw].transpose(1, 0, 2))

        last = i == depth - 1
        jj = i // 2
        if i % 2 == 0:
            wg, wu, wd = ffn_w_gate[jj], ffn_w_up[jj], ffn_w_down[jj]
            xp = _ffn(xp, mod_p, norm_ffn[i], norm_final, wg, wu, wd, last)
            xs = _ffn(xs, mod_s, norm_ffn[i], norm_final, wg, wu, wd, last)
        else:
            wg, wu, wd = moe_w_gate[jj], moe_w_up[jj], moe_w_down[jj]
            wr = jnp.pad(router[jj], ((0, 0), (0, LANES - n_experts)))
            xp, xs = _moe([(xp, mod_p), (xs, mod_s)], norm_ffn[i], norm_final, wr, wg, wu, wd, last)

    y_sample = xs.reshape(nt, nbat, d).transpose(1, 0, 2)
    st = lambda k: jnp.stack(outs[k])
    return (xp, y_sample, st("kp"), st("vp"), st("cp"), st("scp"), st("sp"),
            st("ks"), st("vs"), st("cs"), st("scs"), new_state_s)
```

```python
import functools
import math

import jax
import jax.numpy as jnp
from jax import lax
from jax.experimental import pallas as pl
from jax.experimental.pallas import tpu as pltpu

F32 = jnp.float32
BF16 = jnp.bfloat16

PAST_LEN = 8192
ROPE_THETA = 500000.0
EPS = 1e-6
TOP_K = 2
SSM_GROUPS = 2
ATTN_BLOCK = 128
ATTN_BLOCKS_PER_STEP = 1
SSD_CHUNK = 128
MOE_ROW_TILE = 1024
ROUTE_TILE = 512

LANES = 128
SUBLANES = 8
VMEM_LIMIT_BYTES = 56 * 1024 * 1024


def _cparams(*semantics):
    return pltpu.CompilerParams(dimension_semantics=semantics, vmem_limit_bytes=VMEM_LIMIT_BYTES)


def _tile(n, pref):
    if n <= pref:
        return n
    t = pref
    while n % t:
        t //= 2
    return t


def _silu(x):
    return x / (1.0 + jnp.exp(-x))


def _sigmoid(x):
    return 1.0 / (1.0 + jnp.exp(-x))


def _softplus(x):
    return jnp.maximum(x, 0.0) + jnp.log1p(jnp.exp(-jnp.abs(x)))


def _dot(a, b):
    return jnp.dot(a, b, preferred_element_type=F32)


def _dot_nt(a, b):
    return lax.dot_general(a, b, (((1,), (1,)), ((), ())), preferred_element_type=F32)


def _dot_tn(a, b):
    return lax.dot_general(a, b, (((0,), (0,)), ((), ())), preferred_element_type=F32)


def _split3(x):
    hi = x.astype(BF16)
    r1 = x - hi.astype(F32)
    mid = r1.astype(BF16)
    lo = (r1 - mid.astype(F32)).astype(BF16)
    return hi, mid, lo


def _dot_exact_rhs01(x, m01):
    hi, mid, lo = _split3(x)
    return _dot(hi, m01) + _dot(mid, m01) + _dot(lo, m01)


def _dot_exact_lhs01(m01, x):
    hi, mid, lo = _split3(x)
    return _dot(m01, hi) + _dot(m01, mid) + _dot(m01, lo)


def _rms_mod(x, norm_w, scale, shift):
    xn = x * lax.rsqrt(jnp.mean(x * x, axis=-1, keepdims=True) + EPS)
    return (xn * norm_w) * (1.0 + scale) + shift


def _mod_kernel(c_ref, w_ref, b_ref, o_ref):
    a = _silu(c_ref[...]).astype(BF16)
    o_ref[0] = _dot(a, w_ref[0].astype(BF16)) + b_ref[0]


def _modulation(c_all, w_mod, b_mod):
    depth, d, n = w_mod.shape
    rows = c_all.shape[0]
    tn = _tile(n, 1024)
    return pl.pallas_call(
        _mod_kernel,
        grid=(depth, n // tn),
        in_specs=[
            pl.BlockSpec((rows, d), lambda i, j: (0, 0)),
            pl.BlockSpec((1, d, tn), lambda i, j: (i, 0, j)),
            pl.BlockSpec((1, 1, tn), lambda i, j: (i, 0, j)),
        ],
        out_specs=pl.BlockSpec((1, rows, tn), lambda i, j: (i, 0, j)),
        out_shape=jax.ShapeDtypeStruct((depth, rows, n), F32),
        compiler_params=_cparams("arbitrary", "arbitrary"),
        name="modulation",
    )(c_all, w_mod, b_mod.reshape(depth, 1, n))


def _mod_spec(mod, tm, chunk, d):
    if mod.shape[1] == 1:
        return pl.BlockSpec((1, 1, d), lambda b, i, *_: (b, 0, chunk))
    return pl.BlockSpec((1, tm, d), lambda b, i, *_: (b, i, chunk))


def _inproj_kernel(x_ref, sh_ref, sc_ref, nw_ref, w_ref, o_ref, h_scr):
    @pl.when(pl.program_id(2) == 0)
    def _():
        h_scr[...] = _rms_mod(x_ref[0], nw_ref[...], sc_ref[0], sh_ref[0]).astype(BF16)

    o_ref[0] = _dot(h_scr[...], w_ref[...])


def _inproj(x, mod, norm_w, w):
    nb, l, d = x.shape
    n = w.shape[1]
    tm = _tile(l, 1024)
    tn = _tile(n, 2048)
    return pl.pallas_call(
        _inproj_kernel,
        grid=(nb, l // tm, n // tn),
        in_specs=[
            pl.BlockSpec((1, tm, d), lambda b, i, j: (b, i, 0)),
            _mod_spec(mod, tm, 0, d),
            _mod_spec(mod, tm, 1, d),
            pl.BlockSpec((1, d), lambda b, i, j: (0, 0)),
            pl.BlockSpec((d, tn), lambda b, i, j: (0, j)),
        ],
        out_specs=pl.BlockSpec((1, tm, tn), lambda b, i, j: (b, i, j)),
        out_shape=jax.ShapeDtypeStruct((nb, l, n), F32),
        scratch_shapes=[pltpu.VMEM((tm, d), BF16)],
        compiler_params=_cparams("arbitrary", "arbitrary", "arbitrary"),
        name="inproj",
    )(x, mod, mod, norm_w.reshape(1, d), w)


def _conv_sample_kernel(p_ref, st_ref, w_ref, y_ref, u_ref, *, nt, nbat, cw, k):
    w = w_ref[...]
    full = [st_ref[j] for j in range(k - 1)]
    gates = []
    for t in range(nt):
        p = p_ref[pl.ds(t * nbat, nbat), :]
        gates.append(p[:, :cw])
        u = p[:, cw:2 * cw] * p[:, 2 * cw:]
        u_ref[t] = u
        full.append(u)
    for t in range(nt):
        acc = w[0:1] * full[t]
        for j in range(1, k):
            acc = acc + w[j:j + 1] * full[t + j]
        y_ref[pl.ds(t * nbat, nbat), :] = gates[t] * acc


def _whole(shape):
    return pl.BlockSpec(shape, lambda i: (0,) * len(shape))


def _conv_sample(proj_s, col_block, state_tm, w, nt, nbat, cw):
    k = w.shape[0]
    rows = nt * nbat
    kern = functools.partial(_conv_sample_kernel, nt=nt, nbat=nbat, cw=cw, k=k)
    return pl.pallas_call(
        kern,
        grid=(1,),
        in_specs=[pl.BlockSpec((rows, 3 * cw), lambda i: (0, col_block)), _whole(state_tm.shape), _whole(w.shape)],
        out_specs=[_whole((rows, cw)), _whole((nt, nbat, cw))],
        out_shape=[
            jax.ShapeDtypeStruct((rows, cw), F32),
            jax.ShapeDtypeStruct((nt, nbat, cw), F32),
        ],
        compiler_params=_cparams("arbitrary"),
        name="conv_sample",
    )(proj_s, state_tm, w)


def _rope(x, cos, sin_lo, sin_hi, half_rot):
    return (x * cos + pltpu.roll(x, LANES - half_rot, 1) * sin_lo
            + pltpu.roll(x, half_rot, 1) * sin_hi)


def _attn_core(q, kcat, vcat, sinks_ref, valid, cos, sin_lo, sin_hi, *, n_heads, group, head_dim):
    tq = q.shape[0]
    half_rot = head_dim // 8
    heads_per_slab = LANES // head_dim
    scale = head_dim ** -0.5
    lane = lax.broadcasted_iota(jnp.int32, (tq, LANES), 1)
    k_bf = [kcat.astype(BF16), pltpu.roll(kcat, head_dim, 1).astype(BF16)]
    v_bf = [vcat.astype(BF16), pltpu.roll(vcat, head_dim, 1).astype(BF16)]
    slabs = []
    for s in range(n_heads // heads_per_slab):
        qs = _rope(q[:, s * LANES:(s + 1) * LANES], cos, sin_lo, sin_hi, half_rot)
        out = jnp.zeros((tq, LANES), F32)
        for half in range(heads_per_slab):
            h = s * heads_per_slab + half
            g = h // group
            in_head = (lane >= half * head_dim) & (lane < (half + 1) * head_dim)
            qm = jnp.where(in_head, qs, 0.0).astype(BF16)
            swap = 0 if (g % heads_per_slab) == half else 1
            sc = _dot_nt(qm, k_bf[swap]) * scale
            sc = jnp.where(valid, sc, -1e30)
            sink = sinks_ref[h]
            m = jnp.maximum(jnp.max(sc, axis=-1, keepdims=True), sink)
            p = jnp.exp(sc - m)
            p = p / (jnp.sum(p, axis=-1, keepdims=True) + jnp.exp(sink - m))
            o = _dot(p.astype(BF16), v_bf[swap])
            out = jnp.where(in_head, o, out)
        slabs.append(out)
    return slabs


def _attn_prompt_kernel(sinks_ref, q_ref, kv_ref, cos_ref, slo_ref, shi_ref, y_ref, last_ref, kprev, vprev,
                        *, n_heads, group, head_dim):
    j = pl.program_id(1)
    tq = kprev.shape[0]

    @pl.when(j == 0)
    def _():
        kprev[...] = jnp.zeros_like(kprev)
        vprev[...] = jnp.zeros_like(vprev)

    r = lax.broadcasted_iota(jnp.int32, (tq, 2 * tq), 0)
    c = lax.broadcasted_iota(jnp.int32, (tq, 2 * tq), 1)
    band = (c >= r) & (c <= r + tq)
    k_prev, v_prev = kprev[...], vprev[...]
    for blk in range(q_ref.shape[1] // tq):
        rows = pl.ds(blk * tq, tq)
        cos, slo, shi = cos_ref[rows, :], slo_ref[rows, :], shi_ref[rows, :]
        kv = kv_ref[0, rows, :]
        k_rot = _rope(kv[:, :LANES], cos, slo, shi, head_dim // 8)
        v = kv[:, LANES:]
        kcat = jnp.concatenate([k_prev, k_rot], axis=0)
        vcat = jnp.concatenate([v_prev, v], axis=0)
        if blk == 0:
            valid = band & (c >= jnp.where(j > 0, 0, tq))
        else:
            valid = band
        slabs = _attn_core(q_ref[0, rows, :], kcat, vcat, sinks_ref, valid, cos, slo, shi,
                           n_heads=n_heads, group=group, head_dim=head_dim)
        for s, o in enumerate(slabs):
            y_ref[0, rows, s * LANES:(s + 1) * LANES] = o
        k_prev, v_prev = k_rot, v
    kprev[...] = k_prev
    vprev[...] = v_prev
    last_ref[0, :, :LANES] = k_prev
    last_ref[0, :, LANES:] = v_prev


def _attn_prompt(proj, sinks, tables, q_block, kv_block, n_heads, n_kv, head_dim):
    nb, l, _ = proj.shape
    tq = ATTN_BLOCK
    qw = n_heads * head_dim
    kvw = 2 * n_kv * head_dim
    assert n_kv * head_dim == LANES
    kern = functools.partial(_attn_prompt_kernel, n_heads=n_heads, group=n_heads // n_kv, head_dim=head_dim)
    ts = _tile(l, ATTN_BLOCKS_PER_STEP * tq)
    tab_spec = pl.BlockSpec((ts, LANES), lambda b, j: (j, 0))
    return pl.pallas_call(
        kern,
        grid=(nb, l // ts),
        in_specs=[
            pl.BlockSpec(memory_space=pltpu.SMEM),
            pl.BlockSpec((1, ts, qw), lambda b, j: (b, j, q_block)),
            pl.BlockSpec((1, ts, kvw), lambda b, j: (b, j, kv_block)),
            tab_spec, tab_spec, tab_spec,
        ],
        out_specs=[
            pl.BlockSpec((1, ts, qw), lambda b, j: (b, j, 0)),
            pl.BlockSpec((1, tq, kvw), lambda b, j: (b, 0, 0)),
        ],
        out_shape=[
            jax.ShapeDtypeStruct((nb, l, qw), F32),
            jax.ShapeDtypeStruct((nb, tq, kvw), F32),
        ],
        scratch_shapes=[pltpu.VMEM((tq, LANES), F32), pltpu.VMEM((tq, LANES), F32)],
        compiler_params=_cparams("arbitrary", "arbitrary"),
        name="attn_prompt",
    )(sinks, proj, proj, *tables)


def _attn_sample_kernel(sinks_ref, q_ref, kv_ref, ck_ref, cv_ref, cos_ref, slo_ref, shi_ref, y_ref, knew_ref,
                        qh, kc, vc, ob, *, n_heads, group, head_dim, nt):
    gb, window = ck_ref.shape[0], ck_ref.shape[1]
    half_rot = head_dim // 8
    heads_per_slab = LANES // head_dim
    nq = n_heads * SUBLANES
    nk = kc.shape[1]
    lane = lax.broadcasted_iota(jnp.int32, (gb, LANES), 1)

    @pl.when(pl.program_id(0) == 0)
    def _():
        qh[...] = jnp.zeros_like(qh)
        kc[...] = jnp.zeros_like(kc)
        vc[...] = jnp.zeros_like(vc)

    kc[:, 0:window, :] = ck_ref[...]
    vc[:, 0:window, :] = cv_ref[...]
    for t in range(nt):
        cos, slo, shi = cos_ref[t:t + 1, :], slo_ref[t:t + 1, :], shi_ref[t:t + 1, :]
        kv = kv_ref[t]
        k_rot = _rope(kv[:, :LANES], cos, slo, shi, half_rot)
        knew_ref[t] = k_rot
        kc[:, window + t, :] = k_rot
        vc[:, window + t, :] = kv[:, LANES:]
        for s in range(n_heads // heads_per_slab):
            qs = _rope(q_ref[t][:, s * LANES:(s + 1) * LANES], cos, slo, shi, half_rot)
            qs_swapped = pltpu.roll(qs, head_dim, 1)
            for half in range(heads_per_slab):
                h = s * heads_per_slab + half
                g = (h // group) % heads_per_slab
                in_kv_half = (lane >= g * head_dim) & (lane < (g + 1) * head_dim)
                qh[:, h * SUBLANES + t, :] = jnp.where(in_kv_half, qs if g == half else qs_swapped, 0.0)

    sc = jnp.einsum("bqd,bkd->bqk", qh[...].astype(BF16), kc[...].astype(BF16),
                    preferred_element_type=F32) * (head_dim ** -0.5)
    r = lax.broadcasted_iota(jnp.int32, (nq, nk), 0) % SUBLANES
    c = lax.broadcasted_iota(jnp.int32, (nq, nk), 1)
    valid = (c >= r) & (c <= r + window)
    sc = jnp.where(valid[None], sc, -1e30)
    row_head = lax.broadcasted_iota(jnp.int32, (nq, 1), 0) // SUBLANES
    sink = jnp.zeros((nq, 1), F32)
    for h in range(n_heads):
        sink = jnp.where(row_head == h, sinks_ref[h], sink)
    m = jnp.maximum(jnp.max(sc, axis=-1, keepdims=True), sink[None])
    p = jnp.exp(sc - m)
    p = p / (jnp.sum(p, axis=-1, keepdims=True) + jnp.exp(sink[None] - m))
    ob[...] = jnp.einsum("bqk,bkd->bqd", p.astype(BF16), vc[...].astype(BF16), preferred_element_type=F32)

    for t in range(nt):
        for s in range(n_heads // heads_per_slab):
            out = jnp.zeros((gb, LANES), F32)
            for half in range(heads_per_slab):
                h = s * heads_per_slab + half
                g = (h // group) % heads_per_slab
                o = ob[:, h * SUBLANES + t, :]
                if g != half:
                    o = pltpu.roll(o, head_dim, 1)
                out = jnp.where((lane >= half * head_dim) & (lane < (half + 1) * head_dim), o, out)
            y_ref[t, :, s * LANES:(s + 1) * LANES] = out


def _attn_sample(proj_tm, ck, cv, sinks, tables, q_block, kv_block, n_heads, n_kv, head_dim):
    nt, nbat, _ = proj_tm.shape
    window = ck.shape[1]
    qw = n_heads * head_dim
    assert n_kv * head_dim == LANES and nt <= SUBLANES
    gb = _tile(nbat, 16)
    nk = window + 2 * SUBLANES
    kern = functools.partial(_attn_sample_kernel, n_heads=n_heads, group=n_heads // n_kv, head_dim=head_dim, nt=nt)
    tab_spec = pl.BlockSpec((SUBLANES, LANES), lambda b: (0, 0))
    return pl.pallas_call(
        kern,
        grid=(nbat // gb,),
        in_specs=[
            pl.BlockSpec(memory_space=pltpu.SMEM),
            pl.BlockSpec((nt, gb, qw), lambda b: (0, b, q_block)),
            pl.BlockSpec((nt, gb, 2 * LANES), lambda b: (0, b, kv_block)),
            pl.BlockSpec((gb, window, LANES), lambda b: (b, 0, 0)),
            pl.BlockSpec((gb, window, LANES), lambda b: (b, 0, 0)),
            tab_spec, tab_spec, tab_spec,
        ],
        out_specs=[
            pl.BlockSpec((nt, gb, qw), lambda b: (0, b, 0)),
            pl.BlockSpec((nt, gb, LANES), lambda b: (0, b, 0)),
        ],
        out_shape=[
            jax.ShapeDtypeStruct((nt, nbat, qw), F32),
            jax.ShapeDtypeStruct((nt, nbat, LANES), F32),
        ],
        scratch_shapes=[
            pltpu.VMEM((gb, n_heads * SUBLANES, LANES), F32),
            pltpu.VMEM((gb, nk, LANES), F32), pltpu.VMEM((gb, nk, LANES), F32),
            pltpu.VMEM((gb, n_heads * SUBLANES, LANES), F32),
        ],
        compiler_params=_cparams("arbitrary"),
        name="attn_sample",
    )(sinks, proj_tm, proj_tm, ck, cv, *tables)


def _rope_tables(pos, head_dim):
    rot = head_dim // 4
    half = rot // 2
    inv = jnp.exp(-(2.0 * jnp.arange(half, dtype=F32) / rot) * math.log(ROPE_THETA))
    ang = pos.astype(F32)[:, None] * inv[None, :]
    cos, sin = jnp.cos(ang), jnp.sin(ang)
    n = pos.shape[0]
    pad = jnp.zeros((n, head_dim - rot), F32)
    zeros = jnp.zeros((n, half), F32)
    cos_h = jnp.concatenate([cos, cos, pad + 1.0], axis=1)
    lo_h = jnp.concatenate([-sin, zeros, pad], axis=1)
    hi_h = jnp.concatenate([zeros, sin, pad], axis=1)
    reps = LANES // head_dim
    return tuple(jnp.tile(t, (1, reps)) for t in (cos_h, lo_h, hi_h))


def _gated_group_norm(y, z, norm_w, groups):
    y = y * _silu(z)
    gw = y.shape[1] // groups
    parts = []
    for g in range(groups):
        yg = y[:, g * gw:(g + 1) * gw]
        parts.append(yg * lax.rsqrt(jnp.mean(yg * yg, axis=-1, keepdims=True) + EPS))
    return jnp.concatenate(parts, axis=1) * norm_w


def _ssd_prompt_kernel(xbc_ref, z_ref, dt_ref, cw_ref, cb_ref, dtb_ref, alog_ref, dsk_ref, nw_ref, e_ref,
                       y_ref, hfin_ref, ext, ht, *, q, width, dstate, hdim, groups):
    j = pl.program_id(1)
    kconv = cw_ref.shape[0]

    @pl.when(j == 0)
    def _():
        ext[0:SUBLANES, :] = jnp.zeros((SUBLANES, ext.shape[1]), F32)
        ht[...] = jnp.zeros_like(ht)

    xbc = xbc_ref[0]
    ext[SUBLANES:, :] = xbc
    cw = cw_ref[...]
    conv = cw[kconv - 1:kconv] * xbc + cb_ref[...]
    for t in range(kconv - 1):
        conv = conv + cw[t:t + 1] * ext[pl.ds(SUBLANES - (kconv - 1) + t, q), :]
    ext[0:SUBLANES, :] = xbc[q - SUBLANES:, :]
    act = _silu(conv)
    xs = act[:, :width]
    bm = act[:, width:width + groups * dstate]
    cm = act[:, width + groups * dstate:]

    expand = e_ref[...]
    nh = expand.shape[0]
    dt_h = _softplus(dt_ref[0] + dtb_ref[...])
    a_h = dt_h * (-jnp.exp(alog_ref[...]))
    row = lax.broadcasted_iota(jnp.int32, (q, q), 0)
    col = lax.broadcasted_iota(jnp.int32, (q, q), 1)
    causal = col <= row
    tri = jnp.where(causal, 1.0, 0.0).astype(BF16)
    acum_h = _dot_exact_lhs01(tri, a_h)
    acum_ht = acum_h.T
    dt_x = _dot_exact_rhs01(dt_h[:, :nh], expand)
    acum_x = _dot_exact_rhs01(acum_h[:, :nh], expand)
    xdt = xs * dt_x
    acum_last = acum_x[q - 1:q, :]
    xd = xdt * jnp.exp(acum_last - acum_x)
    chunk_decay = jnp.exp(acum_last)
    exp_acum = jnp.exp(acum_x)

    lane = lax.broadcasted_iota(jnp.int32, (q, LANES), 1)
    gw = width // groups
    heads_per_group = gw // hdim
    pair = LANES // hdim
    y_parts = []
    for g in range(groups):
        b_g = bm[:, g * dstate:(g + 1) * dstate].astype(BF16)
        c_g = cm[:, g * dstate:(g + 1) * dstate].astype(BF16)
        cbm = _dot_nt(c_g, b_g)
        h_g = ht[:, g * gw:(g + 1) * gw]
        y_off = _dot(c_g, h_g.astype(BF16)) * exp_acum[:, g * gw:(g + 1) * gw]
        diag_parts = []
        for jp in range(heads_per_group // pair):
            l0 = g * gw + jp * LANES
            x_pair = xdt[:, l0:l0 + LANES].astype(BF16)
            out = jnp.zeros((q, LANES), F32)
            for half in range(pair):
                hd = g * heads_per_group + jp * pair + half
                decay = jnp.exp(jnp.where(causal, acum_h[:, hd:hd + 1] - acum_ht[hd:hd + 1, :], -jnp.inf))
                res = _dot((cbm * decay).astype(BF16), x_pair)
                in_head = (lane >= half * hdim) & (lane < (half + 1) * hdim)
                out = jnp.where(in_head, res, out)
            diag_parts.append(out)
        y_parts.append(jnp.concatenate(diag_parts, axis=1) + y_off)
        s_t = _dot_tn(b_g, xd[:, g * gw:(g + 1) * gw].astype(BF16))
        ht[:, g * gw:(g + 1) * gw] = h_g * chunk_decay[:, g * gw:(g + 1) * gw] + s_t
    y = jnp.concatenate(y_parts, axis=1) + xs * dsk_ref[...]
    y_ref[0] = _gated_group_norm(y, z_ref[0], nw_ref[...], groups)

    @pl.when(j == pl.num_programs(1) - 1)
    def _():
        hfin_ref[0] = ht[...].T


def _ssd_prompt(proj, blocks, params, dims):
    nb, l, _ = proj.shape
    q = SSD_CHUNK
    width, dstate, hdim, groups, xbcw = dims
    xbc_block, z_block, dt_block = blocks
    cw, cb, dtb, alog, dsk, nw, expand = params
    kern = functools.partial(_ssd_prompt_kernel, q=q, width=width, dstate=dstate, hdim=hdim, groups=groups)
    full = lambda a: pl.BlockSpec(a.shape, lambda b, j: (0,) * a.ndim)
    return pl.pallas_call(
        kern,
        grid=(nb, l // q),
        in_specs=[
            pl.BlockSpec((1, q, xbcw), lambda b, j: (b, j, xbc_block)),
            pl.BlockSpec((1, q, width), lambda b, j: (b, j, z_block)),
            pl.BlockSpec((1, q, LANES), lambda b, j: (b, j, dt_block)),
            full(cw), full(cb), full(dtb), full(alog), full(dsk), full(nw), full(expand),
        ],
        out_specs=[
            pl.BlockSpec((1, q, width), lambda b, j: (b, j, 0)),
            pl.BlockSpec((1, width, dstate), lambda b, j: (b, 0, 0)),
        ],
        out_shape=[
            jax.ShapeDtypeStruct((nb, l, width), F32),
            jax.ShapeDtypeStruct((nb, width, dstate), F32),
        ],
        scratch_shapes=[pltpu.VMEM((q + SUBLANES, xbcw), F32), pltpu.VMEM((dstate, width), F32)],
        compiler_params=_cparams("arbitrary", "arbitrary"),
        name="ssd_prompt",
    )(proj, proj, proj, cw, cb, dtb, alog, dsk, nw, expand)


def _ssd_sample_pre_kernel(xbc_ref, dt_ref, st_ref, cw_ref, cb_ref, dtb_ref, alog_ref, dsk_ref, e_ref,
                           ypart_ref, expa_ref, xd_ref, b_ref, c_ref, cd_ref,
                           *, nt, nbat, width, dstate, groups):
    kconv = cw_ref.shape[0]
    cw = cw_ref[...]
    expand = e_ref[...]
    nh = expand.shape[0]
    neg_a = -jnp.exp(alog_ref[...])
    full = [st_ref[t] for t in range(kconv - 1)]
    for t in range(nt):
        full.append(xbc_ref[pl.ds(t * nbat, nbat), :])
    xs, bm, cm, dt_x, acum_x, xdt = [], [], [], [], [], []
    acum_h = None
    for t in range(nt):
        conv = cb_ref[...] + cw[0:1] * full[t]
        for jj in range(1, kconv):
            conv = conv + cw[jj:jj + 1] * full[t + jj]
        act = _silu(conv)
        xs.append(act[:, :width])
        bm.append(act[:, width:width + groups * dstate])
        cm.append(act[:, width + groups * dstate:])
        dt_h = _softplus(dt_ref[pl.ds(t * nbat, nbat), :] + dtb_ref[...])
        a_h = dt_h * neg_a
        acum_h = a_h if acum_h is None else acum_h + a_h
        dt_x.append(_dot_exact_rhs01(dt_h[:, :nh], expand))
        acum_x.append(_dot_exact_rhs01(acum_h[:, :nh], expand))
        xdt.append(xs[t] * dt_x[t])
    cd_ref[...] = jnp.exp(acum_h)
    gw = width // groups
    for t in range(nt):
        y = xs[t] * dsk_ref[...]
        for s in range(t + 1):
            cb_parts = []
            for g in range(groups):
                prod = cm[t][:, g * dstate:(g + 1) * dstate] * bm[s][:, g * dstate:(g + 1) * dstate]
                cb_parts.append(jnp.broadcast_to(jnp.sum(prod, axis=-1, keepdims=True), (nbat, gw)))
            cb_x = jnp.concatenate(cb_parts, axis=1)
            y = y + cb_x * jnp.exp(acum_x[t] - acum_x[s]) * xdt[s]
        ypart_ref[t] = y
        expa_ref[t] = jnp.exp(acum_x[t])
    xd_ref[...] = jnp.zeros_like(xd_ref)
    b_ref[...] = jnp.zeros_like(b_ref)
    c_ref[...] = jnp.zeros_like(c_ref)
    for t in range(nt):
        xd_ref[:, t, :] = xdt[t] * jnp.exp(acum_x[nt - 1] - acum_x[t])
        b_ref[:, t, :] = bm[t]
        c_ref[:, t, :] = cm[t]


def _ssd_sample_pre(proj_s, blocks, state_tm, params, nt, nbat, dims, srows):
    width, dstate, hdim, groups, xbcw = dims
    xbc_block, dt_block = blocks
    cw, cb, dtb, alog, dsk, expand = params
    rows = nt * nbat
    kern = functools.partial(_ssd_sample_pre_kernel, nt=nt, nbat=nbat, width=width, dstate=dstate, groups=groups)
    sd = jax.ShapeDtypeStruct
    out_shapes = [(nt, nbat, width), (nt, nbat, width), (nbat, srows, width),
                  (nbat, srows, groups * dstate), (nbat, srows, groups * dstate), (nbat, LANES)]
    return pl.pallas_call(
        kern,
        grid=(1,),
        in_specs=[pl.BlockSpec((rows, xbcw), lambda i: (0, xbc_block)),
                  pl.BlockSpec((rows, LANES), lambda i: (0, dt_block)),
                  _whole(state_tm.shape)] + [_whole(a.shape) for a in params],
        out_specs=[_whole(s) for s in out_shapes],
        out_shape=[sd(s, F32) for s in out_shapes],
        compiler_params=_cparams("arbitrary"),
        name="ssd_sample_pre",
    )(proj_s, proj_s, state_tm, cw, cb, dtb, alog, dsk, expand)


def _ssd_sample_state_kernel(cd_ref, c_ref, b_ref, xd_ref, h0_ref, *rest, heads, hdim, dstate, groups):
    yoff_ref, hnew_ref = rest[-2:]
    gb = h0_ref.shape[1]
    hpg = heads // groups
    gw = hpg * hdim
    for i in range(gb):
        b = pl.program_id(0) * gb + i
        for g in range(groups):
            hm = h0_ref[0, i, g * hpg:(g + 1) * hpg].reshape(gw, dstate)
            c_g = c_ref[i, :, g * dstate:(g + 1) * dstate].astype(BF16)
            b_g = b_ref[i, :, g * dstate:(g + 1) * dstate].astype(BF16)
            yoff_ref[i, :, g * gw:(g + 1) * gw] = _dot_nt(c_g, hm.astype(BF16))
            upd = _dot_tn(xd_ref[i, :, g * gw:(g + 1) * gw].astype(BF16), b_g)
            for hh in range(hpg):
                hd = g * hpg + hh
                hnew_ref[0, i, hd] = (h0_ref[0, i, hd] * cd_ref[b * heads + hd]
                                      + upd[hh * hdim:(hh + 1) * hdim, :])
    for later in range(1, hnew_ref.shape[0]):
        hnew_ref[later] = jnp.zeros(hnew_ref.shape[1:], F32)


def _ssd_sample_state(cd_flat, c_bm, b_bm, xd_bm, state_all, layer, stacked, dims):
    width, dstate, hdim, groups, _ = dims
    depth, nbat, heads = state_all.shape[:3]
    rows = c_bm.shape[1]
    gb = _tile(nbat, 8)
    kern = functools.partial(_ssd_sample_state_kernel, heads=heads, hdim=hdim, dstate=dstate, groups=groups)
    in_specs = [
        pl.BlockSpec(memory_space=pltpu.SMEM),
        pl.BlockSpec((gb, rows, groups * dstate), lambda b: (b, 0, 0)),
        pl.BlockSpec((gb, rows, groups * dstate), lambda b: (b, 0, 0)),
        pl.BlockSpec((gb, rows, width), lambda b: (b, 0, 0)),
        pl.BlockSpec((1, gb, heads, hdim, dstate), lambda b: (layer, b, 0, 0, 0)),
    ]
    args = [cd_flat, c_bm, b_bm, xd_bm, state_all]
    if layer == 0:
        assert stacked is None
        state_spec = pl.BlockSpec((depth, gb, heads, hdim, dstate), lambda b: (0, b, 0, 0, 0))
        aliases = {}
    else:
        in_specs.append(pl.BlockSpec(memory_space=pl.ANY))
        args.append(stacked)
        state_spec = pl.BlockSpec((1, gb, heads, hdim, dstate), lambda b: (layer, b, 0, 0, 0))
        aliases = {len(args) - 1: 1}
    return pl.pallas_call(
        kern,
        grid=(nbat // gb,),
        in_specs=in_specs,
        out_specs=[pl.BlockSpec((gb, rows, width), lambda b: (b, 0, 0)), state_spec],
        out_shape=[
            jax.ShapeDtypeStruct((nbat, rows, width), F32),
            jax.ShapeDtypeStruct(state_all.shape, F32),
        ],
        input_output_aliases=aliases,
        compiler_params=_cparams("arbitrary"),
        name="ssd_sample_state",
    )(*args)


def _ssd_sample_post_kernel(ypart_ref, yoff_ref, expa_ref, z_ref, nw_ref, y_ref, *, groups, nt, nbat):
    for t in range(nt):
        rows = pl.ds(t * nbat, nbat)
        y = ypart_ref[t] + yoff_ref[:, t, :] * expa_ref[t]
        y_ref[rows, :] = _gated_group_norm(y, z_ref[rows, :], nw_ref[...], groups)


def _ssd_sample_post(ypart, yoff_bm, expa, proj_s, z_block, nw, groups):
    nt, nbat, width = ypart.shape
    rows = nt * nbat
    kern = functools.partial(_ssd_sample_post_kernel, groups=groups, nt=nt, nbat=nbat)
    return pl.pallas_call(
        kern,
        grid=(1,),
        in_specs=[_whole(ypart.shape), _whole(yoff_bm.shape), _whole(expa.shape),
                  pl.BlockSpec((rows, width), lambda i: (0, z_block)), _whole(nw.shape)],
        out_specs=_whole((rows, width)),
        out_shape=jax.ShapeDtypeStruct((rows, width), F32),
        compiler_params=_cparams("arbitrary"),
        name="ssd_sample_post",
    )(ypart, yoff_bm, expa, proj_s, nw)


def _merge_math(x, gates, y_conv, y_attn, y_ssm, gate1, wc_ref, wa_ref, ws_ref, wo_ref, d):
    merged = (_sigmoid(gates[:, :d]) * _dot(y_conv.astype(BF16), wc_ref[...])
              + _sigmoid(gates[:, d:2 * d]) * _dot(y_attn.astype(BF16), wa_ref[...])
              + _sigmoid(gates[:, 2 * d:]) * _dot(y_ssm.astype(BF16), ws_ref[...]))
    return x + gate1 * _dot(merged.astype(BF16), wo_ref[...])


def _merge_kernel(*refs, d, tm, cw, conv):
    refs = list(refs)
    x_ref, g_ref = refs[:2]
    del refs[:2]
    if conv:
        p_ref, cw_ref = refs[:2]
        del refs[:2]
    else:
        yc_ref = refs.pop(0)
    ya_ref, ys_ref, g1_ref, wc_ref, wa_ref, ws_ref, wo_ref = refs[:7]
    del refs[:7]
    o_ref = refs.pop(0)

    if conv:
        st_ref, ext = refs

        @pl.when(pl.program_id(1) == 0)
        def _():
            ext[0:SUBLANES, :] = jnp.zeros((SUBLANES, cw), F32)

        p = p_ref[0]
        u = p[:, cw:2 * cw] * p[:, 2 * cw:]
        ext[SUBLANES:, :] = u
        w = cw_ref[...]
        acc = w[0:1] * ext[pl.ds(SUBLANES - 2, tm), :] + w[1:2] * ext[pl.ds(SUBLANES - 1, tm), :] + w[2:3] * u
        tail = u[tm - SUBLANES:, :]
        ext[0:SUBLANES, :] = tail
        st_ref[0] = tail
        y_conv = p[:, :cw] * acc
    else:
        y_conv = yc_ref[0]

    o_ref[0] = _merge_math(x_ref[0], g_ref[0], y_conv, ya_ref[0], ys_ref[0], g1_ref[0],
                           wc_ref, wa_ref, ws_ref, wo_ref, d)


def _merge(x, proj, gate_block, conv_src, ya, ys, mod, weights):
    nb, l, d = x.shape
    tm = _tile(l, ROUTE_TILE)
    nt = l // tm
    conv = isinstance(conv_src, tuple)
    tok = lambda w: pl.BlockSpec((1, tm, w), lambda b, i: (b, i, 0))
    full = lambda a: pl.BlockSpec(a.shape, lambda b, i: (0, 0))
    in_specs = [tok(d), pl.BlockSpec((1, tm, 3 * d), lambda b, i: (b, i, gate_block))]
    args = [x, proj]
    out_specs = [tok(d)]
    out_shape = [jax.ShapeDtypeStruct((nb, l, d), F32)]
    scratch = []
    if conv:
        conv_block, w_conv = conv_src
        cw = w_conv.shape[1]
        assert w_conv.shape[0] == 3
        in_specs += [pl.BlockSpec((1, tm, 3 * cw), lambda b, i: (b, i, conv_block)), full(w_conv)]
        args += [proj, w_conv]
        out_specs.append(pl.BlockSpec((1, SUBLANES, cw), lambda b, i: (b, 0, 0)))
        out_shape.append(jax.ShapeDtypeStruct((nb, SUBLANES, cw), F32))
        scratch.append(pltpu.VMEM((tm + SUBLANES, cw), F32))
    else:
        cw = conv_src.shape[2]
        in_specs.append(tok(cw))
        args.append(conv_src)
    in_specs += [tok(ya.shape[2]), tok(ys.shape[2]), _mod_spec(mod, tm, 2, d)] + [full(w) for w in weights]
    args += [ya, ys, mod] + list(weights)
    kern = functools.partial(_merge_kernel, d=d, tm=tm, cw=cw, conv=conv)
    return pl.pallas_call(
        kern,
        grid=(nb, nt),
        in_specs=in_specs,
        out_specs=out_specs,
        out_shape=out_shape,
        scratch_shapes=scratch,
        compiler_params=_cparams("arbitrary", "arbitrary"),
        name="merge",
    )(*args)


def _finish(x, gate, f, nf_ref, final_norm):
    out = x + gate * f
    if final_norm:
        out = out * lax.rsqrt(jnp.mean(out * out, axis=-1, keepdims=True) + EPS) * nf_ref[...]
    return out


def _ffn_kernel(x_ref, sh_ref, sc_ref, g2_ref, nw_ref, nf_ref, wg_ref, wu_ref, wd_ref, o_ref, h_scr, acc,
                *, final_norm):
    f = pl.program_id(2)

    @pl.when(f == 0)
    def _():
        h_scr[...] = _rms_mod(x_ref[0], nw_ref[...], sc_ref[0], sh_ref[0]).astype(BF16)
        acc[...] = jnp.zeros_like(acc)

    h = h_scr[...]
    a = _silu(_dot(h, wg_ref[...].astype(BF16))) * _dot(h, wu_ref[...].astype(BF16))
    acc[...] += _dot(a.astype(BF16), wd_ref[...].astype(BF16))

    @pl.when(f == pl.num_programs(2) - 1)
    def _():
        o_ref[0] = _finish(x_ref[0], g2_ref[0], acc[...], nf_ref, final_norm)


def _ffn(x, mod, norm_w, norm_final, wg, wu, wd, final_norm):
    nb, l, d = x.shape
    ff = wg.shape[1]
    tm = _tile(l, 1024)
    tf = _tile(ff, 512)
    kern = functools.partial(_ffn_kernel, final_norm=final_norm)
    vec = pl.BlockSpec((1, d), lambda b, i, f: (0, 0))
    return pl.pallas_call(
        kern,
        grid=(nb, l // tm, ff // tf),
        in_specs=[
            pl.BlockSpec((1, tm, d), lambda b, i, f: (b, i, 0)),
            _mod_spec(mod, tm, 3, d), _mod_spec(mod, tm, 4, d), _mod_spec(mod, tm, 5, d),
            vec, vec,
            pl.BlockSpec((d, tf), lambda b, i, f: (0, f)),
            pl.BlockSpec((d, tf), lambda b, i, f: (0, f)),
            pl.BlockSpec((tf, d), lambda b, i, f: (f, 0)),
        ],
        out_specs=pl.BlockSpec((1, tm, d), lambda b, i, f: (b, i, 0)),
        out_shape=jax.ShapeDtypeStruct((nb, l, d), F32),
        scratch_shapes=[pltpu.VMEM((tm, d), BF16), pltpu.VMEM((tm, d), F32)],
        compiler_params=_cparams("arbitrary", "arbitrary", "arbitrary"),
        name="ffn",
    )(x, mod, mod, mod, norm_w.reshape(1, d), norm_final.reshape(1, d), wg, wu, wd)


def _route_math(x, norm_w, scale, shift, wr, n_experts):
    h = _rms_mod(x, norm_w, scale, shift)
    h_hi = h.astype(BF16)
    h_lo = (h - h_hi.astype(F32)).astype(BF16)
    r_hi = wr.astype(BF16)
    r_lo = (wr - r_hi.astype(F32)).astype(BF16)
    logits = _dot(h_hi, r_hi) + _dot(h_lo, r_hi) + _dot(h_hi, r_lo)
    lane = lax.broadcasted_iota(jnp.int32, logits.shape, 1).astype(F32)
    neg = -jnp.inf
    lg = jnp.where(lane < n_experts, logits, neg)
    m1 = jnp.max(lg, axis=-1, keepdims=True)
    i1 = jnp.min(jnp.where(lg == m1, lane, float(LANES)), axis=-1, keepdims=True)
    rest = jnp.where(lane == i1, neg, lg)
    m2 = jnp.max(rest, axis=-1, keepdims=True)
    i2 = jnp.min(jnp.where(rest == m2, lane, float(LANES)), axis=-1, keepdims=True)
    e2 = jnp.exp(m2 - m1)
    w1 = 1.0 / (1.0 + e2)
    w2 = e2 / (1.0 + e2)
    w_cols = jnp.where(lane == 0.0, w1, jnp.where(lane == 1.0, w2, 0.0))
    chosen = jnp.where(lane == 0.0, i1, jnp.where(lane == 1.0, i2, 0.0))
    return w_cols, chosen.T[:SUBLANES, :]


def _route_kernel(x_ref, sh_ref, sc_ref, nw_ref, wr_ref, w_ref, e_ref, *, n_experts):
    w_cols, e_rows = _route_math(x_ref[0], nw_ref[...], sc_ref[0], sh_ref[0], wr_ref[...], n_experts)
    w_ref[...] = w_cols
    e_ref[...] = e_rows


def _route(x, mod, norm_w, w_router_pad, n_experts):
    nb, l, d = x.shape
    tm = _tile(l, ROUTE_TILE)
    nt = l // tm
    kern = functools.partial(_route_kernel, n_experts=n_experts)
    return pl.pallas_call(
        kern,
        grid=(nb, nt),
        in_specs=[
            pl.BlockSpec((1, tm, d), lambda b, i: (b, i, 0)),
            _mod_spec(mod, tm, 3, d), _mod_spec(mod, tm, 4, d),
            pl.BlockSpec((1, d), lambda b, i: (0, 0)),
            pl.BlockSpec((d, LANES), lambda b, i: (0, 0)),
        ],
        out_specs=[
            pl.BlockSpec((tm, LANES), lambda b, i: (b * nt + i, 0)),
            pl.BlockSpec((SUBLANES, tm), lambda b, i: (0, b * nt + i)),
        ],
        out_shape=[
            jax.ShapeDtypeStruct((nb * l, LANES), F32),
            jax.ShapeDtypeStruct((SUBLANES, nb * l), F32),
        ],
        compiler_params=_cparams("arbitrary", "arbitrary"),
        name="moe_route",
    )(x, mod, mod, norm_w.reshape(1, d), w_router_pad)


def _plan_kernel(e_ref, pos_ref, te_ref, *, n_experts, tile, row_tile):
    steps = e_ref.shape[1] // tile
    sub = lax.broadcasted_iota(jnp.int32, (SUBLANES, tile), 0).astype(F32)
    sub_col = lax.broadcasted_iota(jnp.int32, (SUBLANES, 1), 0)

    def member(i):
        blk = e_ref[:, pl.ds(pl.multiple_of(i * tile, tile), tile)]
        e1, e2 = blk[0:1, :], blk[1:2, :]
        return e1, e2, jnp.where((sub == e1) | (sub == e2), 1.0, 0.0)

    def count_body(i, cnt):
        return cnt + jnp.sum(member(i)[2], axis=1, keepdims=True)

    cnt = lax.fori_loop(0, steps, count_body, jnp.zeros((SUBLANES, 1), F32))
    padded = jnp.floor((cnt + (row_tile - 1)) * (1.0 / row_tile)) * row_tile
    off = jnp.zeros((SUBLANES, 1), F32)
    run = jnp.zeros((1, 1), F32)
    for e in range(n_experts):
        off = jnp.where(sub_col == e, run, off)
        run = run + padded[e:e + 1, :]
    seg_end = off + padded

    r = lax.broadcasted_iota(jnp.int32, (tile, tile), 0)
    c = lax.broadcasted_iota(jnp.int32, (tile, tile), 1)
    before = jnp.where(r < c, 1.0, 0.0).astype(BF16)

    def pos_body(i, carry):
        e1, e2, m = member(i)
        val = off + carry + _dot(m.astype(BF16), before)
        p1 = jnp.sum(jnp.where(sub == e1, val, 0.0), axis=0, keepdims=True)
        p2 = jnp.sum(jnp.where(sub == e2, val, 0.0), axis=0, keepdims=True)
        rows = jnp.where(sub == 0.0, p1, jnp.where(sub == 1.0, p2, 0.0))
        pos_ref[:, pl.ds(pl.multiple_of(i * tile, tile), tile)] = rows.astype(jnp.int32)
        return carry + jnp.sum(m, axis=1, keepdims=True)

    lax.fori_loop(0, steps, pos_body, jnp.zeros((SUBLANES, 1), F32))

    sub_l = lax.broadcasted_iota(jnp.int32, (SUBLANES, LANES), 0)
    start = lax.broadcasted_iota(jnp.int32, (SUBLANES, LANES), 1).astype(F32) * row_tile
    owner = jnp.sum(jnp.where((seg_end <= start) & (sub_l < n_experts), 1.0, 0.0), axis=0, keepdims=True)
    owner = jnp.minimum(owner, n_experts - 1.0)
    used = run * (1.0 / row_tile)
    te_ref[...] = jnp.where(sub_l == 0, owner, jnp.where(sub_l == 1, used, 0.0)).astype(jnp.int32)


def _plan(e_all, n_experts, row_tile):
    t = e_all.shape[1]
    assert t % LANES == 0 and n_experts <= SUBLANES
    tile = _tile(t, ROUTE_TILE)
    kern = functools.partial(_plan_kernel, n_experts=n_experts, tile=tile, row_tile=row_tile)
    return pl.pallas_call(
        kern,
        out_shape=[
            jax.ShapeDtypeStruct((SUBLANES, t), jnp.int32),
            jax.ShapeDtypeStruct((SUBLANES, LANES), jnp.int32),
        ],
        compiler_params=pltpu.CompilerParams(vmem_limit_bytes=VMEM_LIMIT_BYTES),
        name="moe_plan",
    )(e_all)


def _scatter_kernel(p1_ref, p2_ref, x_ref, sh_ref, sc_ref, nw_ref, xs_in_ref, xs_ref, h_scr, sem, *, tm):
    del xs_in_ref
    h_scr[...] = _rms_mod(x_ref[0], nw_ref[...], sc_ref[0], sh_ref[0]).reshape(h_scr.shape)

    def row_copy(i, k, p):
        return pltpu.make_async_copy(h_scr.at[i, pl.ds(k, 1), :], xs_ref.at[pl.ds(p, 1), :], sem)

    def issue(i, carry):
        for k in range(SUBLANES):
            t = i * SUBLANES + k
            row_copy(i, k, p1_ref[t]).start(priority=0)
            row_copy(i, k, p2_ref[t]).start(priority=1)
        return carry

    def drain(t, carry):
        row_copy(0, 0, 0).wait()
        row_copy(0, 0, 0).wait()
        return carry

    lax.fori_loop(0, tm // SUBLANES, issue, 0)
    lax.fori_loop(0, tm, drain, 0, unroll=SUBLANES)


def _scatter(x, mod, norm_w, p1, p2, xs_sorted):
    nb, l, d = x.shape
    tm = _tile(l, ROUTE_TILE)
    nt = l // tm
    kern = functools.partial(_scatter_kernel, tm=tm)
    idx = pl.BlockSpec((tm,), lambda b, i: (b * nt + i,), memory_space=pltpu.SMEM)
    return pl.pallas_call(
        kern,
        grid=(nb, nt),
        in_specs=[
            idx, idx,
            pl.BlockSpec((1, tm, d), lambda b, i: (b, i, 0)),
            _mod_spec(mod, tm, 3, d), _mod_spec(mod, tm, 4, d),
            pl.BlockSpec((1, d), lambda b, i: (0, 0)),
            pl.BlockSpec(memory_space=pl.ANY),
        ],
        out_specs=pl.BlockSpec(memory_space=pl.ANY),
        out_shape=jax.ShapeDtypeStruct(xs_sorted.shape, F32),
        scratch_shapes=[pltpu.VMEM((tm // SUBLANES, SUBLANES, d), F32), pltpu.SemaphoreType.DMA],
        input_output_aliases={6: 0},
        compiler_params=_cparams("arbitrary", "arbitrary"),
        name="moe_scatter",
    )(p1, p2, x, mod, mod, norm_w.reshape(1, d), xs_sorted)


def _group_ffn_kernel(te_ref, xs_ref, wg_ref, wu_ref, wd_ref, y_ref, h_scr, acc):
    j = pl.program_id(0)
    f = pl.program_id(1)

    @pl.when(j < te_ref[LANES])
    def _():
        @pl.when(f == 0)
        def _():
            h_scr[...] = xs_ref[...].astype(BF16)
            acc[...] = jnp.zeros_like(acc)

        h = h_scr[...]
        a = _silu(_dot(h, wg_ref[0].astype(BF16))) * _dot(h, wu_ref[0].astype(BF16))
        acc[...] += _dot(a.astype(BF16), wd_ref[0].astype(BF16))

        @pl.when(f == pl.num_programs(1) - 1)
        def _():
            y_ref[...] = acc[...]

    @pl.when((j >= te_ref[LANES]) & (f == 0))
    def _():
        y_ref[...] = jnp.zeros_like(y_ref)


def _group_ffn(te_flat, xs_sorted, wg, wu, wd, row_tile):
    rows, d = xs_sorted.shape
    ff = wg.shape[2]
    tf = _tile(ff, 512)
    nf = ff // tf

    def tile_of(j, te):
        return jnp.minimum(j, te[LANES] - 1)

    def f_of(j, f, te):
        return jnp.where(j < te[LANES], f, nf - 1)

    grid_spec = pltpu.PrefetchScalarGridSpec(
        num_scalar_prefetch=1,
        grid=(rows // row_tile, nf),
        in_specs=[
            pl.BlockSpec((row_tile, d), lambda j, f, te: (tile_of(j, te), 0)),
            pl.BlockSpec((1, d, tf), lambda j, f, te: (te[tile_of(j, te)], 0, f_of(j, f, te))),
            pl.BlockSpec((1, d, tf), lambda j, f, te: (te[tile_of(j, te)], 0, f_of(j, f, te))),
            pl.BlockSpec((1, tf, d), lambda j, f, te: (te[tile_of(j, te)], f_of(j, f, te), 0)),
        ],
        out_specs=pl.BlockSpec((row_tile, d), lambda j, f, te: (j, 0)),
        scratch_shapes=[pltpu.VMEM((row_tile, d), BF16), pltpu.VMEM((row_tile, d), F32)],
    )
    return pl.pallas_call(
        _group_ffn_kernel,
        grid_spec=grid_spec,
        out_shape=jax.ShapeDtypeStruct((rows, d), F32),
        compiler_params=_cparams("arbitrary", "arbitrary"),
        name="moe_group_ffn",
    )(te_flat, xs_sorted, wg, wu, wd)


def _combine_kernel(p1_ref, p2_ref, x_ref, g2_ref, w_ref, nf_ref, y_hbm, o_ref, buf, sems,
                    *, tm, nt, n_steps, final_norm):
    step = pl.program_id(0) * nt + pl.program_id(1)

    def row_copy(slot, choice, i, k, p):
        return pltpu.make_async_copy(y_hbm.at[pl.ds(p, 1), :], buf.at[slot, choice, i, pl.ds(k, 1), :],
                                     sems.at[slot])

    def start_tile(tile, slot):
        def issue(i, carry):
            for k in range(SUBLANES):
                t = tile * tm + i * SUBLANES + k
                row_copy(slot, 0, i, k, p1_ref[t]).start(priority=0)
                row_copy(slot, 1, i, k, p2_ref[t]).start(priority=1)
            return carry
        lax.fori_loop(0, tm // SUBLANES, issue, 0)

    def wait_tile(slot):
        def drain(t, carry):
            row_copy(slot, 0, 0, 0, 0).wait()
            row_copy(slot, 0, 0, 0, 0).wait()
            return carry
        lax.fori_loop(0, tm, drain, 0, unroll=SUBLANES)

    @pl.when(step == 0)
    def _():
        start_tile(0, 0)

    for slot in range(2):
        @pl.when(step % 2 == slot)
        def _(slot=slot):
            @pl.when(step + 1 < n_steps)
            def _():
                start_tile(step + 1, 1 - slot)

            wait_tile(slot)
            w = w_ref[...]
            d = buf.shape[-1]
            f = w[:, 0:1] * buf[slot, 0].reshape(tm, d) + w[:, 1:2] * buf[slot, 1].reshape(tm, d)
            o_ref[0] = _finish(x_ref[0], g2_ref[0], f, nf_ref, final_norm)


def _combine(x, mod, w_cols, p1, p2, y_sorted, norm_final, final_norm):
    nb, l, d = x.shape
    tm = _tile(l, ROUTE_TILE)
    nt = l // tm
    kern = functools.partial(_combine_kernel, tm=tm, nt=nt, n_steps=nb * nt, final_norm=final_norm)
    grid_spec = pltpu.PrefetchScalarGridSpec(
        num_scalar_prefetch=2,
        grid=(nb, nt),
        in_specs=[
            pl.BlockSpec((1, tm, d), lambda b, i, *_: (b, i, 0)),
            _mod_spec(mod, tm, 5, d),
            pl.BlockSpec((tm, LANES), lambda b, i, *_: (b * nt + i, 0)),
            pl.BlockSpec((1, d), lambda b, i, *_: (0, 0)),
            pl.BlockSpec(memory_space=pl.ANY),
        ],
        out_specs=pl.BlockSpec((1, tm, d), lambda b, i, *_: (b, i, 0)),
        scratch_shapes=[pltpu.VMEM((2, 2, tm // SUBLANES, SUBLANES, d), F32), pltpu.SemaphoreType.DMA((2,))],
    )
    return pl.pallas_call(
        kern,
        grid_spec=grid_spec,
        out_shape=jax.ShapeDtypeStruct((nb, l, d), F32),
        compiler_params=_cparams("arbitrary", "arbitrary"),
        name="moe_combine",
    )(p1, p2, x, mod, w_cols, norm_final.reshape(1, d), y_sorted)


def _moe(groups, norm_w, norm_final, w_router_pad, wg, wu, wd, final_norm):
    n_experts = wg.shape[0]
    d = groups[0][0].shape[2]
    routed = [_route(x, mod, norm_w, w_router_pad, n_experts) for x, mod in groups]
    e_all = jnp.concatenate([e for _, e in routed], axis=1)
    t = e_all.shape[1]
    pos, te = _plan(e_all, n_experts, MOE_ROW_TILE)
    te_flat = te.reshape(SUBLANES * LANES)
    n_tiles = -(-TOP_K * t // MOE_ROW_TILE) + n_experts
    assert n_tiles <= LANES
    xs_sorted = jnp.zeros((n_tiles * MOE_ROW_TILE, d), F32)
    spans, start = [], 0
    for x, _ in groups:
        n = x.shape[0] * x.shape[1]
        spans.append((start, start + n))
        start += n
    for (x, mod), (lo, hi) in zip(groups, spans):
        xs_sorted = _scatter(x, mod, norm_w, pos[0, lo:hi], pos[1, lo:hi], xs_sorted)
    y_sorted = _group_ffn(te_flat, xs_sorted, wg, wu, wd, MOE_ROW_TILE)
    return [_combine(x, mod, w_cols, pos[0, lo:hi], pos[1, lo:hi], y_sorted, norm_final, final_norm)
            for (x, mod), (w_cols, _), (lo, hi) in zip(groups, routed, spans)]


def kernel(x_prompt, x_sample, c_prompt, c_sample, cache_k, cache_v, state_conv, state_ssm_conv, state_ssm,
           w_mod, b_mod, norm_mix, norm_ffn, norm_final, w_in, w_sconv, sinks, ssm_conv_w, ssm_conv_b,
           dt_bias, a_log, d_skip, ssm_norm, w_br_conv, w_br_attn, w_br_ssm, w_o,
           ffn_w_gate, ffn_w_up, ffn_w_down, router, moe_w_gate, moe_w_up, moe_w_down):
    nbp, seq, d = x_prompt.shape
    nbat, nt, _ = x_sample.shape
    depth = w_mod.shape[0]
    cwid = w_sconv.shape[2]
    n_heads = sinks.shape[1]
    window, n_kv, head_dim = cache_k.shape[2:]
    heads, hdim, dstate = state_ssm.shape[2:]
    width = heads * hdim
    xbcw = ssm_conv_w.shape[2]
    groups = SSM_GROUPS
    n_experts = router.shape[2]
    aw = n_heads * head_dim
    kvw = n_kv * head_dim
    dims = (width, dstate, hdim, groups, xbcw)
    assert window == ATTN_BLOCK and seq % ATTN_BLOCK == 0 and xbcw == width + 2 * groups * dstate

    o_cv, o_q, o_k, o_z, o_xbc = 0, 3 * cwid, 3 * cwid + aw, 3 * cwid + aw + 2 * kvw, 3 * cwid + aw + 2 * kvw + width
    o_dt = o_xbc + xbcw
    o_g = o_dt + heads
    n_in = w_in.shape[2]
    dt_pad = 2 * LANES - heads

    def regroup(w):
        return jnp.concatenate(
            [w[:, o_g:n_in], w[:, o_cv:o_q], w[:, o_xbc:o_dt], w[:, o_z:o_xbc], w[:, o_q:o_k], w[:, o_k:o_z],
             w[:, o_dt:o_g], jnp.zeros((d, dt_pad), w.dtype)], axis=1).astype(BF16)

    p_gate, p_conv, p_xbc, p_z = 0, 3 * d, 3 * d + 3 * cwid, 3 * d + 3 * cwid + xbcw
    p_q = p_z + width
    p_kv = p_q + aw
    p_dt = p_kv + 2 * kvw
    blk = lambda off, w: off // w
    assert all(off % w == 0 for off, w in ((p_conv, 3 * cwid), (p_xbc, xbcw), (p_z, width), (p_q, aw),
                                           (p_kv, 2 * kvw), (p_dt, LANES)))

    n_c = nbp + nbat
    c_rows = -(-n_c // SUBLANES) * SUBLANES
    c_all = jnp.pad(jnp.concatenate([c_prompt, c_sample], axis=0), ((0, c_rows - n_c), (0, 0)))
    mod_all = _modulation(c_all, w_mod, b_mod)

    xs_tm = x_sample.transpose(1, 0, 2).reshape(1, nt * nbat, d)
    srows = 2 * SUBLANES

    pos_p = jnp.arange(seq, dtype=jnp.int32)
    pos_s = PAST_LEN + jnp.arange(SUBLANES, dtype=jnp.int32)
    tab_p = _rope_tables(pos_p, head_dim)
    tab_s = _rope_tables(pos_s, head_dim)

    expand = jnp.repeat(jnp.eye(heads, dtype=F32), hdim, axis=1).astype(BF16)
    pad_h = lambda v: jnp.pad(v, (0, LANES - heads)).reshape(1, LANES)

    xp, xs = x_prompt, xs_tm
    outs = {k: [] for k in ("kp", "vp", "cp", "scp", "sp", "ks", "vs", "cs", "scs")}
    new_state_s = None
    for i in range(depth):
        w_in_i = regroup(w_in[i])
        wc, wa, ws, wo = (w[i].astype(BF16) for w in (w_br_conv, w_br_attn, w_br_ssm, w_o))
        mod_p = mod_all[i, :nbp].reshape(nbp, 1, 6 * d)
        mod_s = jnp.tile(mod_all[i, nbp:n_c], (nt, 1)).reshape(1, nt * nbat, 6 * d)
        ssm_params = (ssm_conv_w[i], ssm_conv_b[i].reshape(1, xbcw), pad_h(dt_bias[i]), pad_h(a_log[i]),
                      jnp.repeat(d_skip[i], hdim).reshape(1, width))
        nw_ssm = ssm_norm[i].reshape(1, width)

        proj = _inproj(xp, mod_p, norm_mix[i], w_in_i)
        y_attn, kv_last = _attn_prompt(proj, sinks[i], tab_p, blk(p_q, aw), blk(p_kv, 2 * kvw),
                                       n_heads, n_kv, head_dim)
        y_ssm, h_fin = _ssd_prompt(proj, (blk(p_xbc, xbcw), blk(p_z, width), blk(p_dt, LANES)),
                                   ssm_params + (nw_ssm, expand), dims)
        xp, conv_tail = _merge(xp, proj, blk(p_gate, 3 * d), (blk(p_conv, 3 * cwid), w_sconv[i]),
                               y_attn, y_ssm, mod_p, (wc, wa, ws, wo))
        outs["kp"].append(kv_last[:, :, :kvw].reshape(nbp, window, n_kv, head_dim))
        outs["vp"].append(kv_last[:, :, kvw:].reshape(nbp, window, n_kv, head_dim))
        outs["cp"].append(conv_tail[:, SUBLANES - (w_sconv.shape[1] - 1):])
        outs["scp"].append(proj[:, seq - (ssm_conv_w.shape[1] - 1):, p_xbc:p_xbc + xbcw])
        outs["sp"].append(h_fin.reshape(nbp, heads, hdim, dstate))

        proj_s = _inproj(xs, mod_s, norm_mix[i], w_in_i)[0]
        proj_tm = proj_s.reshape(nt, nbat, proj_s.shape[1])
        y_conv_s, u_s = _conv_sample(proj_s, blk(p_conv, 3 * cwid), state_conv[i].transpose(1, 0, 2),
                                     w_sconv[i], nt, nbat, cwid)
        y_attn_tm, k_new = _attn_sample(proj_tm, cache_k[i].reshape(nbat, window, kvw),
                                        cache_v[i].reshape(nbat, window, kvw), sinks[i], tab_s,
                                        blk(p_q, aw), blk(p_kv, 2 * kvw), n_heads, n_kv, head_dim)
        ypart, expa, xd_bm, b_bm, c_bm, cd = _ssd_sample_pre(
            proj_s, (blk(p_xbc, xbcw), blk(p_dt, LANES)), state_ssm_conv[i].transpose(1, 0, 2),
            ssm_params + (expand,), nt, nbat, dims, srows)
        yoff_bm, new_state_s = _ssd_sample_state(cd[:, :heads].reshape(nbat * heads), c_bm, b_bm, xd_bm,
                                                 state_ssm, i, new_state_s, dims)
        y_ssm_s = _ssd_sample_post(ypart, yoff_bm, expa, proj_s, blk(p_z, width), nw_ssm, groups)
        (xs,) = _merge(xs, proj_s[None], blk(p_gate, 3 * d), y_conv_s[None],
                       y_attn_tm.reshape(1, nt * nbat, aw), y_ssm_s[None], mod_s, (wc, wa, ws, wo))
        k_rows = k_new.transpose(1, 0, 2).reshape(nbat, nt, n_kv, head_dim)
        v_rows = proj_tm[:, :, p_kv + kvw:p_kv + 2 * kvw].transpose(1, 0, 2).reshape(nbat, nt, n_kv, head_dim)
        outs["ks"].append(jnp.concatenate([cache_k[i][:, nt:], k_rows], axis=1))
        outs["vs"].append(jnp.concatenate([cache_v[i][:, nt:], v_rows], axis=1))
        outs["cs"].append(u_s[nt - (w_sconv.shape[1] - 1):].transpose(1, 0, 2))
        kc = ssm_conv_w.shape[1] - 1
        outs["scs"].append(proj_tm[nt - kc:, :, p_xbc:p_xbc + xbcw].transpose(1, 0, 2))

        last = i == depth - 1
        jj = i // 2
        if i % 2 == 0:
            wg, wu, wd = ffn_w_gate[jj], ffn_w_up[jj], ffn_w_down[jj]
            xp = _ffn(xp, mod_p, norm_ffn[i], norm_final, wg, wu, wd, last)
            xs = _ffn(xs, mod_s, norm_ffn[i], norm_final, wg, wu, wd, last)
        else:
            wg, wu, wd = moe_w_gate[jj], moe_w_up[jj], moe_w_down[jj]
            wr = jnp.pad(router[jj], ((0, 0), (0, LANES - n_experts)))
            xp, xs = _moe([(xp, mod_p), (xs, mod_s)], norm_ffn[i], norm_final, wr, wg, wu, wd, last)

    y_sample = xs.reshape(nt, nbat, d).transpose(1, 0, 2)
    st = lambda k: jnp.stack(outs[k])
    return (xp, y_sample, st("kp"), st("vp"), st("cp"), st("scp"), st("sp"),
            st("ks"), st("vs"), st("cs"), st("scs"), new_state_s)
```

```python
import functools
import math

import jax
import jax.numpy as jnp
from jax import lax
from jax.experimental import pallas as pl
from jax.experimental.pallas import tpu as pltpu

F32 = jnp.float32
BF16 = jnp.bfloat16

PAST_LEN = 8192
ROPE_THETA = 500000.0
EPS = 1e-6
TOP_K = 2
SSM_GROUPS = 2
ATTN_BLOCK = 128
ATTN_BLOCKS_PER_STEP = 1
SSD_CHUNK = 128
MOE_ROW_TILE = 1024
ROUTE_TILE = 512

LANES = 128
SUBLANES = 8
VMEM_LIMIT_BYTES = 56 * 1024 * 1024


def _cparams(*semantics):
    return pltpu.CompilerParams(dimension_semantics=semantics, vmem_limit_bytes=VMEM_LIMIT_BYTES)


def _tile(n, pref):
    if n <= pref:
        return n
    t = pref
    while n % t:
        t //= 2
    return t


def _silu(x):
    return x / (1.0 + jnp.exp(-x))


def _sigmoid(x):
    return 1.0 / (1.0 + jnp.exp(-x))


def _softplus(x):
    return jnp.maximum(x, 0.0) + jnp.log1p(jnp.exp(-jnp.abs(x)))


def _dot(a, b):
    return jnp.dot(a, b, preferred_element_type=F32)


def _dot_nt(a, b):
    return lax.dot_general(a, b, (((1,), (1,)), ((), ())), preferred_element_type=F32)


def _dot_tn(a, b):
    return lax.dot_general(a, b, (((0,), (0,)), ((), ())), preferred_element_type=F32)


def _split3(x):
    hi = x.astype(BF16)
    r1 = x - hi.astype(F32)
    mid = r1.astype(BF16)
    lo = (r1 - mid.astype(F32)).astype(BF16)
    return hi, mid, lo


def _dot_exact_rhs01(x, m01):
    hi, mid, lo = _split3(x)
    return _dot(hi, m01) + _dot(mid, m01) + _dot(lo, m01)


def _dot_exact_lhs01(m01, x):
    hi, mid, lo = _split3(x)
    return _dot(m01, hi) + _dot(m01, mid) + _dot(m01, lo)


def _rms_mod(x, norm_w, scale, shift):
    xn = x * lax.rsqrt(jnp.mean(x * x, axis=-1, keepdims=True) + EPS)
    return (xn * norm_w) * (1.0 + scale) + shift


def _mod_kernel(c_ref, w_ref, b_ref, o_ref):
    a = _silu(c_ref[...]).astype(BF16)
    o_ref[0] = _dot(a, w_ref[0].astype(BF16)) + b_ref[0]


def _modulation(c_all, w_mod, b_mod):
    depth, d, n = w_mod.shape
    rows = c_all.shape[0]
    tn = _tile(n, 1024)
    return pl.pallas_call(
        _mod_kernel,
        grid=(depth, n // tn),
        in_specs=[
            pl.BlockSpec((rows, d), lambda i, j: (0, 0)),
            pl.BlockSpec((1, d, tn), lambda i, j: (i, 0, j)),
            pl.BlockSpec((1, 1, tn), lambda i, j: (i, 0, j)),
        ],
        out_specs=pl.BlockSpec((1, rows, tn), lambda i, j: (i, 0, j)),
        out_shape=jax.ShapeDtypeStruct((depth, rows, n), F32),
        compiler_params=_cparams("arbitrary", "arbitrary"),
        name="modulation",
    )(c_all, w_mod, b_mod.reshape(depth, 1, n))


def _mod_spec(mod, tm, chunk, d):
    if mod.shape[1] == 1:
        return pl.BlockSpec((1, 1, d), lambda b, i, *_: (b, 0, chunk))
    return pl.BlockSpec((1, tm, d), lambda b, i, *_: (b, i, chunk))


def _regroup_kernel(w_ref, o_ref, *, pieces, pad):
    dst = 0
    for lo, hi in pieces:
        o_ref[0, :, dst:dst + hi - lo] = w_ref[0, :, lo:hi].astype(BF16)
        dst += hi - lo
    o_ref[0, :, dst:dst + pad] = jnp.zeros((o_ref.shape[1], pad), BF16)


def _regroup_weights(w, pieces, pad):
    depth, d, n = w.shape
    assert sum(hi - lo for lo, hi in pieces) == n
    tr = _tile(d, 256)
    kern = functools.partial(_regroup_kernel, pieces=pieces, pad=pad)
    return pl.pallas_call(
        kern,
        grid=(depth, d // tr),
        in_specs=[pl.BlockSpec((1, tr, n), lambda i, r: (i, r, 0))],
        out_specs=pl.BlockSpec((1, tr, n + pad), lambda i, r: (i, r, 0)),
        out_shape=jax.ShapeDtypeStruct((depth, d, n + pad), BF16),
        compiler_params=_cparams("arbitrary", "arbitrary"),
        name="regroup_w_in",
    )(w)


def _inproj_kernel(x_ref, sh_ref, sc_ref, nw_ref, w_ref, o_ref, h_scr):
    @pl.when(pl.program_id(2) == 0)
    def _():
        h_scr[...] = _rms_mod(x_ref[0], nw_ref[...], sc_ref[0], sh_ref[0]).astype(BF16)

    o_ref[0] = _dot(h_scr[...], w_ref[0])


def _inproj(x, mod, norm_w, w_all, layer):
    nb, l, d = x.shape
    n = w_all.shape[2]
    tm = _tile(l, 1024)
    tn = _tile(n, 2048)
    return pl.pallas_call(
        _inproj_kernel,
        grid=(nb, l // tm, n // tn),
        in_specs=[
            pl.BlockSpec((1, tm, d), lambda b, i, j: (b, i, 0)),
            _mod_spec(mod, tm, 0, d),
            _mod_spec(mod, tm, 1, d),
            pl.BlockSpec((1, d), lambda b, i, j: (0, 0)),
            pl.BlockSpec((1, d, tn), lambda b, i, j: (layer, 0, j)),
        ],
        out_specs=pl.BlockSpec((1, tm, tn), lambda b, i, j: (b, i, j)),
        out_shape=jax.ShapeDtypeStruct((nb, l, n), F32),
        scratch_shapes=[pltpu.VMEM((tm, d), BF16)],
        compiler_params=_cparams("arbitrary", "arbitrary", "arbitrary"),
        name="inproj",
    )(x, mod, mod, norm_w.reshape(1, d), w_all)


def _conv_sample_kernel(p_ref, st_ref, w_ref, y_ref, u_ref, *, nt, nbat, cw, k):
    w = w_ref[...]
    full = [st_ref[j] for j in range(k - 1)]
    gates = []
    for t in range(nt):
        p = p_ref[pl.ds(t * nbat, nbat), :]
        gates.append(p[:, :cw])
        u = p[:, cw:2 * cw] * p[:, 2 * cw:]
        u_ref[t] = u
        full.append(u)
    for t in range(nt):
        acc = w[0:1] * full[t]
        for j in range(1, k):
            acc = acc + w[j:j + 1] * full[t + j]
        y_ref[pl.ds(t * nbat, nbat), :] = gates[t] * acc


def _whole(shape):
    return pl.BlockSpec(shape, lambda i: (0,) * len(shape))


def _conv_sample(proj_s, col_block, state_tm, w, nt, nbat, cw):
    k = w.shape[0]
    rows = nt * nbat
    kern = functools.partial(_conv_sample_kernel, nt=nt, nbat=nbat, cw=cw, k=k)
    return pl.pallas_call(
        kern,
        grid=(1,),
        in_specs=[pl.BlockSpec((rows, 3 * cw), lambda i: (0, col_block)), _whole(state_tm.shape), _whole(w.shape)],
        out_specs=[_whole((rows, cw)), _whole((nt, nbat, cw))],
        out_shape=[
            jax.ShapeDtypeStruct((rows, cw), F32),
            jax.ShapeDtypeStruct((nt, nbat, cw), F32),
        ],
        compiler_params=_cparams("arbitrary"),
        name="conv_sample",
    )(proj_s, state_tm, w)


def _rope(x, cos, sin_lo, sin_hi, half_rot):
    return (x * cos + pltpu.roll(x, LANES - half_rot, 1) * sin_lo
            + pltpu.roll(x, half_rot, 1) * sin_hi)


def _attn_core(q, kcat, vcat, sinks_ref, valid, cos, sin_lo, sin_hi, *, n_heads, group, head_dim):
    tq = q.shape[0]
    half_rot = head_dim // 8
    heads_per_slab = LANES // head_dim
    scale = head_dim ** -0.5
    lane = lax.broadcasted_iota(jnp.int32, (tq, LANES), 1)
    k_bf = [kcat.astype(BF16), pltpu.roll(kcat, head_dim, 1).astype(BF16)]
    v_bf = [vcat.astype(BF16), pltpu.roll(vcat, head_dim, 1).astype(BF16)]
    slabs = []
    for s in range(n_heads // heads_per_slab):
        qs = _rope(q[:, s * LANES:(s + 1) * LANES], cos, sin_lo, sin_hi, half_rot)
        out = jnp.zeros((tq, LANES), F32)
        for half in range(heads_per_slab):
            h = s * heads_per_slab + half
            g = h // group
            in_head = (lane >= half * head_dim) & (lane < (half + 1) * head_dim)
            qm = jnp.where(in_head, qs, 0.0).astype(BF16)
            swap = 0 if (g % heads_per_slab) == half else 1
            sc = _dot_nt(qm, k_bf[swap]) * scale
            sc = jnp.where(valid, sc, -1e30)
            sink = sinks_ref[h]
            m = jnp.maximum(jnp.max(sc, axis=-1, keepdims=True), sink)
            p = jnp.exp(sc - m)
            p = p / (jnp.sum(p, axis=-1, keepdims=True) + jnp.exp(sink - m))
            o = _dot(p.astype(BF16), v_bf[swap])
            out = jnp.where(in_head, o, out)
        slabs.append(out)
    return slabs


def _attn_prompt_kernel(sinks_ref, q_ref, kv_ref, cos_ref, slo_ref, shi_ref, y_ref, last_ref, kprev, vprev,
                        *, n_heads, group, head_dim):
    j = pl.program_id(1)
    tq = kprev.shape[0]

    @pl.when(j == 0)
    def _():
        kprev[...] = jnp.zeros_like(kprev)
        vprev[...] = jnp.zeros_like(vprev)

    r = lax.broadcasted_iota(jnp.int32, (tq, 2 * tq), 0)
    c = lax.broadcasted_iota(jnp.int32, (tq, 2 * tq), 1)
    band = (c >= r) & (c <= r + tq)
    k_prev, v_prev = kprev[...], vprev[...]
    for blk in range(q_ref.shape[1] // tq):
        rows = pl.ds(blk * tq, tq)
        cos, slo, shi = cos_ref[rows, :], slo_ref[rows, :], shi_ref[rows, :]
        kv = kv_ref[0, rows, :]
        k_rot = _rope(kv[:, :LANES], cos, slo, shi, head_dim // 8)
        v = kv[:, LANES:]
        kcat = jnp.concatenate([k_prev, k_rot], axis=0)
        vcat = jnp.concatenate([v_prev, v], axis=0)
        if blk == 0:
            valid = band & (c >= jnp.where(j > 0, 0, tq))
        else:
            valid = band
        slabs = _attn_core(q_ref[0, rows, :], kcat, vcat, sinks_ref, valid, cos, slo, shi,
                           n_heads=n_heads, group=group, head_dim=head_dim)
        for s, o in enumerate(slabs):
            y_ref[0, rows, s * LANES:(s + 1) * LANES] = o
        k_prev, v_prev = k_rot, v
    kprev[...] = k_prev
    vprev[...] = v_prev
    last_ref[0, :, :LANES] = k_prev
    last_ref[0, :, LANES:] = v_prev


def _attn_prompt(proj, sinks, tables, q_block, kv_block, n_heads, n_kv, head_dim):
    nb, l, _ = proj.shape
    tq = ATTN_BLOCK
    qw = n_heads * head_dim
    kvw = 2 * n_kv * head_dim
    assert n_kv * head_dim == LANES
    kern = functools.partial(_attn_prompt_kernel, n_heads=n_heads, group=n_heads // n_kv, head_dim=head_dim)
    ts = _tile(l, ATTN_BLOCKS_PER_STEP * tq)
    tab_spec = pl.BlockSpec((ts, LANES), lambda b, j: (j, 0))
    return pl.pallas_call(
        kern,
        grid=(nb, l // ts),
        in_specs=[
            pl.BlockSpec(memory_space=pltpu.SMEM),
            pl.BlockSpec((1, ts, qw), lambda b, j: (b, j, q_block)),
            pl.BlockSpec((1, ts, kvw), lambda b, j: (b, j, kv_block)),
            tab_spec, tab_spec, tab_spec,
        ],
        out_specs=[
            pl.BlockSpec((1, ts, qw), lambda b, j: (b, j, 0)),
            pl.BlockSpec((1, tq, kvw), lambda b, j: (b, 0, 0)),
        ],
        out_shape=[
            jax.ShapeDtypeStruct((nb, l, qw), F32),
            jax.ShapeDtypeStruct((nb, tq, kvw), F32),
        ],
        scratch_shapes=[pltpu.VMEM((tq, LANES), F32), pltpu.VMEM((tq, LANES), F32)],
        compiler_params=_cparams("arbitrary", "arbitrary"),
        name="attn_prompt",
    )(sinks, proj, proj, *tables)


def _attn_sample_kernel(sinks_ref, q_ref, kv_ref, ck_ref, cv_ref, cos_ref, slo_ref, shi_ref, y_ref, knew_ref,
                        qh, kc, vc, ob, *, n_heads, group, head_dim, nt):
    gb, window = ck_ref.shape[0], ck_ref.shape[1]
    half_rot = head_dim // 8
    heads_per_slab = LANES // head_dim
    nq = n_heads * SUBLANES
    nk = kc.shape[1]
    lane = lax.broadcasted_iota(jnp.int32, (gb, LANES), 1)

    @pl.when(pl.program_id(0) == 0)
    def _():
        qh[...] = jnp.zeros_like(qh)
        kc[...] = jnp.zeros_like(kc)
        vc[...] = jnp.zeros_like(vc)

    kc[:, 0:window, :] = ck_ref[...]
    vc[:, 0:window, :] = cv_ref[...]
    for t in range(nt):
        cos, slo, shi = cos_ref[t:t + 1, :], slo_ref[t:t + 1, :], shi_ref[t:t + 1, :]
        kv = kv_ref[t]
        k_rot = _rope(kv[:, :LANES], cos, slo, shi, half_rot)
        knew_ref[t] = k_rot
        kc[:, window + t, :] = k_rot
        vc[:, window + t, :] = kv[:, LANES:]
        for s in range(n_heads // heads_per_slab):
            qs = _rope(q_ref[t][:, s * LANES:(s + 1) * LANES], cos, slo, shi, half_rot)
            qs_swapped = pltpu.roll(qs, head_dim, 1)
            for half in range(heads_per_slab):
                h = s * heads_per_slab + half
                g = (h // group) % heads_per_slab
                in_kv_half = (lane >= g * head_dim) & (lane < (g + 1) * head_dim)
                qh[:, h * SUBLANES + t, :] = jnp.where(in_kv_half, qs if g == half else qs_swapped, 0.0)

    sc = jnp.einsum("bqd,bkd->bqk", qh[...].astype(BF16), kc[...].astype(BF16),
                    preferred_element_type=F32) * (head_dim ** -0.5)
    r = lax.broadcasted_iota(jnp.int32, (nq, nk), 0) % SUBLANES
    c = lax.broadcasted_iota(jnp.int32, (nq, nk), 1)
    valid = (c >= r) & (c <= r + window)
    sc = jnp.where(valid[None], sc, -1e30)
    row_head = lax.broadcasted_iota(jnp.int32, (nq, 1), 0) // SUBLANES
    sink = jnp.zeros((nq, 1), F32)
    for h in range(n_heads):
        sink = jnp.where(row_head == h, sinks_ref[h], sink)
    m = jnp.maximum(jnp.max(sc, axis=-1, keepdims=True), sink[None])
    p = jnp.exp(sc - m)
    p = p / (jnp.sum(p, axis=-1, keepdims=True) + jnp.exp(sink[None] - m))
    ob[...] = jnp.einsum("bqk,bkd->bqd", p.astype(BF16), vc[...].astype(BF16), preferred_element_type=F32)

    for t in range(nt):
        for s in range(n_heads // heads_per_slab):
            out = jnp.zeros((gb, LANES), F32)
            for half in range(heads_per_slab):
                h = s * heads_per_slab + half
                g = (h // group) % heads_per_slab
                o = ob[:, h * SUBLANES + t, :]
                if g != half:
                    o = pltpu.roll(o, head_dim, 1)
                out = jnp.where((lane >= half * head_dim) & (lane < (half + 1) * head_dim), o, out)
            y_ref[t, :, s * LANES:(s + 1) * LANES] = out


def _attn_sample(proj_tm, ck, cv, sinks, tables, q_block, kv_block, n_heads, n_kv, head_dim):
    nt, nbat, _ = proj_tm.shape
    window = ck.shape[1]
    qw = n_heads * head_dim
    assert n_kv * head_dim == LANES and nt <= SUBLANES
    gb = _tile(nbat, 16)
    nk = window + 2 * SUBLANES
    kern = functools.partial(_attn_sample_kernel, n_heads=n_heads, group=n_heads // n_kv, head_dim=head_dim, nt=nt)
    tab_spec = pl.BlockSpec((SUBLANES, LANES), lambda b: (0, 0))
    return pl.pallas_call(
        kern,
        grid=(nbat // gb,),
        in_specs=[
            pl.BlockSpec(memory_space=pltpu.SMEM),
            pl.BlockSpec((nt, gb, qw), lambda b: (0, b, q_block)),
            pl.BlockSpec((nt, gb, 2 * LANES), lambda b: (0, b, kv_block)),
            pl.BlockSpec((gb, window, LANES), lambda b: (b, 0, 0)),
            pl.BlockSpec((gb, window, LANES), lambda b: (b, 0, 0)),
            tab_spec, tab_spec, tab_spec,
        ],
        out_specs=[
            pl.BlockSpec((nt, gb, qw), lambda b: (0, b, 0)),
            pl.BlockSpec((nt, gb, LANES), lambda b: (0, b, 0)),
        ],
        out_shape=[
            jax.ShapeDtypeStruct((nt, nbat, qw), F32),
            jax.ShapeDtypeStruct((nt, nbat, LANES), F32),
        ],
        scratch_shapes=[
            pltpu.VMEM((gb, n_heads * SUBLANES, LANES), F32),
            pltpu.VMEM((gb, nk, LANES), F32), pltpu.VMEM((gb, nk, LANES), F32),
            pltpu.VMEM((gb, n_heads * SUBLANES, LANES), F32),
        ],
        compiler_params=_cparams("arbitrary"),
        name="attn_sample",
    )(sinks, proj_tm, proj_tm, ck, cv, *tables)


def _rope_tables(pos, head_dim):
    rot = head_dim // 4
    half = rot // 2
    inv = jnp.exp(-(2.0 * jnp.arange(half, dtype=F32) / rot) * math.log(ROPE_THETA))
    ang = pos.astype(F32)[:, None] * inv[None, :]
    cos, sin = jnp.cos(ang), jnp.sin(ang)
    n = pos.shape[0]
    pad = jnp.zeros((n, head_dim - rot), F32)
    zeros = jnp.zeros((n, half), F32)
    cos_h = jnp.concatenate([cos, cos, pad + 1.0], axis=1)
    lo_h = jnp.concatenate([-sin, zeros, pad], axis=1)
    hi_h = jnp.concatenate([zeros, sin, pad], axis=1)
    reps = LANES // head_dim
    return tuple(jnp.tile(t, (1, reps)) for t in (cos_h, lo_h, hi_h))


def _gated_group_norm(y, z, norm_w, groups):
    y = y * _silu(z)
    gw = y.shape[1] // groups
    parts = []
    for g in range(groups):
        yg = y[:, g * gw:(g + 1) * gw]
        parts.append(yg * lax.rsqrt(jnp.mean(yg * yg, axis=-1, keepdims=True) + EPS))
    return jnp.concatenate(parts, axis=1) * norm_w


def _ssd_prompt_kernel(xbc_ref, z_ref, dt_ref, cw_ref, cb_ref, dtb_ref, alog_ref, dsk_ref, nw_ref, e_ref,
                       y_ref, hfin_ref, ext, ht, *, q, width, dstate, hdim, groups):
    j = pl.program_id(1)
    kconv = cw_ref.shape[0]

    @pl.when(j == 0)
    def _():
        ext[0:SUBLANES, :] = jnp.zeros((SUBLANES, ext.shape[1]), F32)
        ht[...] = jnp.zeros_like(ht)

    xbc = xbc_ref[0]
    ext[SUBLANES:, :] = xbc
    cw = cw_ref[...]
    conv = cw[kconv - 1:kconv] * xbc + cb_ref[...]
    for t in range(kconv - 1):
        conv = conv + cw[t:t + 1] * ext[pl.ds(SUBLANES - (kconv - 1) + t, q), :]
    ext[0:SUBLANES, :] = xbc[q - SUBLANES:, :]
    act = _silu(conv)
    xs = act[:, :width]
    bm = act[:, width:width + groups * dstate]
    cm = act[:, width + groups * dstate:]

    expand = e_ref[...]
    nh = expand.shape[0]
    dt_h = _softplus(dt_ref[0] + dtb_ref[...])
    a_h = dt_h * (-jnp.exp(alog_ref[...]))
    row = lax.broadcasted_iota(jnp.int32, (q, q), 0)
    col = lax.broadcasted_iota(jnp.int32, (q, q), 1)
    causal = col <= row
    tri = jnp.where(causal, 1.0, 0.0).astype(BF16)
    acum_h = _dot_exact_lhs01(tri, a_h)
    acum_ht = acum_h.T
    dt_x = _dot_exact_rhs01(dt_h[:, :nh], expand)
    acum_x = _dot_exact_rhs01(acum_h[:, :nh], expand)
    xdt = xs * dt_x
    acum_last = acum_x[q - 1:q, :]
    xd = xdt * jnp.exp(acum_last - acum_x)
    chunk_decay = jnp.exp(acum_last)
    exp_acum = jnp.exp(acum_x)

    lane = lax.broadcasted_iota(jnp.int32, (q, LANES), 1)
    gw = width // groups
    heads_per_group = gw // hdim
    pair = LANES // hdim
    y_parts = []
    for g in range(groups):
        b_g = bm[:, g * dstate:(g + 1) * dstate].astype(BF16)
        c_g = cm[:, g * dstate:(g + 1) * dstate].astype(BF16)
        cbm = _dot_nt(c_g, b_g)
        h_g = ht[:, g * gw:(g + 1) * gw]
        y_off = _dot(c_g, h_g.astype(BF16)) * exp_acum[:, g * gw:(g + 1) * gw]
        diag_parts = []
        for jp in range(heads_per_group // pair):
            l0 = g * gw + jp * LANES
            x_pair = xdt[:, l0:l0 + LANES].astype(BF16)
            out = jnp.zeros((q, LANES), F32)
            for half in range(pair):
                hd = g * heads_per_group + jp * pair + half
                decay = jnp.exp(jnp.where(causal, acum_h[:, hd:hd + 1] - acum_ht[hd:hd + 1, :], -jnp.inf))
                res = _dot((cbm * decay).astype(BF16), x_pair)
                in_head = (lane >= half * hdim) & (lane < (half + 1) * hdim)
                out = jnp.where(in_head, res, out)
            diag_parts.append(out)
        y_parts.append(jnp.concatenate(diag_parts, axis=1) + y_off)
        s_t = _dot_tn(b_g, xd[:, g * gw:(g + 1) * gw].astype(BF16))
        ht[:, g * gw:(g + 1) * gw] = h_g * chunk_decay[:, g * gw:(g + 1) * gw] + s_t
    y = jnp.concatenate(y_parts, axis=1) + xs * dsk_ref[...]
    y_ref[0] = _gated_group_norm(y, z_ref[0], nw_ref[...], groups)

    @pl.when(j == pl.num_programs(1) - 1)
    def _():
        hfin_ref[0] = ht[...].T


def _ssd_prompt(proj, blocks, params, dims):
    nb, l, _ = proj.shape
    q = SSD_CHUNK
    width, dstate, hdim, groups, xbcw = dims
    xbc_block, z_block, dt_block = blocks
    cw, cb, dtb, alog, dsk, nw, expand = params
    kern = functools.partial(_ssd_prompt_kernel, q=q, width=width, dstate=dstate, hdim=hdim, groups=groups)
    full = lambda a: pl.BlockSpec(a.shape, lambda b, j: (0,) * a.ndim)
    return pl.pallas_call(
        kern,
        grid=(nb, l // q),
        in_specs=[
            pl.BlockSpec((1, q, xbcw), lambda b, j: (b, j, xbc_block)),
            pl.BlockSpec((1, q, width), lambda b, j: (b, j, z_block)),
            pl.BlockSpec((1, q, LANES), lambda b, j: (b, j, dt_block)),
            full(cw), full(cb), full(dtb), full(alog), full(dsk), full(nw), full(expand),
        ],
        out_specs=[
            pl.BlockSpec((1, q, width), lambda b, j: (b, j, 0)),
            pl.BlockSpec((1, width, dstate), lambda b, j: (b, 0, 0)),
        ],
        out_shape=[
            jax.ShapeDtypeStruct((nb, l, width), F32),
            jax.ShapeDtypeStruct((nb, width, dstate), F32),
        ],
        scratch_shapes=[pltpu.VMEM((q + SUBLANES, xbcw), F32), pltpu.VMEM((dstate, width), F32)],
        compiler_params=_cparams("arbitrary", "arbitrary"),
        name="ssd_prompt",
    )(proj, proj, proj, cw, cb, dtb, alog, dsk, nw, expand)


def _ssd_sample_pre_kernel(xbc_ref, dt_ref, st_ref, cw_ref, cb_ref, dtb_ref, alog_ref, dsk_ref, e_ref,
                           ypart_ref, expa_ref, xd_ref, b_ref, c_ref, cd_ref,
                           *, nt, nbat, width, dstate, groups):
    kconv = cw_ref.shape[0]
    cw = cw_ref[...]
    expand = e_ref[...]
    nh = expand.shape[0]
    neg_a = -jnp.exp(alog_ref[...])
    full = [st_ref[t] for t in range(kconv - 1)]
    for t in range(nt):
        full.append(xbc_ref[pl.ds(t * nbat, nbat), :])
    xs, bm, cm, dt_x, acum_x, xdt = [], [], [], [], [], []
    acum_h = None
    for t in range(nt):
        conv = cb_ref[...] + cw[0:1] * full[t]
        for jj in range(1, kconv):
            conv = conv + cw[jj:jj + 1] * full[t + jj]
        act = _silu(conv)
        xs.append(act[:, :width])
        bm.append(act[:, width:width + groups * dstate])
        cm.append(act[:, width + groups * dstate:])
        dt_h = _softplus(dt_ref[pl.ds(t * nbat, nbat), :] + dtb_ref[...])
        a_h = dt_h * neg_a
        acum_h = a_h if acum_h is None else acum_h + a_h
        dt_x.append(_dot_exact_rhs01(dt_h[:, :nh], expand))
        acum_x.append(_dot_exact_rhs01(acum_h[:, :nh], expand))
        xdt.append(xs[t] * dt_x[t])
    cd_ref[...] = jnp.exp(acum_h)
    gw = width // groups
    for t in range(nt):
        y = xs[t] * dsk_ref[...]
        for s in range(t + 1):
            cb_parts = []
            for g in range(groups):
                prod = cm[t][:, g * dstate:(g + 1) * dstate] * bm[s][:, g * dstate:(g + 1) * dstate]
                cb_parts.append(jnp.broadcast_to(jnp.sum(prod, axis=-1, keepdims=True), (nbat, gw)))
            cb_x = jnp.concatenate(cb_parts, axis=1)
            y = y + cb_x * jnp.exp(acum_x[t] - acum_x[s]) * xdt[s]
        ypart_ref[t] = y
        expa_ref[t] = jnp.exp(acum_x[t])
    xd_ref[...] = jnp.zeros_like(xd_ref)
    b_ref[...] = jnp.zeros_like(b_ref)
    c_ref[...] = jnp.zeros_like(c_ref)
    for t in range(nt):
        xd_ref[:, t, :] = xdt[t] * jnp.exp(acum_x[nt - 1] - acum_x[t])
        b_ref[:, t, :] = bm[t]
        c_ref[:, t, :] = cm[t]


def _ssd_sample_pre(proj_s, blocks, state_tm, params, nt, nbat, dims, srows):
    width, dstate, hdim, groups, xbcw = dims
    xbc_block, dt_block = blocks
    cw, cb, dtb, alog, dsk, expand = params
    rows = nt * nbat
    kern = functools.partial(_ssd_sample_pre_kernel, nt=nt, nbat=nbat, width=width, dstate=dstate, groups=groups)
    sd = jax.ShapeDtypeStruct
    out_shapes = [(nt, nbat, width), (nt, nbat, width), (nbat, srows, width),
                  (nbat, srows, groups * dstate), (nbat, srows, groups * dstate), (nbat, LANES)]
    return pl.pallas_call(
        kern,
        grid=(1,),
        in_specs=[pl.BlockSpec((rows, xbcw), lambda i: (0, xbc_block)),
                  pl.BlockSpec((rows, LANES), lambda i: (0, dt_block)),
                  _whole(state_tm.shape)] + [_whole(a.shape) for a in params],
        out_specs=[_whole(s) for s in out_shapes],
        out_shape=[sd(s, F32) for s in out_shapes],
        compiler_params=_cparams("arbitrary"),
        name="ssd_sample_pre",
    )(proj_s, proj_s, state_tm, cw, cb, dtb, alog, dsk, expand)


def _ssd_sample_state_kernel(cd_ref, c_ref, b_ref, xd_ref, h0_ref, *rest, heads, hdim, dstate, groups):
    yoff_ref, hnew_ref = rest[-2:]
    gb = h0_ref.shape[1]
    hpg = heads // groups
    gw = hpg * hdim
    for i in range(gb):
        b = pl.program_id(0) * gb + i
        for g in range(groups):
            hm = h0_ref[0, i, g * hpg:(g + 1) * hpg].reshape(gw, dstate)
            c_g = c_ref[i, :, g * dstate:(g + 1) * dstate].astype(BF16)
            b_g = b_ref[i, :, g * dstate:(g + 1) * dstate].astype(BF16)
            yoff_ref[i, :, g * gw:(g + 1) * gw] = _dot_nt(c_g, hm.astype(BF16))
            upd = _dot_tn(xd_ref[i, :, g * gw:(g + 1) * gw].astype(BF16), b_g)
            for hh in range(hpg):
                hd = g * hpg + hh
                hnew_ref[0, i, hd] = (h0_ref[0, i, hd] * cd_ref[b * heads + hd]
                                      + upd[hh * hdim:(hh + 1) * hdim, :])
    for later in range(1, hnew_ref.shape[0]):
        hnew_ref[later] = jnp.zeros(hnew_ref.shape[1:], F32)


def _ssd_sample_state(cd_flat, c_bm, b_bm, xd_bm, state_all, layer, stacked, dims):
    width, dstate, hdim, groups, _ = dims
    depth, nbat, heads = state_all.shape[:3]
    rows = c_bm.shape[1]
    gb = _tile(nbat, 8)
    kern = functools.partial(_ssd_sample_state_kernel, heads=heads, hdim=hdim, dstate=dstate, groups=groups)
    in_specs = [
        pl.BlockSpec(memory_space=pltpu.SMEM),
        pl.BlockSpec((gb, rows, groups * dstate), lambda b: (b, 0, 0)),
        pl.BlockSpec((gb, rows, groups * dstate), lambda b: (b, 0, 0)),
        pl.BlockSpec((gb, rows, width), lambda b: (b, 0, 0)),
        pl.BlockSpec((1, gb, heads, hdim, dstate), lambda b: (layer, b, 0, 0, 0)),
    ]
    args = [cd_flat, c_bm, b_bm, xd_bm, state_all]
    if layer == 0:
        assert stacked is None
        state_spec = pl.BlockSpec((depth, gb, heads, hdim, dstate), lambda b: (0, b, 0, 0, 0))
        aliases = {}
    else:
        in_specs.append(pl.BlockSpec(memory_space=pl.ANY))
        args.append(stacked)
        state_spec = pl.BlockSpec((1, gb, heads, hdim, dstate), lambda b: (layer, b, 0, 0, 0))
        aliases = {len(args) - 1: 1}
    return pl.pallas_call(
        kern,
        grid=(nbat // gb,),
        in_specs=in_specs,
        out_specs=[pl.BlockSpec((gb, rows, width), lambda b: (b, 0, 0)), state_spec],
        out_shape=[
            jax.ShapeDtypeStruct((nbat, rows, width), F32),
            jax.ShapeDtypeStruct(state_all.shape, F32),
        ],
        input_output_aliases=aliases,
        compiler_params=_cparams("arbitrary"),
        name="ssd_sample_state",
    )(*args)


def _ssd_sample_post_kernel(ypart_ref, yoff_ref, expa_ref, z_ref, nw_ref, y_ref, *, groups, nt, nbat):
    for t in range(nt):
        rows = pl.ds(t * nbat, nbat)
        y = ypart_ref[t] + yoff_ref[:, t, :] * expa_ref[t]
        y_ref[rows, :] = _gated_group_norm(y, z_ref[rows, :], nw_ref[...], groups)


def _ssd_sample_post(ypart, yoff_bm, expa, proj_s, z_block, nw, groups):
    nt, nbat, width = ypart.shape
    rows = nt * nbat
    kern = functools.partial(_ssd_sample_post_kernel, groups=groups, nt=nt, nbat=nbat)
    return pl.pallas_call(
        kern,
        grid=(1,),
        in_specs=[_whole(ypart.shape), _whole(yoff_bm.shape), _whole(expa.shape),
                  pl.BlockSpec((rows, width), lambda i: (0, z_block)), _whole(nw.shape)],
        out_specs=_whole((rows, width)),
        out_shape=jax.ShapeDtypeStruct((rows, width), F32),
        compiler_params=_cparams("arbitrary"),
        name="ssd_sample_post",
    )(ypart, yoff_bm, expa, proj_s, nw)


def _merge_math(x, gates, y_conv, y_attn, y_ssm, gate1, wc_ref, wa_ref, ws_ref, wo_ref, d):
    merged = (_sigmoid(gates[:, :d]) * _dot(y_conv.astype(BF16), wc_ref[...])
              + _sigmoid(gates[:, d:2 * d]) * _dot(y_attn.astype(BF16), wa_ref[...])
              + _sigmoid(gates[:, 2 * d:]) * _dot(y_ssm.astype(BF16), ws_ref[...]))
    return x + gate1 * _dot(merged.astype(BF16), wo_ref[...])


def _merge_kernel(*refs, d, tm, cw, conv):
    refs = list(refs)
    x_ref, g_ref = refs[:2]
    del refs[:2]
    if conv:
        p_ref, cw_ref = refs[:2]
        del refs[:2]
    else:
        yc_ref = refs.pop(0)
    ya_ref, ys_ref, g1_ref, wc_ref, wa_ref, ws_ref, wo_ref = refs[:7]
    del refs[:7]
    o_ref = refs.pop(0)

    if conv:
        st_ref, ext = refs

        @pl.when(pl.program_id(1) == 0)
        def _():
            ext[0:SUBLANES, :] = jnp.zeros((SUBLANES, cw), F32)

        p = p_ref[0]
        u = p[:, cw:2 * cw] * p[:, 2 * cw:]
        ext[SUBLANES:, :] = u
        w = cw_ref[...]
        acc = w[0:1] * ext[pl.ds(SUBLANES - 2, tm), :] + w[1:2] * ext[pl.ds(SUBLANES - 1, tm), :] + w[2:3] * u
        tail = u[tm - SUBLANES:, :]
        ext[0:SUBLANES, :] = tail
        st_ref[0] = tail
        y_conv = p[:, :cw] * acc
    else:
        y_conv = yc_ref[0]

    o_ref[0] = _merge_math(x_ref[0], g_ref[0], y_conv, ya_ref[0], ys_ref[0], g1_ref[0],
                           wc_ref, wa_ref, ws_ref, wo_ref, d)


def _merge(x, proj, gate_block, conv_src, ya, ys, mod, weights):
    nb, l, d = x.shape
    tm = _tile(l, ROUTE_TILE)
    nt = l // tm
    conv = isinstance(conv_src, tuple)
    tok = lambda w: pl.BlockSpec((1, tm, w), lambda b, i: (b, i, 0))
    full = lambda a: pl.BlockSpec(a.shape, lambda b, i: (0, 0))
    in_specs = [tok(d), pl.BlockSpec((1, tm, 3 * d), lambda b, i: (b, i, gate_block))]
    args = [x, proj]
    out_specs = [tok(d)]
    out_shape = [jax.ShapeDtypeStruct((nb, l, d), F32)]
    scratch = []
    if conv:
        conv_block, w_conv = conv_src
        cw = w_conv.shape[1]
        assert w_conv.shape[0] == 3
        in_specs += [pl.BlockSpec((1, tm, 3 * cw), lambda b, i: (b, i, conv_block)), full(w_conv)]
        args += [proj, w_conv]
        out_specs.append(pl.BlockSpec((1, SUBLANES, cw), lambda b, i: (b, 0, 0)))
        out_shape.append(jax.ShapeDtypeStruct((nb, SUBLANES, cw), F32))
        scratch.append(pltpu.VMEM((tm + SUBLANES, cw), F32))
    else:
        cw = conv_src.shape[2]
        in_specs.append(tok(cw))
        args.append(conv_src)
    in_specs += [tok(ya.shape[2]), tok(ys.shape[2]), _mod_spec(mod, tm, 2, d)] + [full(w) for w in weights]
    args += [ya, ys, mod] + list(weights)
    kern = functools.partial(_merge_kernel, d=d, tm=tm, cw=cw, conv=conv)
    return pl.pallas_call(
        kern,
        grid=(nb, nt),
        in_specs=in_specs,
        out_specs=out_specs,
        out_shape=out_shape,
        scratch_shapes=scratch,
        compiler_params=_cparams("arbitrary", "arbitrary"),
        name="merge",
    )(*args)


def _finish(x, gate, f, nf_ref, final_norm):
    out = x + gate * f
    if final_norm:
        out = out * lax.rsqrt(jnp.mean(out * out, axis=-1, keepdims=True) + EPS) * nf_ref[...]
    return out


def _ffn_kernel(x_ref, sh_ref, sc_ref, g2_ref, nw_ref, nf_ref, wg_ref, wu_ref, wd_ref, o_ref, h_scr, acc,
                *, final_norm):
    f = pl.program_id(2)

    @pl.when(f == 0)
    def _():
        h_scr[...] = _rms_mod(x_ref[0], nw_ref[...], sc_ref[0], sh_ref[0]).astype(BF16)
        acc[...] = jnp.zeros_like(acc)

    h = h_scr[...]
    a = _silu(_dot(h, wg_ref[...].astype(BF16))) * _dot(h, wu_ref[...].astype(BF16))
    acc[...] += _dot(a.astype(BF16), wd_ref[...].astype(BF16))

    @pl.when(f == pl.num_programs(2) - 1)
    def _():
        o_ref[0] = _finish(x_ref[0], g2_ref[0], acc[...], nf_ref, final_norm)


def _ffn(x, mod, norm_w, norm_final, wg, wu, wd, final_norm):
    nb, l, d = x.shape
    ff = wg.shape[1]
    tm = _tile(l, 1024)
    tf = _tile(ff, 512)
    kern = functools.partial(_ffn_kernel, final_norm=final_norm)
    vec = pl.BlockSpec((1, d), lambda b, i, f: (0, 0))
    return pl.pallas_call(
        kern,
        grid=(nb, l // tm, ff // tf),
        in_specs=[
            pl.BlockSpec((1, tm, d), lambda b, i, f: (b, i, 0)),
            _mod_spec(mod, tm, 3, d), _mod_spec(mod, tm, 4, d), _mod_spec(mod, tm, 5, d),
            vec, vec,
            pl.BlockSpec((d, tf), lambda b, i, f: (0, f)),
            pl.BlockSpec((d, tf), lambda b, i, f: (0, f)),
            pl.BlockSpec((tf, d), lambda b, i, f: (f, 0)),
        ],
        out_specs=pl.BlockSpec((1, tm, d), lambda b, i, f: (b, i, 0)),
        out_shape=jax.ShapeDtypeStruct((nb, l, d), F32),
        scratch_shapes=[pltpu.VMEM((tm, d), BF16), pltpu.VMEM((tm, d), F32)],
        compiler_params=_cparams("arbitrary", "arbitrary", "arbitrary"),
        name="ffn",
    )(x, mod, mod, mod, norm_w.reshape(1, d), norm_final.reshape(1, d), wg, wu, wd)


def _route_math(x, norm_w, scale, shift, wr, n_experts):
    h = _rms_mod(x, norm_w, scale, shift)
    h_hi = h.astype(BF16)
    h_lo = (h - h_hi.astype(F32)).astype(BF16)
    r_hi = wr.astype(BF16)
    r_lo = (wr - r_hi.astype(F32)).astype(BF16)
    logits = _dot(h_hi, r_hi) + _dot(h_lo, r_hi) + _dot(h_hi, r_lo)
    lane = lax.broadcasted_iota(jnp.int32, logits.shape, 1).astype(F32)
    neg = -jnp.inf
    lg = jnp.where(lane < n_experts, logits, neg)
    m1 = jnp.max(lg, axis=-1, keepdims=True)
    i1 = jnp.min(jnp.where(lg == m1, lane, float(LANES)), axis=-1, keepdims=True)
    rest = jnp.where(lane == i1, neg, lg)
    m2 = jnp.max(rest, axis=-1, keepdims=True)
    i2 = jnp.min(jnp.where(rest == m2, lane, float(LANES)), axis=-1, keepdims=True)
    e2 = jnp.exp(m2 - m1)
    w1 = 1.0 / (1.0 + e2)
    w2 = e2 / (1.0 + e2)
    w_cols = jnp.where(lane == 0.0, w1, jnp.where(lane == 1.0, w2, 0.0))
    chosen = jnp.where(lane == 0.0, i1, jnp.where(lane == 1.0, i2, 0.0))
    return w_cols, chosen.T[:SUBLANES, :]


def _route_kernel(x_ref, sh_ref, sc_ref, nw_ref, wr_ref, w_ref, e_ref, *, n_experts):
    w_cols, e_rows = _route_math(x_ref[0], nw_ref[...], sc_ref[0], sh_ref[0], wr_ref[...], n_experts)
    w_ref[...] = w_cols
    e_ref[...] = e_rows


def _route(x, mod, norm_w, w_router_pad, n_experts):
    nb, l, d = x.shape
    tm = _tile(l, ROUTE_TILE)
    nt = l // tm
    kern = functools.partial(_route_kernel, n_experts=n_experts)
    return pl.pallas_call(
        kern,
        grid=(nb, nt),
        in_specs=[
            pl.BlockSpec((1, tm, d), lambda b, i: (b, i, 0)),
            _mod_spec(mod, tm, 3, d), _mod_spec(mod, tm, 4, d),
            pl.BlockSpec((1, d), lambda b, i: (0, 0)),
            pl.BlockSpec((d, LANES), lambda b, i: (0, 0)),
        ],
        out_specs=[
            pl.BlockSpec((tm, LANES), lambda b, i: (b * nt + i, 0)),
            pl.BlockSpec((SUBLANES, tm), lambda b, i: (0, b * nt + i)),
        ],
        out_shape=[
            jax.ShapeDtypeStruct((nb * l, LANES), F32),
            jax.ShapeDtypeStruct((SUBLANES, nb * l), F32),
        ],
        compiler_params=_cparams("arbitrary", "arbitrary"),
        name="moe_route",
    )(x, mod, mod, norm_w.reshape(1, d), w_router_pad)


def _plan_kernel(e_ref, pos_ref, te_ref, *, n_experts, tile, row_tile):
    steps = e_ref.shape[1] // tile
    sub = lax.broadcasted_iota(jnp.int32, (SUBLANES, tile), 0).astype(F32)
    sub_col = lax.broadcasted_iota(jnp.int32, (SUBLANES, 1), 0)

    def member(i):
        blk = e_ref[:, pl.ds(pl.multiple_of(i * tile, tile), tile)]
        e1, e2 = blk[0:1, :], blk[1:2, :]
        return e1, e2, jnp.where((sub == e1) | (sub == e2), 1.0, 0.0)

    def count_body(i, cnt):
        return cnt + jnp.sum(member(i)[2], axis=1, keepdims=True)

    cnt = lax.fori_loop(0, steps, count_body, jnp.zeros((SUBLANES, 1), F32))
    padded = jnp.floor((cnt + (row_tile - 1)) * (1.0 / row_tile)) * row_tile
    off = jnp.zeros((SUBLANES, 1), F32)
    run = jnp.zeros((1, 1), F32)
    for e in range(n_experts):
        off = jnp.where(sub_col == e, run, off)
        run = run + padded[e:e + 1, :]
    seg_end = off + padded

    r = lax.broadcasted_iota(jnp.int32, (tile, tile), 0)
    c = lax.broadcasted_iota(jnp.int32, (tile, tile), 1)
    before = jnp.where(r < c, 1.0, 0.0).astype(BF16)

    def pos_body(i, carry):
        e1, e2, m = member(i)
        val = off + carry + _dot(m.astype(BF16), before)
        p1 = jnp.sum(jnp.where(sub == e1, val, 0.0), axis=0, keepdims=True)
        p2 = jnp.sum(jnp.where(sub == e2, val, 0.0), axis=0, keepdims=True)
        rows = jnp.where(sub == 0.0, p1, jnp.where(sub == 1.0, p2, 0.0))
        pos_ref[:, pl.ds(pl.multiple_of(i * tile, tile), tile)] = rows.astype(jnp.int32)
        return carry + jnp.sum(m, axis=1, keepdims=True)

    lax.fori_loop(0, steps, pos_body, jnp.zeros((SUBLANES, 1), F32))

    sub_l = lax.broadcasted_iota(jnp.int32, (SUBLANES, LANES), 0)
    start = lax.broadcasted_iota(jnp.int32, (SUBLANES, LANES), 1).astype(F32) * row_tile
    owner = jnp.sum(jnp.where((seg_end <= start) & (sub_l < n_experts), 1.0, 0.0), axis=0, keepdims=True)
    owner = jnp.minimum(owner, n_experts - 1.0)
    used = run * (1.0 / row_tile)
    te_ref[...] = jnp.where(sub_l == 0, owner, jnp.where(sub_l == 1, used, 0.0)).astype(jnp.int32)


def _plan(e_all, n_experts, row_tile):
    t = e_all.shape[1]
    assert t % LANES == 0 and n_experts <= SUBLANES
    tile = _tile(t, ROUTE_TILE)
    kern = functools.partial(_plan_kernel, n_experts=n_experts, tile=tile, row_tile=row_tile)
    return pl.pallas_call(
        kern,
        out_shape=[
            jax.ShapeDtypeStruct((SUBLANES, t), jnp.int32),
            jax.ShapeDtypeStruct((SUBLANES, LANES), jnp.int32),
        ],
        compiler_params=pltpu.CompilerParams(vmem_limit_bytes=VMEM_LIMIT_BYTES),
        name="moe_plan",
    )(e_all)


def _scatter_kernel(p1_ref, p2_ref, x_ref, sh_ref, sc_ref, nw_ref, xs_in_ref, xs_ref, h_scr, sem, *, tm):
    del xs_in_ref
    h_scr[...] = _rms_mod(x_ref[0], nw_ref[...], sc_ref[0], sh_ref[0]).reshape(h_scr.shape)

    def row_copy(i, k, p):
        return pltpu.make_async_copy(h_scr.at[i, pl.ds(k, 1), :], xs_ref.at[pl.ds(p, 1), :], sem)

    def issue(i, carry):
        for k in range(SUBLANES):
            t = i * SUBLANES + k
            row_copy(i, k, p1_ref[t]).start(priority=0)
            row_copy(i, k, p2_ref[t]).start(priority=1)
        return carry

    def drain(t, carry):
        row_copy(0, 0, 0).wait()
        row_copy(0, 0, 0).wait()
        return carry

    lax.fori_loop(0, tm // SUBLANES, issue, 0)
    lax.fori_loop(0, tm, drain, 0, unroll=SUBLANES)


def _scatter(x, mod, norm_w, p1, p2, xs_sorted):
    nb, l, d = x.shape
    tm = _tile(l, ROUTE_TILE)
    nt = l // tm
    kern = functools.partial(_scatter_kernel, tm=tm)
    idx = pl.BlockSpec((tm,), lambda b, i: (b * nt + i,), memory_space=pltpu.SMEM)
    return pl.pallas_call(
        kern,
        grid=(nb, nt),
        in_specs=[
            idx, idx,
            pl.BlockSpec((1, tm, d), lambda b, i: (b, i, 0)),
            _mod_spec(mod, tm, 3, d), _mod_spec(mod, tm, 4, d),
            pl.BlockSpec((1, d), lambda b, i: (0, 0)),
            pl.BlockSpec(memory_space=pl.ANY),
        ],
        out_specs=pl.BlockSpec(memory_space=pl.ANY),
        out_shape=jax.ShapeDtypeStruct(xs_sorted.shape, F32),
        scratch_shapes=[pltpu.VMEM((tm // SUBLANES, SUBLANES, d), F32), pltpu.SemaphoreType.DMA],
        input_output_aliases={6: 0},
        compiler_params=_cparams("arbitrary", "arbitrary"),
        name="moe_scatter",
    )(p1, p2, x, mod, mod, norm_w.reshape(1, d), xs_sorted)


def _group_ffn_kernel(te_ref, xs_ref, wg_ref, wu_ref, wd_ref, y_ref, h_scr, acc):
    j = pl.program_id(0)
    f = pl.program_id(1)

    @pl.when(j < te_ref[LANES])
    def _():
        @pl.when(f == 0)
        def _():
            h_scr[...] = xs_ref[...].astype(BF16)
            acc[...] = jnp.zeros_like(acc)

        h = h_scr[...]
        a = _silu(_dot(h, wg_ref[0].astype(BF16))) * _dot(h, wu_ref[0].astype(BF16))
        acc[...] += _dot(a.astype(BF16), wd_ref[0].astype(BF16))

        @pl.when(f == pl.num_programs(1) - 1)
        def _():
            y_ref[...] = acc[...]

    @pl.when((j >= te_ref[LANES]) & (f == 0))
    def _():
        y_ref[...] = jnp.zeros_like(y_ref)


def _group_ffn(te_flat, xs_sorted, wg, wu, wd, row_tile):
    rows, d = xs_sorted.shape
    ff = wg.shape[2]
    tf = _tile(ff, 512)
    nf = ff // tf

    def tile_of(j, te):
        return jnp.minimum(j, te[LANES] - 1)

    def f_of(j, f, te):
        return jnp.where(j < te[LANES], f, nf - 1)

    grid_spec = pltpu.PrefetchScalarGridSpec(
        num_scalar_prefetch=1,
        grid=(rows // row_tile, nf),
        in_specs=[
            pl.BlockSpec((row_tile, d), lambda j, f, te: (tile_of(j, te), 0)),
            pl.BlockSpec((1, d, tf), lambda j, f, te: (te[tile_of(j, te)], 0, f_of(j, f, te))),
            pl.BlockSpec((1, d, tf), lambda j, f, te: (te[tile_of(j, te)], 0, f_of(j, f, te))),
            pl.BlockSpec((1, tf, d), lambda j, f, te: (te[tile_of(j, te)], f_of(j, f, te), 0)),
        ],
        out_specs=pl.BlockSpec((row_tile, d), lambda j, f, te: (j, 0)),
        scratch_shapes=[pltpu.VMEM((row_tile, d), BF16), pltpu.VMEM((row_tile, d), F32)],
    )
    return pl.pallas_call(
        _group_ffn_kernel,
        grid_spec=grid_spec,
        out_shape=jax.ShapeDtypeStruct((rows, d), F32),
        compiler_params=_cparams("arbitrary", "arbitrary"),
        name="moe_group_ffn",
    )(te_flat, xs_sorted, wg, wu, wd)


def _combine_kernel(p1_ref, p2_ref, x_ref, g2_ref, w_ref, nf_ref, y_hbm, o_ref, buf, sems,
                    *, tm, nt, n_steps, final_norm):
    step = pl.program_id(0) * nt + pl.program_id(1)

    def row_copy(slot, choice, i, k, p):
        return pltpu.make_async_copy(y_hbm.at[pl.ds(p, 1), :], buf.at[slot, choice, i, pl.ds(k, 1), :],
                                     sems.at[slot])

    def start_tile(tile, slot):
        def issue(i, carry):
            for k in range(SUBLANES):
                t = tile * tm + i * SUBLANES + k
                row_copy(slot, 0, i, k, p1_ref[t]).start(priority=0)
                row_copy(slot, 1, i, k, p2_ref[t]).start(priority=1)
            return carry
        lax.fori_loop(0, tm // SUBLANES, issue, 0)

    def wait_tile(slot):
        def drain(t, carry):
            row_copy(slot, 0, 0, 0, 0).wait()
            row_copy(slot, 0, 0, 0, 0).wait()
            return carry
        lax.fori_loop(0, tm, drain, 0, unroll=SUBLANES)

    @pl.when(step == 0)
    def _():
        start_tile(0, 0)

    for slot in range(2):
        @pl.when(step % 2 == slot)
        def _(slot=slot):
            @pl.when(step + 1 < n_steps)
            def _():
                start_tile(step + 1, 1 - slot)

            wait_tile(slot)
            w = w_ref[...]
            d = buf.shape[-1]
            f = w[:, 0:1] * buf[slot, 0].reshape(tm, d) + w[:, 1:2] * buf[slot, 1].reshape(tm, d)
            o_ref[0] = _finish(x_ref[0], g2_ref[0], f, nf_ref, final_norm)


def _combine(x, mod, w_cols, p1, p2, y_sorted, norm_final, final_norm):
    nb, l, d = x.shape
    tm = _tile(l, ROUTE_TILE)
    nt = l // tm
    kern = functools.partial(_combine_kernel, tm=tm, nt=nt, n_steps=nb * nt, final_norm=final_norm)
    grid_spec = pltpu.PrefetchScalarGridSpec(
        num_scalar_prefetch=2,
        grid=(nb, nt),
        in_specs=[
            pl.BlockSpec((1, tm, d), lambda b, i, *_: (b, i, 0)),
            _mod_spec(mod, tm, 5, d),
            pl.BlockSpec((tm, LANES), lambda b, i, *_: (b * nt + i, 0)),
            pl.BlockSpec((1, d), lambda b, i, *_: (0, 0)),
            pl.BlockSpec(memory_space=pl.ANY),
        ],
        out_specs=pl.BlockSpec((1, tm, d), lambda b, i, *_: (b, i, 0)),
        scratch_shapes=[pltpu.VMEM((2, 2, tm // SUBLANES, SUBLANES, d), F32), pltpu.SemaphoreType.DMA((2,))],
    )
    return pl.pallas_call(
        kern,
        grid_spec=grid_spec,
        out_shape=jax.ShapeDtypeStruct((nb, l, d), F32),
        compiler_params=_cparams("arbitrary", "arbitrary"),
        name="moe_combine",
    )(p1, p2, x, mod, w_cols, norm_final.reshape(1, d), y_sorted)


def _moe(groups, norm_w, norm_final, w_router_pad, wg, wu, wd, final_norm):
    n_experts = wg.shape[0]
    d = groups[0][0].shape[2]
    routed = [_route(x, mod, norm_w, w_router_pad, n_experts) for x, mod in groups]
    e_all = jnp.concatenate([e for _, e in routed], axis=1)
    t = e_all.shape[1]
    pos, te = _plan(e_all, n_experts, MOE_ROW_TILE)
    te_flat = te.reshape(SUBLANES * LANES)
    n_tiles = -(-TOP_K * t // MOE_ROW_TILE) + n_experts
    assert n_tiles <= LANES
    xs_sorted = jnp.zeros((n_tiles * MOE_ROW_TILE, d), F32)
    spans, start = [], 0
    for x, _ in groups:
        n = x.shape[0] * x.shape[1]
        spans.append((start, start + n))
        start += n
    for (x, mod), (lo, hi) in zip(groups, spans):
        xs_sorted = _scatter(x, mod, norm_w, pos[0, lo:hi], pos[1, lo:hi], xs_sorted)
    y_sorted = _group_ffn(te_flat, xs_sorted, wg, wu, wd, MOE_ROW_TILE)
    return [_combine(x, mod, w_cols, pos[0, lo:hi], pos[1, lo:hi], y_sorted, norm_final, final_norm)
            for (x, mod), (w_cols, _), (lo, hi) in zip(groups, routed, spans)]


def kernel(x_prompt, x_sample, c_prompt, c_sample, cache_k, cache_v, state_conv, state_ssm_conv, state_ssm,
           w_mod, b_mod, norm_mix, norm_ffn, norm_final, w_in, w_sconv, sinks, ssm_conv_w, ssm_conv_b,
           dt_bias, a_log, d_skip, ssm_norm, w_br_conv, w_br_attn, w_br_ssm, w_o,
           ffn_w_gate, ffn_w_up, ffn_w_down, router, moe_w_gate, moe_w_up, moe_w_down):
    nbp, seq, d = x_prompt.shape
    nbat, nt, _ = x_sample.shape
    depth = w_mod.shape[0]
    cwid = w_sconv.shape[2]
    n_heads = sinks.shape[1]
    window, n_kv, head_dim = cache_k.shape[2:]
    heads, hdim, dstate = state_ssm.shape[2:]
    width = heads * hdim
    xbcw = ssm_conv_w.shape[2]
    groups = SSM_GROUPS
    n_experts = router.shape[2]
    aw = n_heads * head_dim
    kvw = n_kv * head_dim
    dims = (width, dstate, hdim, groups, xbcw)
    assert window == ATTN_BLOCK and seq % ATTN_BLOCK == 0 and xbcw == width + 2 * groups * dstate

    o_cv, o_q, o_k, o_z, o_xbc = 0, 3 * cwid, 3 * cwid + aw, 3 * cwid + aw + 2 * kvw, 3 * cwid + aw + 2 * kvw + width
    o_dt = o_xbc + xbcw
    o_g = o_dt + heads
    n_in = w_in.shape[2]
    dt_pad = 2 * LANES - heads

    pieces = ((o_g, n_in), (o_cv, o_q), (o_xbc, o_dt), (o_z, o_xbc), (o_q, o_k), (o_k, o_z), (o_dt, o_g))
    w_in_regrouped = _regroup_weights(w_in, pieces, dt_pad)

    p_gate, p_conv, p_xbc, p_z = 0, 3 * d, 3 * d + 3 * cwid, 3 * d + 3 * cwid + xbcw
    p_q = p_z + width
    p_kv = p_q + aw
    p_dt = p_kv + 2 * kvw
    blk = lambda off, w: off // w
    assert all(off % w == 0 for off, w in ((p_conv, 3 * cwid), (p_xbc, xbcw), (p_z, width), (p_q, aw),
                                           (p_kv, 2 * kvw), (p_dt, LANES)))

    n_c = nbp + nbat
    c_rows = -(-n_c // SUBLANES) * SUBLANES
    c_all = jnp.pad(jnp.concatenate([c_prompt, c_sample], axis=0), ((0, c_rows - n_c), (0, 0)))
    mod_all = _modulation(c_all, w_mod, b_mod)

    xs_tm = x_sample.transpose(1, 0, 2).reshape(1, nt * nbat, d)
    srows = 2 * SUBLANES

    pos_p = jnp.arange(seq, dtype=jnp.int32)
    pos_s = PAST_LEN + jnp.arange(SUBLANES, dtype=jnp.int32)
    tab_p = _rope_tables(pos_p, head_dim)
    tab_s = _rope_tables(pos_s, head_dim)

    expand = jnp.repeat(jnp.eye(heads, dtype=F32), hdim, axis=1).astype(BF16)
    pad_h = lambda v: jnp.pad(v, (0, LANES - heads)).reshape(1, LANES)

    xp, xs = x_prompt, xs_tm
    outs = {k: [] for k in ("kp", "vp", "cp", "scp", "sp", "ks", "vs", "cs", "scs")}
    new_state_s = None
    for i in range(depth):
        wc, wa, ws, wo = (w[i].astype(BF16) for w in (w_br_conv, w_br_attn, w_br_ssm, w_o))
        mod_p = mod_all[i, :nbp].reshape(nbp, 1, 6 * d)
        mod_s = jnp.tile(mod_all[i, nbp:n_c], (nt, 1)).reshape(1, nt * nbat, 6 * d)
        ssm_params = (ssm_conv_w[i], ssm_conv_b[i].reshape(1, xbcw), pad_h(dt_bias[i]), pad_h(a_log[i]),
                      jnp.repeat(d_skip[i], hdim).reshape(1, width))
        nw_ssm = ssm_norm[i].reshape(1, width)

        proj = _inproj(xp, mod_p, norm_mix[i], w_in_regrouped, i)
        y_attn, kv_last = _attn_prompt(proj, sinks[i], tab_p, blk(p_q, aw), blk(p_kv, 2 * kvw),
                                       n_heads, n_kv, head_dim)
        y_ssm, h_fin = _ssd_prompt(proj, (blk(p_xbc, xbcw), blk(p_z, width), blk(p_dt, LANES)),
                                   ssm_params + (nw_ssm, expand), dims)
        xp, conv_tail = _merge(xp, proj, blk(p_gate, 3 * d), (blk(p_conv, 3 * cwid), w_sconv[i]),
                               y_attn, y_ssm, mod_p, (wc, wa, ws, wo))
        outs["kp"].append(kv_last[:, :, :kvw].reshape(nbp, window, n_kv, head_dim))
        outs["vp"].append(kv_last[:, :, kvw:].reshape(nbp, window, n_kv, head_dim))
        outs["cp"].append(conv_tail[:, SUBLANES - (w_sconv.shape[1] - 1):])
        outs["scp"].append(proj[:, seq - (ssm_conv_w.shape[1] - 1):, p_xbc:p_xbc + xbcw])
        outs["sp"].append(h_fin.reshape(nbp, heads, hdim, dstate))

        proj_s = _inproj(xs, mod_s, norm_mix[i], w_in_regrouped, i)[0]
        proj_tm = proj_s.reshape(nt, nbat, proj_s.shape[1])
        y_conv_s, u_s = _conv_sample(proj_s, blk(p_conv, 3 * cwid), state_conv[i].transpose(1, 0, 2),
                                     w_sconv[i], nt, nbat, cwid)
        y_attn_tm, k_new = _attn_sample(proj_tm, cache_k[i].reshape(nbat, window, kvw),
                                        cache_v[i].reshape(nbat, window, kvw), sinks[i], tab_s,
                                        blk(p_q, aw), blk(p_kv, 2 * kvw), n_heads, n_kv, head_dim)
        ypart, expa, xd_bm, b_bm, c_bm, cd = _ssd_sample_pre(
            proj_s, (blk(p_xbc, xbcw), blk(p_dt, LANES)), state_ssm_conv[i].transpose(1, 0, 2),
            ssm_params + (expand,), nt, nbat, dims, srows)
        yoff_bm, new_state_s = _ssd_sample_state(cd[:, :heads].reshape(nbat * heads), c_bm, b_bm, xd_bm,
                                                 state_ssm, i, new_state_s, dims)
        y_ssm_s = _ssd_sample_post(ypart, yoff_bm, expa, proj_s, blk(p_z, width), nw_ssm, groups)
        (xs,) = _merge(xs, proj_s[None], blk(p_gate, 3 * d), y_conv_s[None],
                       y_attn_tm.reshape(1, nt * nbat, aw), y_ssm_s[None], mod_s, (wc, wa, ws, wo))
        k_rows = k_new.transpose(1, 0, 2).reshape(nbat, nt, n_kv, head_dim)
        v_rows = proj_tm[:, :, p_kv + kvw:p_kv + 2 * kvw].transpose(1, 0, 2).reshape(nbat, nt, n_kv, head_dim)
        outs["ks"].append(jnp.concatenate([cache_k[i][:, nt:], k_rows], axis=1))
        outs["vs"].append(jnp.concatenate([cache_v[i][:, nt:], v_rows], axis=1))
        outs["cs"].append(u_s[nt - (w_sconv.shape[1] - 1):].transpose(1, 0, 2))
        kc = ssm_conv_w.shape[1] - 1
        outs["scs"].append(proj_tm[nt - kc:, :, p_xbc:p_xbc + xbcw].transpose(1, 0, 2))

        last = i == depth - 1
        jj = i // 2
        if i % 2 == 0:
            wg, wu, wd = ffn_w_gate[jj], ffn_w_up[jj], ffn_w_down[jj]
            xp = _ffn(xp, mod_p, norm_ffn[i], norm_final, wg, wu, wd, last)
            xs = _ffn(xs, mod_s, norm_ffn[i], norm_final, wg, wu, wd, last)
        else:
            wg, wu, wd = moe_w_gate[jj], moe_w_up[jj], moe_w_down[jj]
            wr = jnp.pad(router[jj], ((0, 0), (0, LANES - n_experts)))
            xp, xs = _moe([(xp, mod_p), (xs, mod_s)], norm_ffn[i], norm_final, wr, wg, wu, wd, last)

    y_sample = xs.reshape(nt, nbat, d).transpose(1, 0, 2)
    st = lambda k: jnp.stack(outs[k])
    return (xp, y_sample, st("kp"), st("vp"), st("cp"), st("scp"), st("sp"),
            st("ks"), st("vs"), st("cs"), st("scs"), new_state_s)
```
